```python
import jax, jax.numpy as jnp
from jax import lax
import numpy as np

D_MODEL = 2048
BATCH = 4
SEQ = 2048
DEPTH = 1
DEC_BATCH = 128
DEC_SEQ = 8
PAST_LEN = 16384
PAGE_SIZE = 128

D_RNN = D_MODEL
N_LRU_BLOCKS = 8
LRU_BLOCK = D_RNN // N_LRU_BLOCKS
CONV_W = 4
LRU_C = 8.0
N_RET_HEADS = 8
RET_DK = D_MODEL // N_RET_HEADS
RET_V_EXPAND = 2
RET_DV = RET_V_EXPAND * D_MODEL // N_RET_HEADS
QK_DIM = N_RET_HEADS * RET_DK
V_DIM = N_RET_HEADS * RET_DV
RET_CHUNK = 64
ROPE_BASE = 10000.0
D_FF = 5632
PLE_DIM = 256
EPS = 1e-6
IN_COLS = 2 * D_RNN + 2 * QK_DIM + 2 * V_DIM + 2 * D_MODEL

kernel_name = "hybrid_rglru_retention_macaron_step"


def rmsnorm(x, g):
    x32 = x.astype(jnp.float32)
    return x32 * lax.rsqrt(jnp.mean(x32 * x32, axis=-1, keepdims=True) + EPS) * g


def swiglu(u, wg, wu, wd):
    return jnp.einsum('btf,fd->btd', jax.nn.silu(jnp.einsum('btd,df->btf', u, wg)) * jnp.einsum('btd,df->btf', u, wu), wd)


def causal_conv(xb, buf, w, b):
    T = xb.shape[1]
    xp = jnp.concatenate([buf.astype(jnp.float32), xb], axis=1)
    y = b + sum(xp[:, j:j + T] * w[j] for j in range(CONV_W))
    return y, xp[:, T:]


def _lin_combine(e1, e2):
    a1, b1 = e1
    a2, b2 = e2
    return a1 * a2, a2 * b1 + b2


def rg_lru(xc, h0, wa, ba, wx, bx, lam, reset_first):
    B, T, _ = xc.shape
    xb = xc.reshape(B, T, N_LRU_BLOCKS, LRU_BLOCK)
    r = jax.nn.sigmoid(jnp.einsum('btnc,ncd->btnd', xb, wa).reshape(B, T, D_RNN) + ba)
    gi = jax.nn.sigmoid(jnp.einsum('btnc,ncd->btnd', xb, wx).reshape(B, T, D_RNN) + bx)
    log_a = -LRU_C * r * jax.nn.softplus(-lam.astype(jnp.float32))
    a = jnp.exp(log_a)
    mult = jnp.sqrt(-jnp.expm1(2.0 * log_a))
    if reset_first:
        mult = mult.at[:, 0].set(1.0)
    bterm = mult * (gi * xc)
    bterm = bterm.at[:, 0].add(a[:, 0] * h0.astype(jnp.float32))
    _, h = lax.associative_scan(_lin_combine, (a, bterm), axis=1)
    return h, h[:, -1]


def rope(x, pos):
    half = x.shape[-1] // 2
    inv = ROPE_BASE ** (-jnp.arange(half, dtype=jnp.float32) / half)
    ang = pos.astype(jnp.float32)[:, None] * inv
    cos = jnp.cos(ang)[None, :, None, :]
    sin = jnp.sin(ang)[None, :, None, :]
    x1, x2 = x[..., :half], x[..., half:]
    return jnp.concatenate([x1 * cos - x2 * sin, x2 * cos + x1 * sin], axis=-1)


def retention(q, k, v, s0):
    B, T, H, _ = q.shape
    chunk = RET_CHUNK if T % RET_CHUNK == 0 else T
    nc = T // chunk

    def to_chunks(a):
        return a.reshape(B, nc, chunk, H, a.shape[-1]).transpose(1, 0, 3, 2, 4)

    log_g = jnp.log1p(-jnp.exp2(-5.0 - jnp.arange(H, dtype=jnp.float32)))
    idx = jnp.arange(chunk, dtype=jnp.float32)
    diff = idx[:, None] - idx[None, :]
    dmask = jnp.where(diff >= 0, jnp.exp(jnp.maximum(diff, 0.0)[None] * log_g[:, None, None]), 0.0)
    cross_decay = jnp.exp((idx[None] + 1.0) * log_g[:, None])[..., None]
    state_decay = jnp.exp((chunk - 1.0 - idx[None]) * log_g[:, None])[..., None]
    chunk_decay = jnp.exp(chunk * log_g)[:, None, None]

    def step(S, inp):
        qc, kc, vc = inp
        scores = jnp.einsum('bhik,bhjk->bhij', qc, kc) * dmask
        inner = jnp.einsum('bhij,bhjv->bhiv', scores, vc)
        cross = jnp.einsum('bhik,bhkv->bhiv', qc, S) * cross_decay
        S_new = S * chunk_decay + jnp.einsum('bhjk,bhjv->bhkv', kc * state_decay, vc)
        return S_new, inner + cross

    s_last, outs = lax.scan(step, s0.astype(jnp.float32), (to_chunks(q), to_chunks(k), to_chunks(v)))
    o = outs.transpose(1, 0, 3, 2, 4).reshape(B, T, H, v.shape[-1])
    return o, s_last


def decoder_layer(x, pe, h0, conv_buf, s0, pos0, reset_first, lp):
    B, T, _ = x.shape
    x = x + 0.5 * swiglu(rmsnorm(x, lp['ffn1_norm']), lp['ffn1_wg'], lp['ffn1_wu'], lp['ffn1_wd'])
    u = rmsnorm(x, lp['mix_norm'])
    z = jnp.einsum('btd,dc->btc', u, lp['w_in']) + lp['b_in']
    cuts = [D_RNN, 2 * D_RNN, 2 * D_RNN + QK_DIM, 2 * D_RNN + 2 * QK_DIM, 2 * D_RNN + 2 * QK_DIM + V_DIM,
            2 * D_RNN + 2 * QK_DIM + 2 * V_DIM, 2 * D_RNN + 2 * QK_DIM + 2 * V_DIM + D_MODEL]
    xa, ga, q, k, v, gr, gate_a, gate_b = jnp.split(z, cuts, axis=-1)
    xc, conv_new = causal_conv(xa, conv_buf, lp['conv_w'], lp['conv_b'])
    ha, h_last = rg_lru(xc, h0, lp['lru_wa'], lp['lru_ba'], lp['lru_wx'], lp['lru_bx'], lp['lru_lambda'], reset_first)
    oa = ha * jax.nn.gelu(ga)
    pos = pos0 + jnp.arange(T)
    qh = rope(q.reshape(B, T, N_RET_HEADS, RET_DK), pos)
    kh = rope(k.reshape(B, T, N_RET_HEADS, RET_DK), pos) * (RET_DK ** -0.5)
    vh = v.reshape(B, T, N_RET_HEADS, RET_DV)
    ob, s_new = retention(qh, kh, vh, s0)
    ob = rmsnorm(ob, lp['ret_norm']).reshape(B, T, V_DIM) * jax.nn.silu(gr)
    merged = (jax.nn.sigmoid(gate_a) * jnp.einsum('btc,cd->btd', oa, lp['proj_a'])
              + jax.nn.sigmoid(gate_b) * jnp.einsum('btc,cd->btd', ob, lp['proj_b']))
    x = x + jnp.einsum('btd,de->bte', merged, lp['w_out'])
    x = x + 0.5 * swiglu(rmsnorm(x, lp['ffn2_norm']), lp['ffn2_wg'], lp['ffn2_wu'], lp['ffn2_wd'])
    gate = jax.nn.sigmoid(jnp.einsum('btd,de->bte', rmsnorm(x, lp['ple_norm']), lp['ple_wg']) + lp['ple_bg'])
    x = x + gate * jnp.einsum('btp,pd->btd', pe.astype(jnp.float32), lp['ple_proj'])
    return x, h_last, conv_new, s_new


def setup_inputs(seed: int = 0) -> dict:
    key = jax.random.key(seed)
    ks = jax.random.split(key, 40)
    f32 = jnp.float32

    def w(k, shape, fan_in, scale=1.0):
        return jax.random.normal(k, shape, f32) * (scale * fan_in ** -0.5)

    def gain(k, shape):
        return 1.0 + 0.05 * jax.random.normal(k, shape, f32)

    def bias(k, shape):
        return 0.02 * jax.random.normal(k, shape, f32)

    a0 = jax.random.uniform(ks[20], (DEPTH, D_RNN), f32, minval=0.9, maxval=0.999)
    s = a0 ** (1.0 / LRU_C)
    lam = jnp.log(s) - jnp.log1p(-s)
    return {
        'x_prompt': jax.random.normal(ks[0], (BATCH, SEQ, D_MODEL), f32),
        'x_sample': jax.random.normal(ks[1], (DEC_BATCH, DEC_SEQ, D_MODEL), f32),
        'p_prompt': jax.random.normal(ks[2], (DEPTH, BATCH, SEQ, PLE_DIM), f32),
        'p_sample': jax.random.normal(ks[3], (DEPTH, DEC_BATCH, DEC_SEQ, PLE_DIM), f32),
        'state_lru': 0.5 * jax.random.normal(ks[4], (DEPTH, DEC_BATCH, D_RNN), f32),
        'state_conv': jax.random.normal(ks[5], (DEPTH, DEC_BATCH, CONV_W - 1, D_RNN), f32),
        'state_ret': 0.5 * jax.random.normal(ks[6], (DEPTH, DEC_BATCH, N_RET_HEADS, RET_DK, RET_DV), f32),
        'ffn1_norm': gain(ks[7], (DEPTH, D_MODEL)),
        'ffn1_wg': w(ks[8], (DEPTH, D_MODEL, D_FF), D_MODEL),
        'ffn1_wu': w(ks[9], (DEPTH, D_MODEL, D_FF), D_MODEL),
        'ffn1_wd': w(ks[10], (DEPTH, D_FF, D_MODEL), D_FF, 0.5),
        'mix_norm': gain(ks[11], (DEPTH, D_MODEL)),
        'w_in': w(ks[12], (DEPTH, D_MODEL, IN_COLS), D_MODEL),
        'b_in': bias(ks[13], (DEPTH, IN_COLS)),
        'conv_w': w(ks[14], (DEPTH, CONV_W, D_RNN), CONV_W),
        'conv_b': bias(ks[15], (DEPTH, D_RNN)),
        'lru_wa': w(ks[16], (DEPTH, N_LRU_BLOCKS, LRU_BLOCK, LRU_BLOCK), LRU_BLOCK),
        'lru_ba': bias(ks[17], (DEPTH, D_RNN)),
        'lru_wx': w(ks[18], (DEPTH, N_LRU_BLOCKS, LRU_BLOCK, LRU_BLOCK), LRU_BLOCK),
        'lru_bx': bias(ks[19], (DEPTH, D_RNN)),
        'lru_lambda': lam,
        'ret_norm': gain(ks[21], (DEPTH, N_RET_HEADS, RET_DV)),
        'proj_a': w(ks[22], (DEPTH, D_RNN, D_MODEL), D_RNN),
        'proj_b': w(ks[23], (DEPTH, V_DIM, D_MODEL), V_DIM),
        'w_out': w(ks[24], (DEPTH, D_MODEL, D_MODEL), D_MODEL, 0.5),
        'ffn2_norm': gain(ks[25], (DEPTH, D_MODEL)),
        'ffn2_wg': w(ks[26], (DEPTH, D_MODEL, D_FF), D_MODEL),
        'ffn2_wu': w(ks[27], (DEPTH, D_MODEL, D_FF), D_MODEL),
        'ffn2_wd': w(ks[28], (DEPTH, D_FF, D_MODEL), D_FF, 0.5),
        'ple_norm': gain(ks[29], (DEPTH, D_MODEL)),
        'ple_wg': w(ks[30], (DEPTH, D_MODEL, D_MODEL), D_MODEL),
        'ple_bg': bias(ks[31], (DEPTH, D_MODEL)),
        'ple_proj': w(ks[32], (DEPTH, PLE_DIM, D_MODEL), PLE_DIM, 0.5),
        'final_norm': gain(ks[33], (D_MODEL,)),
    }


def reference(x_prompt, x_sample, p_prompt, p_sample, state_lru, state_conv, state_ret,
              ffn1_norm, ffn1_wg, ffn1_wu, ffn1_wd, mix_norm, w_in, b_in, conv_w, conv_b,
              lru_wa, lru_ba, lru_wx, lru_bx, lru_lambda, ret_norm, proj_a, proj_b, w_out,
              ffn2_norm, ffn2_wg, ffn2_wu, ffn2_wd, ple_norm, ple_wg, ple_bg, ple_proj, final_norm):
    xp = x_prompt.astype(jnp.float32)
    xs = x_sample.astype(jnp.float32)
    bp = x_prompt.shape[0]
    lru_p, conv_p, ret_p, lru_s, conv_s, ret_s = [], [], [], [], [], []
    for i in range(DEPTH):
        lp = {
            'ffn1_norm': ffn1_norm[i], 'ffn1_wg': ffn1_wg[i], 'ffn1_wu': ffn1_wu[i], 'ffn1_wd': ffn1_wd[i],
            'mix_norm': mix_norm[i], 'w_in': w_in[i], 'b_in': b_in[i], 'conv_w': conv_w[i], 'conv_b': conv_b[i],
            'lru_wa': lru_wa[i], 'lru_ba': lru_ba[i], 'lru_wx': lru_wx[i], 'lru_bx': lru_bx[i],
            'lru_lambda': lru_lambda[i], 'ret_norm': ret_norm[i], 'proj_a': proj_a[i], 'proj_b': proj_b[i],
            'w_out': w_out[i], 'ffn2_norm': ffn2_norm[i], 'ffn2_wg': ffn2_wg[i], 'ffn2_wu': ffn2_wu[i],
            'ffn2_wd': ffn2_wd[i], 'ple_norm': ple_norm[i], 'ple_wg': ple_wg[i], 'ple_bg': ple_bg[i],
            'ple_proj': ple_proj[i],
        }
        xp, hp, cp, sp = decoder_layer(
            xp, p_prompt[i], jnp.zeros((bp, D_RNN), jnp.float32),
            jnp.zeros((bp, CONV_W - 1, D_RNN), jnp.float32),
            jnp.zeros((bp, N_RET_HEADS, RET_DK, RET_DV), jnp.float32), 0, True, lp)
        xs, hs, cs, ss = decoder_layer(xs, p_sample[i], state_lru[i], state_conv[i], state_ret[i], PAST_LEN, False, lp)
        lru_p.append(hp); conv_p.append(cp); ret_p.append(sp)
        lru_s.append(hs); conv_s.append(cs); ret_s.append(ss)
    y_prompt = rmsnorm(xp, final_norm).astype(x_prompt.dtype)
    y_sample = rmsnorm(xs, final_norm).astype(x_sample.dtype)
    new_lru_prompt = jnp.stack(lru_p).astype(state_lru.dtype)
    new_conv_prompt = jnp.stack(conv_p).astype(state_conv.dtype)
    new_ret_prompt = jnp.stack(ret_p).astype(state_ret.dtype)
    new_lru_sample = jnp.stack(lru_s).astype(state_lru.dtype)
    new_conv_sample = jnp.stack(conv_s).astype(state_conv.dtype)
    new_ret_sample = jnp.stack(ret_s).astype(state_ret.dtype)
    return (y_prompt, y_sample, new_lru_prompt, new_conv_prompt, new_ret_prompt, new_lru_sample, new_conv_sample, new_ret_sample)
```

```python
import functools

import jax
import jax.numpy as jnp
from jax import lax
from jax.experimental import pallas as pl
from jax.experimental.pallas import tpu as pltpu

f32 = jnp.float32
bf16 = jnp.bfloat16

N_LRU_BLOCKS = 8
CONV_W = 4
LRU_C = 8.0
N_RET_HEADS = 8
ROPE_BASE = 10000.0
EPS = 1e-6
PAST_LEN = 16384

RET_CHUNK = 256
SAMPLE_PAIR = 2
VMEM_LIMIT = 56 * 1024 * 1024


def _cparams(sem):
    return pltpu.CompilerParams(dimension_semantics=sem, vmem_limit_bytes=VMEM_LIMIT)


def _sigmoid(x):
    return 1.0 / (1.0 + jnp.exp(-x))


def _rms(x, g):
    return x * lax.rsqrt(jnp.mean(x * x, axis=-1, keepdims=True) + EPS) * g


def _dot(a, b):
    return jnp.dot(a, b, preferred_element_type=f32)


def _dot_nt(a, b):
    return lax.dot_general(a, b, (((1,), (1,)), ((), ())), preferred_element_type=f32)


def _dot_tn(a, b):
    return lax.dot_general(a, b, (((0,), (0,)), ((), ())), preferred_element_type=f32)


def _rope_table_kernel(inv_ref, cos_ref, sin_ref, *, seq, dec_seq):
    rows = cos_ref.shape[0]
    r = lax.broadcasted_iota(jnp.int32, (rows, inv_ref.shape[1]), 0)
    pos = jnp.where(r < seq, r, PAST_LEN + lax.rem(r - seq, dec_seq))
    ang = pos.astype(f32) * inv_ref[...]
    cos_ref[...] = jnp.cos(ang)
    sin_ref[...] = jnp.sin(ang)


def _rope_table(seq, dec_seq, n_sample_rows, half):
    inv = (ROPE_BASE ** (-jnp.arange(half, dtype=f32) / half)).reshape(1, half)
    rows = seq + n_sample_rows
    return pl.pallas_call(
        functools.partial(_rope_table_kernel, seq=seq, dec_seq=dec_seq),
        out_shape=(jax.ShapeDtypeStruct((rows, half), f32), jax.ShapeDtypeStruct((rows, half), f32)),
        name="rope_table",
    )(inv)


def _ffn_kernel(x_ref, g_ref, wg_ref, wu_ref, wd_ref, *rest, emit_norm):
    if emit_norm:
        g2_ref, xo_ref, u2_ref, u_scr, acc_scr = rest
    else:
        xo_ref, u_scr, acc_scr = rest
    f = pl.program_id(1)

    @pl.when(f == 0)
    def _():
        u_scr[...] = _rms(x_ref[...], g_ref[...]).astype(bf16)
        acc_scr[...] = jnp.zeros_like(acc_scr)

    u = u_scr[...]
    hg = _dot(u, wg_ref[...])
    hu = _dot(u, wu_ref[...])
    h = (hg * _sigmoid(hg) * hu).astype(bf16)
    acc_scr[...] += _dot(h, wd_ref[...])

    @pl.when(f == pl.num_programs(1) - 1)
    def _():
        xo = x_ref[...] + 0.5 * acc_scr[...]
        xo_ref[...] = xo
        if emit_norm:
            u2_ref[...] = _rms(xo, g2_ref[...]).astype(bf16)


def _ffn(x, g, wg, wu, wd, g2=None, *, tm=512, tf=512):
    m, d = x.shape
    ff = wg.shape[1]
    emit_norm = g2 is not None
    row = lambda i, f: (i, 0)
    in_specs = [
        pl.BlockSpec((tm, d), row),
        pl.BlockSpec((1, d), lambda i, f: (0, 0)),
        pl.BlockSpec((d, tf), lambda i, f: (0, f)),
        pl.BlockSpec((d, tf), lambda i, f: (0, f)),
        pl.BlockSpec((tf, d), lambda i, f: (f, 0)),
    ]
    args = [x, g.reshape(1, d), wg, wu, wd]
    out_shape = [jax.ShapeDtypeStruct((m, d), f32)]
    out_specs = [pl.BlockSpec((tm, d), row)]
    if emit_norm:
        in_specs.append(pl.BlockSpec((1, d), lambda i, f: (0, 0)))
        args.append(g2.reshape(1, d))
        out_shape.append(jax.ShapeDtypeStruct((m, d), bf16))
        out_specs.append(pl.BlockSpec((tm, d), row))
    out = pl.pallas_call(
        functools.partial(_ffn_kernel, emit_norm=emit_norm),
        grid=(m // tm, ff // tf),
        in_specs=in_specs,
        out_specs=out_specs,
        out_shape=out_shape,
        scratch_shapes=[pltpu.VMEM((tm, d), bf16), pltpu.VMEM((tm, d), f32)],
        compiler_params=_cparams(("parallel", "arbitrary")),
        name="ffn",
    )(*args)
    return out if emit_norm else out[0]


def _mm_kernel(a_ref, b_ref, *rest, epilogue):
    *extra, o_ref = rest
    o_ref[...] = epilogue(_dot(a_ref[...], b_ref[...]), *extra).astype(o_ref.dtype)


def _mm(a, b, *, col0, n_cols, out_dtype, epilogue, extras=(), tm=1024, tn=512, name="mm"):
    m, k = a.shape
    off = col0 // tn
    in_specs = [
        pl.BlockSpec((tm, k), lambda j, i: (i, 0)),
        pl.BlockSpec((k, tn), lambda j, i: (0, j + off)),
    ]
    args = [a, b]
    for arr, blk, imap in extras:
        in_specs.append(pl.BlockSpec(blk, imap))
        args.append(arr)
    return pl.pallas_call(
        functools.partial(_mm_kernel, epilogue=epilogue),
        grid=(n_cols // tn, m // tm),
        in_specs=in_specs,
        out_specs=pl.BlockSpec((tm, tn), lambda j, i: (i, j)),
        out_shape=jax.ShapeDtypeStruct((m, n_cols), out_dtype),
        compiler_params=_cparams(("parallel", "parallel")),
        name=name,
    )(*args)


def _gelu_tanh(x):
    return 0.5 * x * (1.0 + jnp.tanh(0.7978845608028654 * (x + 0.044715 * (x * x * x))))


def _rope_epilogue(acc, bias_ref, cos_ref, sin_ref, *, k_tile0, k_scale, head_dim):
    z = acc + bias_ref[...]
    cos = cos_ref[...]
    sin = sin_ref[...]
    half = head_dim // 2
    parts = []
    for h0 in range(0, z.shape[1], head_dim):
        x1 = z[:, h0:h0 + half]
        x2 = z[:, h0 + half:h0 + head_dim]
        parts += [x1 * cos - x2 * sin, x2 * cos + x1 * sin]
    scale = jnp.where(pl.program_id(0) >= k_tile0, k_scale, 1.0).astype(f32)
    return jnp.concatenate(parts, axis=1) * scale


def _lru_gates(xc, n, cs, wa_ref, wx_ref, ba_ref, bx_ref, lam_ref):
    xcb = xc.astype(bf16)
    r = _sigmoid(_dot(xcb, wa_ref[n]) + ba_ref[:, cs])
    gi = _sigmoid(_dot(xcb, wx_ref[n]) + bx_ref[:, cs])
    log_a = -LRU_C * r * jax.nn.softplus(-lam_ref[:, cs])
    a = jnp.exp(log_a)
    a2 = a * a
    one_minus = jnp.where(log_a < -0.25, 1.0 - a2, -jnp.tanh(log_a) * (a2 + 1.0))
    return a, jnp.sqrt(one_minus), gi


def _lru_prompt_kernel(xa_ref, gg_ref, cw_ref, cb_ref, wa_ref, wx_ref, ba_ref, bx_ref, lam_ref,
                       oa_ref, hl_ref, cn_ref, xe_scr, a_scr, b_scr, h_scr, hc_scr):
    tt, c = xa_ref.shape
    blk = c // N_LRU_BLOCKS
    hist = 8
    t = pl.program_id(1)

    @pl.when(t == 0)
    def _():
        xe_scr[0:hist, :] = jnp.zeros((hist, c), f32)
        hc_scr[...] = jnp.zeros_like(hc_scr)

    xe_scr[hist:hist + tt, :] = xa_ref[...]
    pos = lax.broadcasted_iota(jnp.int32, (tt, 1), 0) + t * tt
    for n in range(N_LRU_BLOCKS):
        cs = slice(n * blk, (n + 1) * blk)
        xc = cb_ref[:, cs]
        for s in range(CONV_W):
            xc = xc + cw_ref[s:s + 1, cs] * xe_scr[pl.ds(hist - (CONV_W - 1) + s, tt), cs]
        a, mult, gi = _lru_gates(xc, n, cs, wa_ref, wx_ref, ba_ref, bx_ref, lam_ref)
        mult = jnp.where(pos == 0, 1.0, mult)
        a_scr[:, cs] = a
        b_scr[:, cs] = mult * (gi * xc)

    def step(i, h):
        h = a_scr[pl.ds(i, 1), :] * h + b_scr[pl.ds(i, 1), :]
        h_scr[pl.ds(i, 1), :] = h
        return h

    h = lax.fori_loop(0, tt, step, hc_scr[...], unroll=8)
    hc_scr[...] = h
    hl_ref[0] = h
    for n in range(N_LRU_BLOCKS):
        cs = slice(n * blk, (n + 1) * blk)
        oa_ref[:, cs] = (h_scr[:, cs] * gg_ref[:, cs].astype(f32)).astype(bf16)
    cn_ref[0] = xe_scr[pl.ds(hist + tt - (CONV_W - 1), CONV_W - 1), :]
    xe_scr[0:hist, :] = xe_scr[pl.ds(tt, hist), :]


def _lru_sample_kernel(xa_ref, gg_ref, scp_ref, h0_ref, cw_ref, cb_ref, wa_ref, wx_ref, ba_ref, bx_ref, lam_ref,
                       oa_ref, hl_ref, cn_ref):
    rows, c = xa_ref.shape
    bt, t_len = scp_ref.shape[0], scp_ref.shape[1]
    blk = c // N_LRU_BLOCKS
    tpos = lax.broadcasted_iota(jnp.int32, (bt, t_len, blk), 1)
    for n in range(N_LRU_BLOCKS):
        cs = slice(n * blk, (n + 1) * blk)
        x3 = xa_ref[:, cs].reshape(bt, t_len, blk)
        ext = jnp.concatenate([scp_ref[:, :, cs], x3], axis=1)
        xc3 = cb_ref[:, cs] + cw_ref[CONV_W - 1:CONV_W, cs] * x3
        for s in range(1, CONV_W):
            xc3 = xc3 + cw_ref[CONV_W - 1 - s:CONV_W - s, cs] * pltpu.roll(ext, s, axis=1)[:, t_len:, :]
        xc = xc3.reshape(rows, blk)
        a, mult, gi = _lru_gates(xc, n, cs, wa_ref, wx_ref, ba_ref, bx_ref, lam_ref)
        a3 = a.reshape(bt, t_len, blk)
        b3 = (mult * (gi * xc)).reshape(bt, t_len, blk)
        d = 1
        while d < t_len:
            keep = tpos >= d
            b3 = jnp.where(keep, a3 * pltpu.roll(b3, d, axis=1) + b3, b3)
            a3 = jnp.where(keep, a3 * pltpu.roll(a3, d, axis=1), a3)
            d *= 2
        h3 = b3 + a3 * h0_ref[:, :, cs]
        oa_ref[:, cs] = (h3.reshape(rows, blk) * gg_ref[:, cs].astype(f32)).astype(bf16)
        hl_ref[:, :, cs] = h3[:, t_len - 1:t_len, :]
        cn_ref[:, :, cs] = pltpu.roll(x3, CONV_W - 1, axis=1)[:, 0:CONV_W - 1, :]


def _ret_norm_gate(o, rn_ref, sg_ref):
    return (o * lax.rsqrt(jnp.mean(o * o, axis=-1, keepdims=True) + EPS) * rn_ref[0] * sg_ref[...].astype(f32)).astype(bf16)


def _ret_prompt_kernel(q_ref, k_ref, v_ref, sg_ref, rn_ref, dm_ref, cd_ref, sd_ref, chd_ref,
                       ob_ref, so_ref, s_scr):
    c = pl.program_id(2)

    @pl.when(c == 0)
    def _():
        s_scr[...] = jnp.zeros_like(s_scr)

    q = q_ref[...]
    k = k_ref[...]
    v = v_ref[...]
    s_old = s_scr[...]
    scores = _dot_nt(q, k) * dm_ref[0]
    inner = _dot(scores.astype(bf16), v)
    cross = _dot(q, s_old.astype(bf16)) * cd_ref[0]
    kd = (k.astype(f32) * sd_ref[0]).astype(bf16)
    s_new = s_old * chd_ref[0] + _dot_tn(kd, v)
    s_scr[...] = s_new
    ob_ref[...] = _ret_norm_gate(inner + cross, rn_ref, sg_ref)

    @pl.when(c == pl.num_programs(2) - 1)
    def _():
        so_ref[0, 0] = s_new


def _ret_sample_kernel(q_ref, k_ref, v_ref, sg_ref, s_ref, rn_ref, dm_ref, cd_ref, sd_ref, chd_ref,
                       ob_ref, so_ref, *, t_len):
    q = q_ref[...]
    k = k_ref[...]
    v = v_ref[...]
    rows = q.shape[0]
    scores = _dot_nt(q, k) * dm_ref[0]
    o = _dot(scores.astype(bf16), v)
    kd = k.astype(f32) * sd_ref[0]
    row = lax.broadcasted_iota(jnp.int32, (rows, 1), 0)
    for bb in range(rows // t_len):
        mine = (row >= bb * t_len) & (row < (bb + 1) * t_len)
        s_old = s_ref[bb, 0]
        cross = _dot(q, s_old.astype(bf16)) * cd_ref[0]
        o = o + jnp.where(mine, cross, 0.0)
        so_ref[bb, 0] = s_old * chd_ref[0] + _dot_tn(jnp.where(mine, kd, 0.0).astype(bf16), v)
    ob_ref[...] = _ret_norm_gate(o, rn_ref, sg_ref)


def _decay_tables(chunk, n_seq):
    log_g = jnp.log1p(-jnp.exp2(-5.0 - jnp.arange(N_RET_HEADS, dtype=f32)))
    r = jnp.arange(chunk * n_seq)
    idx = (r % chunk).astype(f32)
    same = (r[:, None] // chunk) == (r[None, :] // chunk)
    diff = idx[:, None] - idx[None, :]
    dmask = jnp.where(same & (diff >= 0), jnp.exp(jnp.maximum(diff, 0.0)[None] * log_g[:, None, None]), 0.0)
    cross_decay = jnp.exp((idx[None] + 1.0) * log_g[:, None])[..., None]
    state_decay = jnp.exp((chunk - 1.0 - idx[None]) * log_g[:, None])[..., None]
    chunk_decay = jnp.exp(chunk * log_g)[:, None, None]
    return dmask, cross_decay, state_decay, chunk_decay


def _ple_kernel(x_ref, pe_ref, gn_ref, wg_ref, bg_ref, pp_ref, fn_ref, *out_refs, final):
    x = x_ref[...]
    u = _rms(x, gn_ref[...]).astype(bf16)
    gate = _sigmoid(_dot(u, wg_ref[...]) + bg_ref[...])
    x = x + gate * _dot(pe_ref[...].astype(bf16), pp_ref[...])
    out_refs[0][...] = _rms(x, fn_ref[...]) if final else x


def _layer(x, pe, n_prompt, seq, dec_seq, h0_s, conv_s, ret_s, cos_t, sin_t, lp, final_norm):
    m, d = x.shape
    mp = n_prompt * seq
    ms = m - mp
    n_sample = ms // dec_seq
    d_rnn = lp['conv_w'].shape[1]
    hh = N_RET_HEADS
    dk = d // hh
    dv = lp['ret_norm'].shape[1]
    qk_dim, v_dim = hh * dk, hh * dv
    tm, tn = 1024, 512

    x1, u = _ffn(x, lp['ffn1_norm'], lp['ffn1_wg'], lp['ffn1_wu'], lp['ffn1_wd'], lp['mix_norm'])

    w_in = lp['w_in']
    b_in = lp['b_in'].reshape(1, -1)

    def bias_extra(col0):
        return (b_in, (1, tn), lambda j, i, o=col0 // tn: (0, j + o))

    def with_bias(fn):
        return lambda acc, b_ref: fn(acc + b_ref[...])

    c_xa, c_ga, c_q, c_v = 0, d_rnn, 2 * d_rnn, 2 * d_rnn + 2 * qk_dim
    c_gr, c_gate = c_v + v_dim, c_v + 2 * v_dim
    proj = functools.partial(_mm, u, w_in, tm=tm, tn=tn)
    xa = proj(col0=c_xa, n_cols=d_rnn, out_dtype=f32, epilogue=with_bias(lambda z: z),
              extras=[bias_extra(c_xa)], name="inproj_xa")
    gg = proj(col0=c_ga, n_cols=d_rnn, out_dtype=bf16, epilogue=with_bias(_gelu_tanh),
              extras=[bias_extra(c_ga)], name="inproj_ga")
    n_pt = mp // tm
    half = dk // 2
    rope_map = lambda j, i: (jnp.where(i < n_pt, i % (seq // tm), seq // tm + i - n_pt), 0)
    qk = proj(col0=c_q, n_cols=2 * qk_dim, out_dtype=bf16,
              epilogue=functools.partial(_rope_epilogue, k_tile0=qk_dim // tn, k_scale=dk ** -0.5, head_dim=dk),
              extras=[bias_extra(c_q), (cos_t, (tm, half), rope_map), (sin_t, (tm, half), rope_map)],
              name="inproj_qk")
    v = proj(col0=c_v, n_cols=v_dim, out_dtype=bf16, epilogue=with_bias(lambda z: z),
             extras=[bias_extra(c_v)], name="inproj_v")
    sg = proj(col0=c_gr, n_cols=v_dim, out_dtype=bf16, epilogue=with_bias(lambda z: z * _sigmoid(z)),
              extras=[bias_extra(c_gr)], name="inproj_gr")
    gates = proj(col0=c_gate, n_cols=2 * d, out_dtype=bf16, epilogue=with_bias(_sigmoid),
                 extras=[bias_extra(c_gate)], name="inproj_gates")

    row2 = lambda a: a.reshape(1, -1)
    lru_w = [lp['conv_w'], row2(lp['conv_b']), lp['lru_wa'], lp['lru_wx'], row2(lp['lru_ba']), row2(lp['lru_bx']),
             row2(lp['lru_lambda'])]
    nb = N_LRU_BLOCKS
    blk = d_rnn // nb

    def lru_w_specs(nidx):
        z2 = (lambda *_: (0, 0))
        z3 = (lambda *_: (0, 0, 0))
        return [pl.BlockSpec((CONV_W, d_rnn), z2), pl.BlockSpec((1, d_rnn), z2),
                pl.BlockSpec((nb, blk, blk), z3), pl.BlockSpec((nb, blk, blk), z3),
                pl.BlockSpec((1, d_rnn), z2), pl.BlockSpec((1, d_rnn), z2), pl.BlockSpec((1, d_rnn), z2)]

    tt = 256
    ntt = seq // tt
    seq_rows = lambda b, t: (b * ntt + t, 0)
    per_seq = lambda b, t: (b, 0, 0)
    oa_p, hl_p, cn_p = pl.pallas_call(
        _lru_prompt_kernel,
        grid=(n_prompt, ntt),
        in_specs=[pl.BlockSpec((tt, d_rnn), seq_rows), pl.BlockSpec((tt, d_rnn), seq_rows)] + lru_w_specs(2),
        out_specs=[pl.BlockSpec((tt, d_rnn), seq_rows), pl.BlockSpec((1, 1, d_rnn), per_seq),
                   pl.BlockSpec((1, CONV_W - 1, d_rnn), per_seq)],
        out_shape=[jax.ShapeDtypeStruct((mp, d_rnn), bf16), jax.ShapeDtypeStruct((n_prompt, 1, d_rnn), f32),
                   jax.ShapeDtypeStruct((n_prompt, CONV_W - 1, d_rnn), f32)],
        scratch_shapes=[pltpu.VMEM((tt + 8, d_rnn), f32), pltpu.VMEM((tt, d_rnn), f32), pltpu.VMEM((tt, d_rnn), f32),
                        pltpu.VMEM((tt, d_rnn), f32), pltpu.VMEM((1, d_rnn), f32)],
        compiler_params=_cparams(("parallel", "arbitrary")),
        name="lru_prompt",
    )(xa, gg, *lru_w)

    bt = 32
    rows_s = bt * dec_seq
    s_off = mp // rows_s
    scp = jnp.pad(conv_s, ((0, 0), (dec_seq - (CONV_W - 1), 0), (0, 0)))
    samp_rows = lambda i: (s_off + i, 0)
    samp3 = lambda i: (i, 0, 0)
    oa_s, hl_s, cn_s = pl.pallas_call(
        _lru_sample_kernel,
        grid=(n_sample // bt,),
        in_specs=[pl.BlockSpec((rows_s, d_rnn), samp_rows), pl.BlockSpec((rows_s, d_rnn), samp_rows),
                  pl.BlockSpec((bt, dec_seq, d_rnn), samp3), pl.BlockSpec((bt, 1, d_rnn), samp3)] + lru_w_specs(1),
        out_specs=[pl.BlockSpec((rows_s, d_rnn), lambda i: (i, 0)), pl.BlockSpec((bt, 1, d_rnn), samp3),
                   pl.BlockSpec((bt, CONV_W - 1, d_rnn), samp3)],
        out_shape=[jax.ShapeDtypeStruct((ms, d_rnn), bf16), jax.ShapeDtypeStruct((n_sample, 1, d_rnn), f32),
                   jax.ShapeDtypeStruct((n_sample, CONV_W - 1, d_rnn), f32)],
        compiler_params=_cparams(("parallel",)),
        name="lru_sample",
    )(xa, gg, scp, h0_s.reshape(n_sample, 1, d_rnn), *lru_w)
    oa = jnp.concatenate([oa_p, oa_s], axis=0)

    rn = lp['ret_norm'].reshape(hh, 1, dv)
    ct = RET_CHUNK
    nct = seq // ct
    dm, cd, sd, chd = _decay_tables(ct, 1)
    head3 = lambda b, h, c: (h, 0, 0)
    ob_p, so_p = pl.pallas_call(
        _ret_prompt_kernel,
        grid=(n_prompt, hh, nct),
        in_specs=[pl.BlockSpec((ct, dk), lambda b, h, c: (b * nct + c, h)),
                  pl.BlockSpec((ct, dk), lambda b, h, c: (b * nct + c, hh + h)),
                  pl.BlockSpec((ct, dv), lambda b, h, c: (b * nct + c, h)),
                  pl.BlockSpec((ct, dv), lambda b, h, c: (b * nct + c, h)),
                  pl.BlockSpec((1, 1, dv), head3), pl.BlockSpec((1, ct, ct), head3),
                  pl.BlockSpec((1, ct, 1), head3), pl.BlockSpec((1, ct, 1), head3), pl.BlockSpec((1, 1, 1), head3)],
        out_specs=[pl.BlockSpec((ct, dv), lambda b, h, c: (b * nct + c, h)),
                   pl.BlockSpec((1, 1, dk, dv), lambda b, h, c: (b, h, 0, 0))],
        out_shape=[jax.ShapeDtypeStruct((mp, v_dim), bf16), jax.ShapeDtypeStruct((n_prompt, hh, dk, dv), f32)],
        scratch_shapes=[pltpu.VMEM((dk, dv), f32)],
        compiler_params=_cparams(("parallel", "parallel", "arbitrary")),
        name="ret_prompt",
    )(qk, qk, v, sg, rn, dm, cd, sd, chd)

    pr = SAMPLE_PAIR
    rows_r = pr * dec_seq
    r_off = mp // rows_r
    dm, cd, sd, chd = _decay_tables(dec_seq, pr)
    head3s = lambda p, h: (h, 0, 0)
    ob_s, so_s = pl.pallas_call(
        functools.partial(_ret_sample_kernel, t_len=dec_seq),
        grid=(n_sample // pr, hh),
        in_specs=[pl.BlockSpec((rows_r, dk), lambda p, h: (r_off + p, h)),
                  pl.BlockSpec((rows_r, dk), lambda p, h: (r_off + p, hh + h)),
                  pl.BlockSpec((rows_r, dv), lambda p, h: (r_off + p, h)),
                  pl.BlockSpec((rows_r, dv), lambda p, h: (r_off + p, h)),
                  pl.BlockSpec((pr, 1, dk, dv), lambda p, h: (p, h, 0, 0)),
                  pl.BlockSpec((1, 1, dv), head3s), pl.BlockSpec((1, rows_r, rows_r), head3s),
                  pl.BlockSpec((1, rows_r, 1), head3s), pl.BlockSpec((1, rows_r, 1), head3s),
                  pl.BlockSpec((1, 1, 1), head3s)],
        out_specs=[pl.BlockSpec((rows_r, dv), lambda p, h: (p, h)),
                   pl.BlockSpec((pr, 1, dk, dv), lambda p, h: (p, h, 0, 0))],
        out_shape=[jax.ShapeDtypeStruct((ms, v_dim), bf16), jax.ShapeDtypeStruct((n_sample, hh, dk, dv), f32)],
        compiler_params=_cparams(("parallel", "parallel")),
        name="ret_sample",
    )(qk, qk, v, sg, ret_s, rn, dm, cd, sd, chd)
    ob = jnp.concatenate([ob_p, ob_s], axis=0)

    tile = lambda j, i: (i, j)
    pa = _mm(oa, lp['proj_a'], col0=0, n_cols=d, out_dtype=f32, epilogue=lambda acc: acc, name="proj_a")
    nd = d // tn
    merged = _mm(ob, lp['proj_b'], col0=0, n_cols=d, out_dtype=bf16,
                 epilogue=lambda acc, pa_ref, ga_ref, gb_ref: (ga_ref[...].astype(f32) * pa_ref[...]
                                                               + gb_ref[...].astype(f32) * acc),
                 extras=[(pa, (tm, tn), tile), (gates, (tm, tn), tile),
                         (gates, (tm, tn), lambda j, i: (i, j + nd))], name="proj_b_merge")
    x2 = _mm(merged, lp['w_out'], col0=0, n_cols=d, out_dtype=f32,
             epilogue=lambda acc, x_ref: x_ref[...] + acc, extras=[(x1, (tm, tn), tile)], name="w_out")

    x3 = _ffn(x2, lp['ffn2_norm'], lp['ffn2_wg'], lp['ffn2_wu'], lp['ffn2_wd'])

    tp = 512
    ple_dim = pe.shape[1]
    fin = final_norm is not None
    fn = (final_norm if fin else lp['ple_norm']).reshape(1, d)
    rowp = lambda i: (i, 0)
    z2 = lambda i: (0, 0)
    x4 = pl.pallas_call(
        functools.partial(_ple_kernel, final=fin),
        grid=(m // tp,),
        in_specs=[pl.BlockSpec((tp, d), rowp), pl.BlockSpec((tp, ple_dim), rowp), pl.BlockSpec((1, d), z2),
                  pl.BlockSpec((d, d), z2), pl.BlockSpec((1, d), z2), pl.BlockSpec((ple_dim, d), z2),
                  pl.BlockSpec((1, d), z2)],
        out_specs=pl.BlockSpec((tp, d), rowp),
        out_shape=jax.ShapeDtypeStruct((m, d), f32),
        compiler_params=_cparams(("parallel",)),
        name="ple",
    )(x3, pe, lp['ple_norm'].reshape(1, d), lp['ple_wg'], lp['ple_bg'].reshape(1, d), lp['ple_proj'], fn)

    states = (hl_p.reshape(n_prompt, d_rnn), cn_p, so_p, hl_s.reshape(n_sample, d_rnn), cn_s, so_s)
    return x4, states


_MATMUL_WEIGHTS = ('ffn1_wg', 'ffn1_wu', 'ffn1_wd', 'w_in', 'lru_wa', 'lru_wx', 'proj_a', 'proj_b', 'w_out',
                   'ffn2_wg', 'ffn2_wu', 'ffn2_wd', 'ple_wg', 'ple_proj')


def kernel(x_prompt, x_sample, p_prompt, p_sample, state_lru, state_conv, state_ret, ffn1_norm, ffn1_wg, ffn1_wu, ffn1_wd, mix_norm, w_in, b_in, conv_w, conv_b, lru_wa, lru_ba, lru_wx, lru_bx, lru_lambda, ret_norm, proj_a, proj_b, w_out, ffn2_norm, ffn2_wg, ffn2_wu, ffn2_wd, ple_norm, ple_wg, ple_bg, ple_proj, final_norm):
    params = dict(ffn1_norm=ffn1_norm, ffn1_wg=ffn1_wg, ffn1_wu=ffn1_wu, ffn1_wd=ffn1_wd, mix_norm=mix_norm,
                  w_in=w_in, b_in=b_in, conv_w=conv_w, conv_b=conv_b, lru_wa=lru_wa, lru_ba=lru_ba, lru_wx=lru_wx,
                  lru_bx=lru_bx, lru_lambda=lru_lambda, ret_norm=ret_norm, proj_a=proj_a, proj_b=proj_b,
                  w_out=w_out, ffn2_norm=ffn2_norm, ffn2_wg=ffn2_wg, ffn2_wu=ffn2_wu, ffn2_wd=ffn2_wd,
                  ple_norm=ple_norm, ple_wg=ple_wg, ple_bg=ple_bg, ple_proj=ple_proj)
    depth = w_in.shape[0]
    n_prompt, seq, d = x_prompt.shape
    n_sample, dec_seq, _ = x_sample.shape
    mp, ms = n_prompt * seq, n_sample * dec_seq
    dk = d // N_RET_HEADS

    x = jnp.concatenate([x_prompt.astype(f32).reshape(mp, d), x_sample.astype(f32).reshape(ms, d)], axis=0)
    cos_t, sin_t = _rope_table(seq, dec_seq, ms, dk // 2)
    outs = [[] for _ in range(6)]
    for i in range(depth):
        lp = {k: (v[i].astype(bf16) if k in _MATMUL_WEIGHTS else v[i].astype(f32)) for k, v in params.items()}
        pe = jnp.concatenate([p_prompt[i].reshape(mp, -1), p_sample[i].reshape(ms, -1)], axis=0).astype(f32)
        x, st = _layer(x, pe, n_prompt, seq, dec_seq, state_lru[i].astype(f32), state_conv[i].astype(f32),
                       state_ret[i].astype(f32), cos_t, sin_t, lp, final_norm.astype(f32) if i == depth - 1 else None)
        for o, s in zip(outs, st):
            o.append(s)
    y_prompt = x[:mp].reshape(n_prompt, seq, d).astype(x_prompt.dtype)
    y_sample = x[mp:].reshape(n_sample, dec_seq, d).astype(x_sample.dtype)
    lru_p, conv_p, ret_p, lru_s, conv_s, ret_s = (jnp.stack(o) for o in outs)
    return (y_prompt, y_sample, lru_p.astype(state_lru.dtype), conv_p.astype(state_conv.dtype),
            ret_p.astype(state_ret.dtype), lru_s.astype(state_lru.dtype), conv_s.astype(state_conv.dtype),
            ret_s.astype(state_ret.dtype))
```

```python
import functools

import jax
import jax.numpy as jnp
from jax import lax
from jax.experimental import pallas as pl
from jax.experimental.pallas import tpu as pltpu

f32 = jnp.float32
bf16 = jnp.bfloat16

N_LRU_BLOCKS = 8
CONV_W = 4
LRU_C = 8.0
N_RET_HEADS = 8
ROPE_BASE = 10000.0
EPS = 1e-6
PAST_LEN = 16384

RET_CHUNK = 256
SAMPLE_PAIR = 2
ROW_TILE = 1024
VMEM_LIMIT = 58 * 1024 * 1024


def _cparams(sem):
    return pltpu.CompilerParams(dimension_semantics=sem, vmem_limit_bytes=VMEM_LIMIT)


def _sigmoid(x):
    return 1.0 / (1.0 + jnp.exp(-x))


def _rms(x, g):
    return x * lax.rsqrt(jnp.mean(x * x, axis=-1, keepdims=True) + EPS) * g


def _dot(a, b):
    return jnp.dot(a, b, preferred_element_type=f32)


def _dot_nt(a, b):
    return lax.dot_general(a, b, (((1,), (1,)), ((), ())), preferred_element_type=f32)


def _dot_tn(a, b):
    return lax.dot_general(a, b, (((0,), (0,)), ((), ())), preferred_element_type=f32)


_ANY = pl.BlockSpec(memory_space=pl.ANY)


def _cast_kernel(x_ref, o_ref):
    o_ref[...] = x_ref[...].astype(o_ref.dtype)


def _to_bf16(w):
    w2 = w.reshape(-1, w.shape[-1])
    return pl.pallas_call(_cast_kernel, out_shape=jax.ShapeDtypeStruct(w2.shape, bf16), name="to_bf16")(w2).reshape(w.shape)


def _rope_table_kernel(inv_ref, cos_ref, sin_ref, *, seq, dec_seq):
    rows = cos_ref.shape[0]
    r = lax.broadcasted_iota(jnp.int32, (rows, inv_ref.shape[1]), 0)
    pos = jnp.where(r < seq, r, PAST_LEN + lax.rem(r - seq, dec_seq))
    ang = pos.astype(f32) * inv_ref[...]
    cos_ref[...] = jnp.cos(ang)
    sin_ref[...] = jnp.sin(ang)


def _rope_table(seq, dec_seq, n_sample_rows, half):
    inv = (ROPE_BASE ** (-jnp.arange(half, dtype=f32) / half)).reshape(1, half)
    rows = seq + n_sample_rows
    return pl.pallas_call(
        functools.partial(_rope_table_kernel, seq=seq, dec_seq=dec_seq),
        out_shape=(jax.ShapeDtypeStruct((rows, half), f32), jax.ShapeDtypeStruct((rows, half), f32)),
        name="rope_table",
    )(inv)


def _ffn_kernel(*refs, group_tiles, emit_norm):
    n_groups = len(group_tiles)
    x_hbm = refs[:n_groups]
    g_ref, wg_ref, wu_ref, wd_ref = refs[n_groups:n_groups + 4]
    rest = list(refs[n_groups + 4:])
    g2_ref = rest.pop(0) if emit_norm else None
    xo_ref = rest.pop(0)
    u2_ref = rest.pop(0) if emit_norm else None
    (u_scr,) = rest
    i = pl.program_id(0)
    f = pl.program_id(1)
    tm = xo_ref.shape[0]

    @pl.when(f == 0)
    def _():
        for gi, lo in enumerate(group_tiles):
            mine = i >= lo
            if gi + 1 < n_groups:
                mine = mine & (i < group_tiles[gi + 1])

            @pl.when(mine)
            def _(gi=gi, lo=lo):
                pltpu.sync_copy(x_hbm[gi].at[pl.ds(pl.multiple_of((i - lo) * tm, tm), tm), :], xo_ref)

        u_scr[...] = _rms(xo_ref[...], g_ref[...]).astype(bf16)

    u = u_scr[...]
    hg = _dot(u, wg_ref[...].astype(bf16))
    hu = _dot(u, wu_ref[...].astype(bf16))
    h = (0.5 * hg * _sigmoid(hg) * hu).astype(bf16)
    xo_ref[...] += _dot(h, wd_ref[...].astype(bf16))

    if emit_norm:
        @pl.when(f == pl.num_programs(1) - 1)
        def _():
            u2_ref[...] = _rms(xo_ref[...], g2_ref[...]).astype(bf16)


def _ffn(parts, m_total, g, wg, wu, wd, g2=None, *, tm=ROW_TILE, tf=256):
    d = wg.shape[0]
    ff = wg.shape[1]
    emit_norm = g2 is not None
    row = lambda i, f: (i, 0)
    in_specs = [_ANY] * len(parts) + [
        pl.BlockSpec((1, d), lambda i, f: (0, 0)),
        pl.BlockSpec((d, tf), lambda i, f: (0, f)),
        pl.BlockSpec((d, tf), lambda i, f: (0, f)),
        pl.BlockSpec((tf, d), lambda i, f: (f, 0)),
    ]
    args = [xp for xp, _ in parts] + [g.reshape(1, d), wg, wu, wd]
    out_shape = [jax.ShapeDtypeStruct((m_total, d), f32)]
    out_specs = [pl.BlockSpec((tm, d), row)]
    if emit_norm:
        in_specs.append(pl.BlockSpec((1, d), lambda i, f: (0, 0)))
        args.append(g2.reshape(1, d))
        out_shape.append(jax.ShapeDtypeStruct((m_total, d), bf16))
        out_specs.append(pl.BlockSpec((tm, d), row))
    outs = pl.pallas_call(
        functools.partial(_ffn_kernel, group_tiles=tuple(t0 for _, t0 in parts), emit_norm=emit_norm),
        grid=(m_total // tm, ff // tf),
        in_specs=in_specs,
        out_specs=out_specs,
        out_shape=out_shape,
        scratch_shapes=[pltpu.VMEM((tm, d), bf16)],
        compiler_params=_cparams(("arbitrary", "arbitrary")),
        name="ffn",
    )(*args)
    return outs if emit_norm else outs[0]


def _mm_kernel(a_ref, b_ref, *rest, epilogue):
    *extra, o_ref, bw_scr = rest

    @pl.when(pl.program_id(1) == 0)
    def _():
        bw_scr[...] = b_ref[...].astype(bf16)

    o_ref[...] = epilogue(_dot(a_ref[...], bw_scr[...]), *extra).astype(o_ref.dtype)


def _mm(a, b, *, col0, n_cols, out_dtype, epilogue, extras=(), tm=ROW_TILE, tn=1024, name="mm"):
    m, k = a.shape
    off = col0 // tn
    in_specs = [
        pl.BlockSpec((tm, k), lambda j, i: (i, 0)),
        pl.BlockSpec((k, tn), lambda j, i: (0, j + off)),
    ]
    args = [a, b]
    for arr, blk, imap in extras:
        in_specs.append(pl.BlockSpec(blk, imap))
        args.append(arr)
    return pl.pallas_call(
        functools.partial(_mm_kernel, epilogue=epilogue),
        grid=(n_cols // tn, m // tm),
        in_specs=in_specs,
        out_specs=pl.BlockSpec((tm, tn), lambda j, i: (i, j)),
        out_shape=jax.ShapeDtypeStruct((m, n_cols), out_dtype),
        scratch_shapes=[pltpu.VMEM((k, tn), bf16)],
        compiler_params=_cparams(("parallel", "arbitrary")),
        name=name,
    )(*args)


def _gelu_tanh(x):
    return 0.5 * x * (1.0 + jnp.tanh(0.7978845608028654 * (x + 0.044715 * (x * x * x))))


def _rope_epilogue(acc, bias_ref, cos_ref, sin_ref, *, k_tile0, k_scale, head_dim):
    z = acc + bias_ref[...]
    cos = cos_ref[...]
    sin = sin_ref[...]
    half = head_dim // 2
    parts = []
    for h0 in range(0, z.shape[1], head_dim):
        x1 = z[:, h0:h0 + half]
        x2 = z[:, h0 + half:h0 + head_dim]
        parts += [x1 * cos - x2 * sin, x2 * cos + x1 * sin]
    scale = jnp.where(pl.program_id(0) >= k_tile0, k_scale, 1.0).astype(f32)
    return jnp.concatenate(parts, axis=1) * scale


def _lru_gates(xc, n, cs, wa_ref, wx_ref, ba_ref, bx_ref, lam_ref):
    xcb = xc.astype(bf16)
    r = _sigmoid(_dot(xcb, wa_ref[n]) + ba_ref[:, cs])
    gi = _sigmoid(_dot(xcb, wx_ref[n]) + bx_ref[:, cs])
    log_a = -LRU_C * r * jax.nn.softplus(-lam_ref[:, cs])
    a = jnp.exp(log_a)
    a2 = a * a
    one_minus = jnp.where(log_a < -0.25, 1.0 - a2, -jnp.tanh(log_a) * (a2 + 1.0))
    return a, jnp.sqrt(one_minus), gi


def _lru_prompt_kernel(*refs, n_seq):
    oa_ref = refs[9]

    @pl.when(pl.program_id(0) >= n_seq)
    def _():
        oa_ref[...] = jnp.zeros_like(oa_ref)

    @pl.when(pl.program_id(0) < n_seq)
    def _():
        _lru_prompt_body(*refs)


def _lru_prompt_body(xa_ref, gg_ref, cw_ref, cb_ref, wa_ref, wx_ref, ba_ref, bx_ref, lam_ref,
                     oa_ref, hl_ref, cn_ref, xe_scr, a_scr, b_scr, h_scr, hc_scr):
    tt, c = xa_ref.shape
    blk = c // N_LRU_BLOCKS
    hist = 8
    t = pl.program_id(1)

    @pl.when(t == 0)
    def _():
        xe_scr[0:hist, :] = jnp.zeros((hist, c), f32)
        hc_scr[...] = jnp.zeros_like(hc_scr)

    xe_scr[hist:hist + tt, :] = xa_ref[...]
    pos = lax.broadcasted_iota(jnp.int32, (tt, 1), 0) + t * tt
    for n in range(N_LRU_BLOCKS):
        cs = slice(n * blk, (n + 1) * blk)
        xc = cb_ref[:, cs]
        for s in range(CONV_W):
            xc = xc + cw_ref[s:s + 1, cs] * xe_scr[pl.ds(hist - (CONV_W - 1) + s, tt), cs]
        a, mult, gi = _lru_gates(xc, n, cs, wa_ref, wx_ref, ba_ref, bx_ref, lam_ref)
        mult = jnp.where(pos == 0, 1.0, mult)
        a_scr[:, cs] = a
        b_scr[:, cs] = mult * (gi * xc)

    def step(i, h):
        h = a_scr[pl.ds(i, 1), :] * h + b_scr[pl.ds(i, 1), :]
        h_scr[pl.ds(i, 1), :] = h
        return h

    h = lax.fori_loop(0, tt, step, hc_scr[...], unroll=8)
    hc_scr[...] = h
    hl_ref[0] = h
    for n in range(N_LRU_BLOCKS):
        cs = slice(n * blk, (n + 1) * blk)
        oa_ref[:, cs] = (h_scr[:, cs] * gg_ref[:, cs].astype(f32)).astype(bf16)
    cn_ref[0] = xe_scr[pl.ds(hist + tt - (CONV_W - 1), CONV_W - 1), :]
    xe_scr[0:hist, :] = xe_scr[pl.ds(tt, hist), :]


def _lru_sample_kernel(xa_ref, gg_ref, scp_ref, h0_ref, cw_ref, cb_ref, wa_ref, wx_ref, ba_ref, bx_ref, lam_ref,
                       oa_prev_ref, oa_ref, hl_ref, cn_ref):
    del oa_prev_ref
    rows, c = xa_ref.shape
    bt, t_len = scp_ref.shape[0], scp_ref.shape[1]
    blk = c // N_LRU_BLOCKS
    tpos = lax.broadcasted_iota(jnp.int32, (bt, t_len, blk), 1)
    for n in range(N_LRU_BLOCKS):
        cs = slice(n * blk, (n + 1) * blk)
        x3 = xa_ref[:, cs].reshape(bt, t_len, blk)
        ext = jnp.concatenate([scp_ref[:, :, cs], x3], axis=1)
        xc3 = cb_ref[:, cs] + cw_ref[CONV_W - 1:CONV_W, cs] * x3
        for s in range(1, CONV_W):
            xc3 = xc3 + cw_ref[CONV_W - 1 - s:CONV_W - s, cs] * pltpu.roll(ext, s, axis=1)[:, t_len:, :]
        xc = xc3.reshape(rows, blk)
        a, mult, gi = _lru_gates(xc, n, cs, wa_ref, wx_ref, ba_ref, bx_ref, lam_ref)
        a3 = a.reshape(bt, t_len, blk)
        b3 = (mult * (gi * xc)).reshape(bt, t_len, blk)
        d = 1
        while d < t_len:
            keep = tpos >= d
            b3 = jnp.where(keep, a3 * pltpu.roll(b3, d, axis=1) + b3, b3)
            a3 = jnp.where(keep, a3 * pltpu.roll(a3, d, axis=1), a3)
            d *= 2
        h3 = b3 + a3 * h0_ref[:, :, cs]
        oa_ref[:, cs] = (h3.reshape(rows, blk) * gg_ref[:, cs].astype(f32)).astype(bf16)
        hl_ref[:, :, cs] = h3[:, t_len - 1:t_len, :]
        cn_ref[:, :, cs] = pltpu.roll(x3, CONV_W - 1, axis=1)[:, 0:CONV_W - 1, :]


def _ret_norm_gate(o, rn, sg):
    return (o * lax.rsqrt(jnp.mean(o * o, axis=-1, keepdims=True) + EPS) * rn * sg.astype(f32)).astype(bf16)


def _ret_prompt_kernel(*refs, n_seq):
    ob_ref = refs[9]

    @pl.when(pl.program_id(0) >= n_seq)
    def _():
        ob_ref[...] = jnp.zeros_like(ob_ref)

    @pl.when(pl.program_id(0) < n_seq)
    def _():
        _ret_prompt_body(*refs)


def _ret_prompt_body(q_ref, k_ref, v_ref, sg_ref, rn_ref, dm_ref, cd_ref, sd_ref, chd_ref, ob_ref, so_ref):
    c = pl.program_id(1)
    dk, dv = so_ref.shape[2], so_ref.shape[3]

    @pl.when(c == 0)
    def _():
        so_ref[...] = jnp.zeros_like(so_ref)

    for h in range(N_RET_HEADS):
        ks = slice(h * dk, (h + 1) * dk)
        vs = slice(h * dv, (h + 1) * dv)
        q = q_ref[:, ks]
        k = k_ref[:, ks]
        v = v_ref[:, vs]
        s_old = so_ref[0, h]
        scores = _dot_nt(q, k) * dm_ref[h]
        inner = _dot(scores.astype(bf16), v)
        cross = _dot(q, s_old.astype(bf16)) * cd_ref[h]
        kd = (k.astype(f32) * sd_ref[h]).astype(bf16)
        so_ref[0, h] = s_old * chd_ref[h] + _dot_tn(kd, v)
        ob_ref[:, vs] = _ret_norm_gate(inner + cross, rn_ref[h], sg_ref[:, vs])


def _ret_sample_kernel(q_ref, k_ref, v_ref, sg_ref, s_ref, rn_ref, dm_ref, cd_ref, sd_ref, chd_ref, ob_prev_ref,
                       ob_ref, so_ref, *, t_len):
    del ob_prev_ref
    rows = q_ref.shape[0]
    dk, dv = so_ref.shape[2], so_ref.shape[3]
    row = lax.broadcasted_iota(jnp.int32, (rows, 1), 0)
    for h in range(N_RET_HEADS):
        ks = slice(h * dk, (h + 1) * dk)
        vs = slice(h * dv, (h + 1) * dv)
        q = q_ref[:, ks]
        k = k_ref[:, ks]
        v = v_ref[:, vs]
        scores = _dot_nt(q, k) * dm_ref[h]
        o = _dot(scores.astype(bf16), v)
        kd = k.astype(f32) * sd_ref[h]
        for bb in range(rows // t_len):
            mine = (row >= bb * t_len) & (row < (bb + 1) * t_len)
            s_old = s_ref[bb, h]
            cross = _dot(q, s_old.astype(bf16)) * cd_ref[h]
            o = o + jnp.where(mine, cross, 0.0)
            so_ref[bb, h] = s_old * chd_ref[h] + _dot_tn(jnp.where(mine, kd, 0.0).astype(bf16), v)
        ob_ref[:, vs] = _ret_norm_gate(o, rn_ref[h], sg_ref[:, vs])


def _decay_tables(chunk, n_seq):
    log_g = jnp.log1p(-jnp.exp2(-5.0 - jnp.arange(N_RET_HEADS, dtype=f32)))
    r = jnp.arange(chunk * n_seq)
    idx = (r % chunk).astype(f32)
    same = (r[:, None] // chunk) == (r[None, :] // chunk)
    diff = idx[:, None] - idx[None, :]
    dmask = jnp.where(same & (diff >= 0), jnp.exp(jnp.maximum(diff, 0.0)[None] * log_g[:, None, None]), 0.0)
    cross_decay = jnp.exp((idx[None] + 1.0) * log_g[:, None])[..., None]
    state_decay = jnp.exp((chunk - 1.0 - idx[None]) * log_g[:, None])[..., None]
    chunk_decay = jnp.exp(chunk * log_g)[:, None, None]
    return dmask, cross_decay, state_decay, chunk_decay


def _ple_kernel(x_ref, pep_ref, pes_ref, gn_ref, wg_ref, bg_ref, pp_ref, fn_ref, yp_ref, ys_ref, *, n_prompt_tiles, final):
    i = pl.program_id(0)
    x = x_ref[...]
    u = _rms(x, gn_ref[...]).astype(bf16)
    gate = _sigmoid(_dot(u, wg_ref[...]) + bg_ref[...])
    pe = jnp.where(i < n_prompt_tiles, pep_ref[...], pes_ref[...]).astype(bf16)
    x = x + gate * _dot(pe, pp_ref[...])
    y = _rms(x, fn_ref[...]) if final else x

    @pl.when(i < n_prompt_tiles)
    def _():
        yp_ref[...] = y

    @pl.when(i >= n_prompt_tiles)
    def _():
        ys_ref[...] = y


def _layer(x_parts, pe_p, pe_s, n_prompt, seq, n_sample, dec_seq, h0_s, conv_s, ret_s, cos_t, sin_t, lp, final_norm):
    d = lp['mix_norm'].shape[0]
    mp, ms = n_prompt * seq, n_sample * dec_seq
    m = mp + ms
    d_rnn = lp['conv_w'].shape[1]
    hh = N_RET_HEADS
    dk = d // hh
    dv = lp['ret_norm'].shape[1]
    qk_dim, v_dim = hh * dk, hh * dv
    tm, tn = ROW_TILE, 1024

    x1, u = _ffn(x_parts, m, lp['ffn1_norm'], lp['ffn1_wg'], lp['ffn1_wu'], lp['ffn1_wd'], lp['mix_norm'])

    w_in = lp['w_in']
    b_in = lp['b_in'].reshape(1, -1)

    def bias_extra(col0):
        return (b_in, (1, tn), lambda j, i, o=col0 // tn: (0, j + o))

    def with_bias(fn):
        return lambda acc, b_ref: fn(acc + b_ref[...])

    c_xa, c_ga, c_q, c_v = 0, d_rnn, 2 * d_rnn, 2 * d_rnn + 2 * qk_dim
    c_gr, c_gate = c_v + v_dim, c_v + 2 * v_dim
    proj = functools.partial(_mm, u, w_in, tm=tm, tn=tn)
    xa = proj(col0=c_xa, n_cols=d_rnn, out_dtype=f32, epilogue=with_bias(lambda z: z),
              extras=[bias_extra(c_xa)], name="inproj_xa")
    gg = proj(col0=c_ga, n_cols=d_rnn, out_dtype=bf16, epilogue=with_bias(_gelu_tanh),
              extras=[bias_extra(c_ga)], name="inproj_ga")
    n_pt = mp // tm
    half = dk // 2
    rope_map = lambda j, i: (jnp.where(i < n_pt, i % (seq // tm), seq // tm + i - n_pt), 0)
    qk = proj(col0=c_q, n_cols=2 * qk_dim, out_dtype=bf16,
              epilogue=functools.partial(_rope_epilogue, k_tile0=qk_dim // tn, k_scale=dk ** -0.5, head_dim=dk),
              extras=[bias_extra(c_q), (cos_t, (tm, half), rope_map), (sin_t, (tm, half), rope_map)],
              name="inproj_qk")
    v = proj(col0=c_v, n_cols=v_dim, out_dtype=bf16, epilogue=with_bias(lambda z: z),
             extras=[bias_extra(c_v)], name="inproj_v")
    sg = proj(col0=c_gr, n_cols=v_dim, out_dtype=bf16, epilogue=with_bias(lambda z: z * _sigmoid(z)),
              extras=[bias_extra(c_gr)], name="inproj_gr")
    gates = proj(col0=c_gate, n_cols=2 * d, out_dtype=bf16, epilogue=with_bias(_sigmoid),
                 extras=[bias_extra(c_gate)], name="inproj_gates")

    row2 = lambda a: a.reshape(1, -1)
    lru_w = [lp['conv_w'], row2(lp['conv_b']), _to_bf16(lp['lru_wa']), _to_bf16(lp['lru_wx']), row2(lp['lru_ba']),
             row2(lp['lru_bx']), row2(lp['lru_lambda'])]
    nb = N_LRU_BLOCKS
    blk = d_rnn // nb
    z2 = lambda *_: (0, 0)
    z3 = lambda *_: (0, 0, 0)
    lru_w_specs = [pl.BlockSpec((CONV_W, d_rnn), z2), pl.BlockSpec((1, d_rnn), z2),
                   pl.BlockSpec((nb, blk, blk), z3), pl.BlockSpec((nb, blk, blk), z3),
                   pl.BlockSpec((1, d_rnn), z2), pl.BlockSpec((1, d_rnn), z2), pl.BlockSpec((1, d_rnn), z2)]

    tt = 256
    ntt = seq // tt
    assert ms % tt == 0 and ms // tt <= ntt
    seq_rows = lambda b, t: (jnp.minimum(b * ntt + t, mp // tt - 1), 0)
    all_rows = lambda b, t: (jnp.minimum(b * ntt + t, m // tt - 1), 0)
    per_seq = lambda b, t: (jnp.minimum(b, n_prompt - 1), 0, 0)
    oa, hl_p, cn_p = pl.pallas_call(
        functools.partial(_lru_prompt_kernel, n_seq=n_prompt),
        grid=(n_prompt + 1, ntt),
        in_specs=[pl.BlockSpec((tt, d_rnn), seq_rows), pl.BlockSpec((tt, d_rnn), seq_rows)] + lru_w_specs,
        out_specs=[pl.BlockSpec((tt, d_rnn), all_rows), pl.BlockSpec((1, 1, d_rnn), per_seq),
                   pl.BlockSpec((1, CONV_W - 1, d_rnn), per_seq)],
        out_shape=[jax.ShapeDtypeStruct((m, d_rnn), bf16), jax.ShapeDtypeStruct((n_prompt, 1, d_rnn), f32),
                   jax.ShapeDtypeStruct((n_prompt, CONV_W - 1, d_rnn), f32)],
        scratch_shapes=[pltpu.VMEM((tt + 8, d_rnn), f32), pltpu.VMEM((tt, d_rnn), f32), pltpu.VMEM((tt, d_rnn), f32),
                        pltpu.VMEM((tt, d_rnn), f32), pltpu.VMEM((1, d_rnn), f32)],
        compiler_params=_cparams(("arbitrary", "arbitrary")),
        name="lru_prompt",
    )(xa, gg, *lru_w)

    bt = 32
    rows_s = bt * dec_seq
    s_off = mp // rows_s
    scp = jnp.pad(conv_s, ((0, 0), (dec_seq - (CONV_W - 1), 0), (0, 0)))
    samp_rows = lambda i: (s_off + i, 0)
    samp3 = lambda i: (i, 0, 0)
    oa, hl_s, cn_s = pl.pallas_call(
        _lru_sample_kernel,
        grid=(n_sample // bt,),
        in_specs=[pl.BlockSpec((rows_s, d_rnn), samp_rows), pl.BlockSpec((rows_s, d_rnn), samp_rows),
                  pl.BlockSpec((bt, dec_seq, d_rnn), samp3), pl.BlockSpec((bt, 1, d_rnn), samp3)] + lru_w_specs + [_ANY],
        out_specs=[pl.BlockSpec((rows_s, d_rnn), samp_rows), pl.BlockSpec((bt, 1, d_rnn), samp3),
                   pl.BlockSpec((bt, CONV_W - 1, d_rnn), samp3)],
        out_shape=[jax.ShapeDtypeStruct((m, d_rnn), bf16), jax.ShapeDtypeStruct((n_sample, 1, d_rnn), f32),
                   jax.ShapeDtypeStruct((n_sample, CONV_W - 1, d_rnn), f32)],
        input_output_aliases={11: 0},
        compiler_params=_cparams(("parallel",)),
        name="lru_sample",
    )(xa, gg, scp, h0_s.reshape(n_sample, 1, d_rnn), *lru_w, oa)

    rn = lp['ret_norm'].reshape(hh, 1, dv)
    ct = RET_CHUNK
    nct = seq // ct
    tabs = _decay_tables(ct, 1)
    full3 = lambda shape: pl.BlockSpec(shape, lambda *_: (0, 0, 0))
    tab_specs = lambda r: [full3((hh, 1, dv)), full3((hh, r, r)), full3((hh, r, 1)), full3((hh, r, 1)), full3((hh, 1, 1))]
    assert ms % ct == 0 and ms // ct <= nct
    chunk_row = lambda b, c: jnp.minimum(b * nct + c, mp // ct - 1)
    ob, so_p = pl.pallas_call(
        functools.partial(_ret_prompt_kernel, n_seq=n_prompt),
        grid=(n_prompt + 1, nct),
        in_specs=[pl.BlockSpec((ct, qk_dim), lambda b, c: (chunk_row(b, c), 0)),
                  pl.BlockSpec((ct, qk_dim), lambda b, c: (chunk_row(b, c), 1)),
                  pl.BlockSpec((ct, v_dim), lambda b, c: (chunk_row(b, c), 0)),
                  pl.BlockSpec((ct, v_dim), lambda b, c: (chunk_row(b, c), 0))] + tab_specs(ct),
        out_specs=[pl.BlockSpec((ct, v_dim), lambda b, c: (jnp.minimum(b * nct + c, m // ct - 1), 0)),
                   pl.BlockSpec((1, hh, dk, dv), lambda b, c: (jnp.minimum(b, n_prompt - 1), 0, 0, 0))],
        out_shape=[jax.ShapeDtypeStruct((m, v_dim), bf16), jax.ShapeDtypeStruct((n_prompt, hh, dk, dv), f32)],
        compiler_params=_cparams(("arbitrary", "arbitrary")),
        name="ret_prompt",
    )(qk, qk, v, sg, rn, *tabs)

    pr = SAMPLE_PAIR
    rows_r = pr * dec_seq
    r_off = mp // rows_r
    tabs = _decay_tables(dec_seq, pr)
    pair_rows = lambda p: (r_off + p, 0)
    pair_state = lambda p: (p, 0, 0, 0)
    ob, so_s = pl.pallas_call(
        functools.partial(_ret_sample_kernel, t_len=dec_seq),
        grid=(n_sample // pr,),
        in_specs=[pl.BlockSpec((rows_r, qk_dim), pair_rows), pl.BlockSpec((rows_r, qk_dim), lambda p: (r_off + p, 1)),
                  pl.BlockSpec((rows_r, v_dim), pair_rows), pl.BlockSpec((rows_r, v_dim), pair_rows),
                  pl.BlockSpec((pr, hh, dk, dv), pair_state)] + tab_specs(rows_r) + [_ANY],
        out_specs=[pl.BlockSpec((rows_r, v_dim), pair_rows), pl.BlockSpec((pr, hh, dk, dv), pair_state)],
        out_shape=[jax.ShapeDtypeStruct((m, v_dim), bf16), jax.ShapeDtypeStruct((n_sample, hh, dk, dv), f32)],
        input_output_aliases={10: 0},
        compiler_params=_cparams(("parallel",)),
        name="ret_sample",
    )(qk, qk, v, sg, ret_s, rn, *tabs, ob)

    tile = lambda j, i: (i, j)
    pa = _mm(oa, lp['proj_a'], col0=0, n_cols=d, out_dtype=f32, epilogue=lambda acc: acc, name="proj_a")
    tnb = 512
    merged = _mm(ob, lp['proj_b'], col0=0, n_cols=d, out_dtype=bf16, tn=tnb,
                 epilogue=lambda acc, pa_ref, ga_ref, gb_ref: (ga_ref[...].astype(f32) * pa_ref[...]
                                                               + gb_ref[...].astype(f32) * acc),
                 extras=[(pa, (tm, tnb), tile), (gates, (tm, tnb), tile),
                         (gates, (tm, tnb), lambda j, i: (i, j + d // tnb))], name="proj_b_merge")
    x2 = _mm(merged, lp['w_out'], col0=0, n_cols=d, out_dtype=f32,
             epilogue=lambda acc, x_ref: x_ref[...] + acc, extras=[(x1, (tm, tn), tile)], name="w_out")

    x3 = _ffn([(x2, 0)], m, lp['ffn2_norm'], lp['ffn2_wg'], lp['ffn2_wu'], lp['ffn2_wd'])

    tp = 512
    ple_dim = pe_p.shape[1]
    fin = final_norm is not None
    fn = (final_norm if fin else lp['ple_norm']).reshape(1, d)
    npt = mp // tp
    c2 = lambda i: (0, 0)
    prow = lambda i: (jnp.minimum(i, npt - 1), 0)
    srow = lambda i: (jnp.maximum(i - npt, 0), 0)
    y_p, y_s = pl.pallas_call(
        functools.partial(_ple_kernel, n_prompt_tiles=npt, final=fin),
        grid=(m // tp,),
        in_specs=[pl.BlockSpec((tp, d), lambda i: (i, 0)), pl.BlockSpec((tp, ple_dim), prow),
                  pl.BlockSpec((tp, ple_dim), srow), pl.BlockSpec((1, d), c2), pl.BlockSpec((d, d), c2),
                  pl.BlockSpec((1, d), c2), pl.BlockSpec((ple_dim, d), c2), pl.BlockSpec((1, d), c2)],
        out_specs=[pl.BlockSpec((tp, d), prow), pl.BlockSpec((tp, d), srow)],
        out_shape=[jax.ShapeDtypeStruct((mp, d), f32), jax.ShapeDtypeStruct((ms, d), f32)],
        compiler_params=_cparams(("arbitrary",)),
        name="ple",
    )(x3, pe_p, pe_s, lp['ple_norm'].reshape(1, d), _to_bf16(lp['ple_wg']), lp['ple_bg'].reshape(1, d),
      _to_bf16(lp['ple_proj']), fn)

    states = (hl_p.reshape(n_prompt, d_rnn), cn_p, so_p, hl_s.reshape(n_sample, d_rnn), cn_s, so_s)
    return y_p, y_s, states


def kernel(x_prompt, x_sample, p_prompt, p_sample, state_lru, state_conv, state_ret, ffn1_norm, ffn1_wg, ffn1_wu, ffn1_wd, mix_norm, w_in, b_in, conv_w, conv_b, lru_wa, lru_ba, lru_wx, lru_bx, lru_lambda, ret_norm, proj_a, proj_b, w_out, ffn2_norm, ffn2_wg, ffn2_wu, ffn2_wd, ple_norm, ple_wg, ple_bg, ple_proj, final_norm):
    params = dict(ffn1_norm=ffn1_norm, ffn1_wg=ffn1_wg, ffn1_wu=ffn1_wu, ffn1_wd=ffn1_wd, mix_norm=mix_norm,
                  w_in=w_in, b_in=b_in, conv_w=conv_w, conv_b=conv_b, lru_wa=lru_wa, lru_ba=lru_ba, lru_wx=lru_wx,
                  lru_bx=lru_bx, lru_lambda=lru_lambda, ret_norm=ret_norm, proj_a=proj_a, proj_b=proj_b,
                  w_out=w_out, ffn2_norm=ffn2_norm, ffn2_wg=ffn2_wg, ffn2_wu=ffn2_wu, ffn2_wd=ffn2_wd,
                  ple_norm=ple_norm, ple_wg=ple_wg, ple_bg=ple_bg, ple_proj=ple_proj)
    depth = w_in.shape[0]
    n_prompt, seq, d = x_prompt.shape
    n_sample, dec_seq, _ = x_sample.shape
    mp, ms = n_prompt * seq, n_sample * dec_seq
    dk = d // N_RET_HEADS

    y_p = x_prompt.astype(f32).reshape(mp, d)
    y_s = x_sample.astype(f32).reshape(ms, d)
    cos_t, sin_t = _rope_table(seq, dec_seq, ms, dk // 2)
    outs = [[] for _ in range(6)]
    for i in range(depth):
        lp = {k: v[i].astype(f32) for k, v in params.items()}
        y_p, y_s, st = _layer([(y_p, 0), (y_s, mp // ROW_TILE)], p_prompt[i].astype(f32).reshape(mp, -1),
                              p_sample[i].astype(f32).reshape(ms, -1), n_prompt, seq, n_sample, dec_seq,
                              state_lru[i].astype(f32), state_conv[i].astype(f32), state_ret[i].astype(f32),
                              cos_t, sin_t, lp, final_norm.astype(f32) if i == depth - 1 else None)
        for o, s in zip(outs, st):
            o.append(s)
    y_prompt = y_p.reshape(n_prompt, seq, d).astype(x_prompt.dtype)
    y_sample = y_s.reshape(n_sample, dec_seq, d).astype(x_sample.dtype)
    lru_p, conv_p, ret_p, lru_s, conv_s, ret_s = (jnp.stack(o) for o in outs)
    return (y_prompt, y_sample, lru_p.astype(state_lru.dtype), conv_p.astype(state_conv.dtype),
            ret_p.astype(state_ret.dtype), lru_s.astype(state_lru.dtype), conv_s.astype(state_conv.dtype),
            ret_s.astype(state_ret.dtype))
```

```python
import functools

import jax
import jax.numpy as jnp
from jax import lax
from jax.experimental import pallas as pl
from jax.experimental.pallas import tpu as pltpu

f32 = jnp.float32
bf16 = jnp.bfloat16

N_LRU_BLOCKS = 8
CONV_W = 4
LRU_C = 8.0
N_RET_HEADS = 8
ROPE_BASE = 10000.0
EPS = 1e-6
PAST_LEN = 16384

RET_CHUNK = 256
SAMPLE_PAIR = 2
ROW_TILE = 1024
VMEM_LIMIT = 58 * 1024 * 1024


def _cparams(sem):
    return pltpu.CompilerParams(dimension_semantics=sem, vmem_limit_bytes=VMEM_LIMIT)


def _sigmoid(x):
    return 1.0 / (1.0 + jnp.exp(-x))


def _rms(x, g):
    return x * lax.rsqrt(jnp.mean(x * x, axis=-1, keepdims=True) + EPS) * g


def _dot(a, b):
    return jnp.dot(a, b, preferred_element_type=f32)


def _dot_nt(a, b):
    return lax.dot_general(a, b, (((1,), (1,)), ((), ())), preferred_element_type=f32)


def _dot_tn(a, b):
    return lax.dot_general(a, b, (((0,), (0,)), ((), ())), preferred_element_type=f32)


_ANY = pl.BlockSpec(memory_space=pl.ANY)


def _cast_kernel(x_ref, o_ref):
    o_ref[...] = x_ref[...].astype(o_ref.dtype)


def _to_bf16(w):
    w2 = w.reshape(-1, w.shape[-1])
    return pl.pallas_call(_cast_kernel, out_shape=jax.ShapeDtypeStruct(w2.shape, bf16), name="to_bf16")(w2).reshape(w.shape)


def _rope_table_kernel(inv_ref, cos_ref, sin_ref, *, seq, dec_seq):
    rows = cos_ref.shape[0]
    r = lax.broadcasted_iota(jnp.int32, (rows, inv_ref.shape[1]), 0)
    pos = jnp.where(r < seq, r, PAST_LEN + lax.rem(r - seq, dec_seq))
    ang = pos.astype(f32) * inv_ref[...]
    cos_ref[...] = jnp.cos(ang)
    sin_ref[...] = jnp.sin(ang)


def _rope_table(seq, dec_seq, n_sample_rows, half):
    inv = (ROPE_BASE ** (-jnp.arange(half, dtype=f32) / half)).reshape(1, half)
    rows = seq + n_sample_rows
    return pl.pallas_call(
        functools.partial(_rope_table_kernel, seq=seq, dec_seq=dec_seq),
        out_shape=(jax.ShapeDtypeStruct((rows, half), f32), jax.ShapeDtypeStruct((rows, half), f32)),
        name="rope_table",
    )(inv)


def _ffn_kernel(*refs, group_tiles, emit_norm):
    n_groups = len(group_tiles)
    x_hbm = refs[:n_groups]
    g_ref, wg_ref, wu_ref, wd_ref = refs[n_groups:n_groups + 4]
    rest = list(refs[n_groups + 4:])
    g2_ref = rest.pop(0) if emit_norm else None
    xo_ref = rest.pop(0)
    u2_ref = rest.pop(0) if emit_norm else None
    (u_scr,) = rest
    i = pl.program_id(0)
    f = pl.program_id(1)
    tm = xo_ref.shape[0]

    @pl.when(f == 0)
    def _():
        for gi, lo in enumerate(group_tiles):
            mine = i >= lo
            if gi + 1 < n_groups:
                mine = mine & (i < group_tiles[gi + 1])

            @pl.when(mine)
            def _(gi=gi, lo=lo):
                pltpu.sync_copy(x_hbm[gi].at[pl.ds(pl.multiple_of((i - lo) * tm, tm), tm), :], xo_ref)

        u_scr[...] = _rms(xo_ref[...], g_ref[...]).astype(bf16)

    u = u_scr[...]
    hg = _dot(u, wg_ref[...].astype(bf16))
    hu = _dot(u, wu_ref[...].astype(bf16))
    h = (0.5 * hg * _sigmoid(hg) * hu).astype(bf16)
    xo_ref[...] += _dot(h, wd_ref[...].astype(bf16))

    if emit_norm:
        @pl.when(f == pl.num_programs(1) - 1)
        def _():
            u2_ref[...] = _rms(xo_ref[...], g2_ref[...]).astype(bf16)


def _ffn(parts, m_total, g, wg, wu, wd, g2=None, *, tm=ROW_TILE, tf=512):
    d = wg.shape[0]
    ff = wg.shape[1]
    emit_norm = g2 is not None
    row = lambda i, f: (i, 0)
    out_mode = dict(pipeline_mode=pl.Buffered(1)) if emit_norm else {}
    in_specs = [_ANY] * len(parts) + [
        pl.BlockSpec((1, d), lambda i, f: (0, 0)),
        pl.BlockSpec((d, tf), lambda i, f: (0, f)),
        pl.BlockSpec((d, tf), lambda i, f: (0, f)),
        pl.BlockSpec((tf, d), lambda i, f: (f, 0)),
    ]
    args = [xp for xp, _ in parts] + [g.reshape(1, d), wg, wu, wd]
    out_shape = [jax.ShapeDtypeStruct((m_total, d), f32)]
    out_specs = [pl.BlockSpec((tm, d), row, **out_mode)]
    if emit_norm:
        in_specs.append(pl.BlockSpec((1, d), lambda i, f: (0, 0)))
        args.append(g2.reshape(1, d))
        out_shape.append(jax.ShapeDtypeStruct((m_total, d), bf16))
        out_specs.append(pl.BlockSpec((tm, d), row, **out_mode))
    outs = pl.pallas_call(
        functools.partial(_ffn_kernel, group_tiles=tuple(t0 for _, t0 in parts), emit_norm=emit_norm),
        grid=(m_total // tm, ff // tf),
        in_specs=in_specs,
        out_specs=out_specs,
        out_shape=out_shape,
        scratch_shapes=[pltpu.VMEM((tm, d), bf16)],
        compiler_params=_cparams(("arbitrary", "arbitrary")),
        name="ffn",
    )(*args)
    return outs if emit_norm else outs[0]


def _mm_kernel(a_ref, b_ref, *rest, epilogue):
    *extra, o_ref, bw_scr = rest

    @pl.when(pl.program_id(1) == 0)
    def _():
        bw_scr[...] = b_ref[...].astype(bf16)

    o_ref[...] = epilogue(_dot(a_ref[...], bw_scr[...]), *extra).astype(o_ref.dtype)


def _mm(a, b, *, col0, n_cols, out_dtype, epilogue, extras=(), tm=ROW_TILE, tn=1024, name="mm"):
    m, k = a.shape
    off = col0 // tn
    in_specs = [
        pl.BlockSpec((tm, k), lambda j, i: (i, 0)),
        pl.BlockSpec((k, tn), lambda j, i: (0, j + off)),
    ]
    args = [a, b]
    for arr, blk, imap in extras:
        in_specs.append(pl.BlockSpec(blk, imap))
        args.append(arr)
    return pl.pallas_call(
        functools.partial(_mm_kernel, epilogue=epilogue),
        grid=(n_cols // tn, m // tm),
        in_specs=in_specs,
        out_specs=pl.BlockSpec((tm, tn), lambda j, i: (i, j)),
        out_shape=jax.ShapeDtypeStruct((m, n_cols), out_dtype),
        scratch_shapes=[pltpu.VMEM((k, tn), bf16)],
        compiler_params=_cparams(("parallel", "arbitrary")),
        name=name,
    )(*args)


def _gelu_tanh(x):
    return 0.5 * x * (1.0 + jnp.tanh(0.7978845608028654 * (x + 0.044715 * (x * x * x))))


def _rope_epilogue(acc, bias_ref, cos_ref, sin_ref, *, k_tile0, k_scale, head_dim):
    z = acc + bias_ref[...]
    cos = cos_ref[...]
    sin = sin_ref[...]
    half = head_dim // 2
    parts = []
    for h0 in range(0, z.shape[1], head_dim):
        x1 = z[:, h0:h0 + half]
        x2 = z[:, h0 + half:h0 + head_dim]
        parts += [x1 * cos - x2 * sin, x2 * cos + x1 * sin]
    scale = jnp.where(pl.program_id(0) >= k_tile0, k_scale, 1.0).astype(f32)
    return jnp.concatenate(parts, axis=1) * scale


def _lru_gates(xc, n, cs, wa_ref, wx_ref, ba_ref, bx_ref, lam_ref):
    xcb = xc.astype(bf16)
    r = _sigmoid(_dot(xcb, wa_ref[n]) + ba_ref[:, cs])
    gi = _sigmoid(_dot(xcb, wx_ref[n]) + bx_ref[:, cs])
    log_a = -LRU_C * r * jax.nn.softplus(-lam_ref[:, cs])
    a = jnp.exp(log_a)
    a2 = a * a
    one_minus = jnp.where(log_a < -0.25, 1.0 - a2, -jnp.tanh(log_a) * (a2 + 1.0))
    return a, one_minus * lax.rsqrt(jnp.maximum(one_minus, 1e-36)), gi


def _lru_prompt_kernel(*refs, n_seq):
    oa_ref = refs[9]

    @pl.when(pl.program_id(0) >= n_seq)
    def _():
        oa_ref[...] = jnp.zeros_like(oa_ref)

    @pl.when(pl.program_id(0) < n_seq)
    def _():
        _lru_prompt_body(*refs)


def _lru_prompt_body(xa_ref, gg_ref, cw_ref, cb_ref, wa_ref, wx_ref, ba_ref, bx_ref, lam_ref,
                     oa_ref, hl_ref, cn_ref, xe_scr, a_scr, b_scr, h_scr, hc_scr):
    tt, c = xa_ref.shape
    blk = c // N_LRU_BLOCKS
    hist = 8
    t = pl.program_id(1)

    @pl.when(t == 0)
    def _():
        xe_scr[0:hist, :] = jnp.zeros((hist, c), f32)
        hc_scr[...] = jnp.zeros_like(hc_scr)

    xe_scr[hist:hist + tt, :] = xa_ref[...]
    pos = lax.broadcasted_iota(jnp.int32, (tt, 1), 0) + t * tt
    for n in range(N_LRU_BLOCKS):
        cs = slice(n * blk, (n + 1) * blk)
        ext = xe_scr[:, cs]
        xc = cb_ref[:, cs] + cw_ref[CONV_W - 1:CONV_W, cs] * ext[hist:, :]
        for s in range(1, CONV_W):
            xc = xc + cw_ref[CONV_W - 1 - s:CONV_W - s, cs] * pltpu.roll(ext, s, axis=0)[hist:, :]
        a, mult, gi = _lru_gates(xc, n, cs, wa_ref, wx_ref, ba_ref, bx_ref, lam_ref)
        mult = jnp.where(pos == 0, 1.0, mult)
        a_scr[:, cs] = a
        b_scr[:, cs] = mult * (gi * xc)

    def step(i, h):
        h = a_scr[pl.ds(i, 1), :] * h + b_scr[pl.ds(i, 1), :]
        h_scr[pl.ds(i, 1), :] = h
        return h

    h = lax.fori_loop(0, tt, step, hc_scr[...], unroll=8)
    hc_scr[...] = h
    hl_ref[0] = h
    for n in range(N_LRU_BLOCKS):
        cs = slice(n * blk, (n + 1) * blk)
        oa_ref[:, cs] = (h_scr[:, cs] * gg_ref[:, cs].astype(f32)).astype(bf16)
    cn_ref[0] = xe_scr[pl.ds(hist + tt - (CONV_W - 1), CONV_W - 1), :]
    xe_scr[0:hist, :] = xe_scr[pl.ds(tt, hist), :]


def _lru_sample_kernel(xa_ref, gg_ref, scp_ref, h0_ref, cw_ref, cb_ref, wa_ref, wx_ref, ba_ref, bx_ref, lam_ref,
                       oa_prev_ref, oa_ref, hl_ref, cn_ref):
    del oa_prev_ref
    rows, c = xa_ref.shape
    bt, t_len = scp_ref.shape[0], scp_ref.shape[1]
    blk = c // N_LRU_BLOCKS
    tpos = lax.broadcasted_iota(jnp.int32, (bt, t_len, blk), 1)
    for n in range(N_LRU_BLOCKS):
        cs = slice(n * blk, (n + 1) * blk)
        x3 = xa_ref[:, cs].reshape(bt, t_len, blk)
        ext = jnp.concatenate([scp_ref[:, :, cs], x3], axis=1)
        xc3 = cb_ref[:, cs] + cw_ref[CONV_W - 1:CONV_W, cs] * x3
        for s in range(1, CONV_W):
            xc3 = xc3 + cw_ref[CONV_W - 1 - s:CONV_W - s, cs] * pltpu.roll(ext, s, axis=1)[:, t_len:, :]
        xc = xc3.reshape(rows, blk)
        a, mult, gi = _lru_gates(xc, n, cs, wa_ref, wx_ref, ba_ref, bx_ref, lam_ref)
        a3 = a.reshape(bt, t_len, blk)
        b3 = (mult * (gi * xc)).reshape(bt, t_len, blk)
        d = 1
        while d < t_len:
            keep = tpos >= d
            b3 = jnp.where(keep, a3 * pltpu.roll(b3, d, axis=1) + b3, b3)
            a3 = jnp.where(keep, a3 * pltpu.roll(a3, d, axis=1), a3)
            d *= 2
        h3 = b3 + a3 * h0_ref[:, :, cs]
        oa_ref[:, cs] = (h3.reshape(rows, blk) * gg_ref[:, cs].astype(f32)).astype(bf16)
        hl_ref[:, :, cs] = h3[:, t_len - 1:t_len, :]
        cn_ref[:, :, cs] = pltpu.roll(x3, CONV_W - 1, axis=1)[:, 0:CONV_W - 1, :]


def _ret_norm_gate(o, rn, sg):
    return (o * lax.rsqrt(jnp.mean(o * o, axis=-1, keepdims=True) + EPS) * rn * sg.astype(f32)).astype(bf16)


def _ret_prompt_kernel(*refs, n_seq):
    ob_ref = refs[9]

    @pl.when(pl.program_id(0) >= n_seq)
    def _():
        ob_ref[...] = jnp.zeros_like(ob_ref)

    @pl.when(pl.program_id(0) < n_seq)
    def _():
        _ret_prompt_body(*refs)


def _ret_prompt_body(q_ref, k_ref, v_ref, sg_ref, rn_ref, dm_ref, cd_ref, sd_ref, chd_ref, ob_ref, so_ref):
    c = pl.program_id(1)
    dk, dv = so_ref.shape[2], so_ref.shape[3]

    @pl.when(c == 0)
    def _():
        so_ref[...] = jnp.zeros_like(so_ref)

    for h in range(N_RET_HEADS):
        ks = slice(h * dk, (h + 1) * dk)
        vs = slice(h * dv, (h + 1) * dv)
        q = q_ref[:, ks]
        k = k_ref[:, ks]
        v = v_ref[:, vs]
        s_old = so_ref[0, h]
        scores = _dot_nt(q, k) * dm_ref[h]
        inner = _dot(scores.astype(bf16), v)
        cross = _dot(q, s_old.astype(bf16)) * cd_ref[h]
        kd = (k.astype(f32) * sd_ref[h]).astype(bf16)
        so_ref[0, h] = s_old * chd_ref[h] + _dot_tn(kd, v)
        ob_ref[:, vs] = _ret_norm_gate(inner + cross, rn_ref[h], sg_ref[:, vs])


def _ret_sample_kernel(q_ref, k_ref, v_ref, sg_ref, s_ref, rn_ref, dm_ref, cd_ref, sd_ref, chd_ref, ob_prev_ref,
                       ob_ref, so_ref, *, t_len):
    del ob_prev_ref
    rows = q_ref.shape[0]
    dk, dv = so_ref.shape[2], so_ref.shape[3]
    row = lax.broadcasted_iota(jnp.int32, (rows, 1), 0)
    for h in range(N_RET_HEADS):
        ks = slice(h * dk, (h + 1) * dk)
        vs = slice(h * dv, (h + 1) * dv)
        q = q_ref[:, ks]
        k = k_ref[:, ks]
        v = v_ref[:, vs]
        scores = _dot_nt(q, k) * dm_ref[h]
        o = _dot(scores.astype(bf16), v)
        kd = k.astype(f32) * sd_ref[h]
        for bb in range(rows // t_len):
            mine = (row >= bb * t_len) & (row < (bb + 1) * t_len)
            s_old = s_ref[bb, h]
            cross = _dot(q, s_old.astype(bf16)) * cd_ref[h]
            o = o + jnp.where(mine, cross, 0.0)
            so_ref[bb, h] = s_old * chd_ref[h] + _dot_tn(jnp.where(mine, kd, 0.0).astype(bf16), v)
        ob_ref[:, vs] = _ret_norm_gate(o, rn_ref[h], sg_ref[:, vs])


def _decay_tables(chunk, n_seq):
    log_g = jnp.log1p(-jnp.exp2(-5.0 - jnp.arange(N_RET_HEADS, dtype=f32)))
    r = jnp.arange(chunk * n_seq)
    idx = (r % chunk).astype(f32)
    same = (r[:, None] // chunk) == (r[None, :] // chunk)
    diff = idx[:, None] - idx[None, :]
    dmask = jnp.where(same & (diff >= 0), jnp.exp(jnp.maximum(diff, 0.0)[None] * log_g[:, None, None]), 0.0)
    cross_decay = jnp.exp((idx[None] + 1.0) * log_g[:, None])[..., None]
    state_decay = jnp.exp((chunk - 1.0 - idx[None]) * log_g[:, None])[..., None]
    chunk_decay = jnp.exp(chunk * log_g)[:, None, None]
    return dmask, cross_decay, state_decay, chunk_decay


def _ple_kernel(x_ref, pep_ref, pes_ref, gn_ref, wg_ref, bg_ref, pp_ref, fn_ref, yp_ref, ys_ref, *, n_prompt_tiles, final):
    i = pl.program_id(0)
    x = x_ref[...]
    u = _rms(x, gn_ref[...]).astype(bf16)
    gate = _sigmoid(_dot(u, wg_ref[...]) + bg_ref[...])
    pe = jnp.where(i < n_prompt_tiles, pep_ref[...], pes_ref[...]).astype(bf16)
    x = x + gate * _dot(pe, pp_ref[...])
    y = _rms(x, fn_ref[...]) if final else x

    @pl.when(i < n_prompt_tiles)
    def _():
        yp_ref[...] = y

    @pl.when(i >= n_prompt_tiles)
    def _():
        ys_ref[...] = y


def _layer(x_parts, pe_p, pe_s, n_prompt, seq, n_sample, dec_seq, h0_s, conv_s, ret_s, cos_t, sin_t, lp, final_norm):
    d = lp['mix_norm'].shape[0]
    mp, ms = n_prompt * seq, n_sample * dec_seq
    m = mp + ms
    d_rnn = lp['conv_w'].shape[1]
    hh = N_RET_HEADS
    dk = d // hh
    dv = lp['ret_norm'].shape[1]
    qk_dim, v_dim = hh * dk, hh * dv
    tm, tn = ROW_TILE, 1024

    x1, u = _ffn(x_parts, m, lp['ffn1_norm'], lp['ffn1_wg'], lp['ffn1_wu'], lp['ffn1_wd'], lp['mix_norm'])

    w_in = lp['w_in']
    b_in = lp['b_in'].reshape(1, -1)

    def bias_extra(col0):
        return (b_in, (1, tn), lambda j, i, o=col0 // tn: (0, j + o))

    def with_bias(fn):
        return lambda acc, b_ref: fn(acc + b_ref[...])

    c_xa, c_ga, c_q, c_v = 0, d_rnn, 2 * d_rnn, 2 * d_rnn + 2 * qk_dim
    c_gr, c_gate = c_v + v_dim, c_v + 2 * v_dim
    proj = functools.partial(_mm, u, w_in, tm=tm, tn=tn)
    xa = proj(col0=c_xa, n_cols=d_rnn, out_dtype=f32, epilogue=with_bias(lambda z: z),
              extras=[bias_extra(c_xa)], name="inproj_xa")
    gg = proj(col0=c_ga, n_cols=d_rnn, out_dtype=bf16, epilogue=with_bias(_gelu_tanh),
              extras=[bias_extra(c_ga)], name="inproj_ga")
    n_pt = mp // tm
    half = dk // 2
    rope_map = lambda j, i: (jnp.where(i < n_pt, i % (seq // tm), seq // tm + i - n_pt), 0)
    qk = proj(col0=c_q, n_cols=2 * qk_dim, out_dtype=bf16,
              epilogue=functools.partial(_rope_epilogue, k_tile0=qk_dim // tn, k_scale=dk ** -0.5, head_dim=dk),
              extras=[bias_extra(c_q), (cos_t, (tm, half), rope_map), (sin_t, (tm, half), rope_map)],
              name="inproj_qk")
    tm_wide = 3 * tm // 2 if m % (3 * tm // 2) == 0 else tm
    v = proj(col0=c_v, n_cols=v_dim, out_dtype=bf16, epilogue=with_bias(lambda z: z),
             extras=[bias_extra(c_v)], tm=tm_wide, name="inproj_v")
    sg = proj(col0=c_gr, n_cols=v_dim, out_dtype=bf16, epilogue=with_bias(lambda z: z * _sigmoid(z)),
              extras=[bias_extra(c_gr)], tm=tm_wide, name="inproj_gr")
    gates = proj(col0=c_gate, n_cols=2 * d, out_dtype=bf16, epilogue=with_bias(_sigmoid),
                 extras=[bias_extra(c_gate)], tm=tm_wide, name="inproj_gates")

    row2 = lambda a: a.reshape(1, -1)
    lru_w = [lp['conv_w'], row2(lp['conv_b']), _to_bf16(lp['lru_wa']), _to_bf16(lp['lru_wx']), row2(lp['lru_ba']),
             row2(lp['lru_bx']), row2(lp['lru_lambda'])]
    nb = N_LRU_BLOCKS
    blk = d_rnn // nb
    z2 = lambda *_: (0, 0)
    z3 = lambda *_: (0, 0, 0)
    lru_w_specs = [pl.BlockSpec((CONV_W, d_rnn), z2), pl.BlockSpec((1, d_rnn), z2),
                   pl.BlockSpec((nb, blk, blk), z3), pl.BlockSpec((nb, blk, blk), z3),
                   pl.BlockSpec((1, d_rnn), z2), pl.BlockSpec((1, d_rnn), z2), pl.BlockSpec((1, d_rnn), z2)]

    tt = 256
    ntt = seq // tt
    assert ms % tt == 0 and ms // tt <= ntt
    seq_rows = lambda b, t: (jnp.minimum(b * ntt + t, mp // tt - 1), 0)
    all_rows = lambda b, t: (jnp.minimum(b * ntt + t, m // tt - 1), 0)
    per_seq = lambda b, t: (jnp.minimum(b, n_prompt - 1), 0, 0)
    oa, hl_p, cn_p = pl.pallas_call(
        functools.partial(_lru_prompt_kernel, n_seq=n_prompt),
        grid=(n_prompt + 1, ntt),
        in_specs=[pl.BlockSpec((tt, d_rnn), seq_rows), pl.BlockSpec((tt, d_rnn), seq_rows)] + lru_w_specs,
        out_specs=[pl.BlockSpec((tt, d_rnn), all_rows), pl.BlockSpec((1, 1, d_rnn), per_seq),
                   pl.BlockSpec((1, CONV_W - 1, d_rnn), per_seq)],
        out_shape=[jax.ShapeDtypeStruct((m, d_rnn), bf16), jax.ShapeDtypeStruct((n_prompt, 1, d_rnn), f32),
                   jax.ShapeDtypeStruct((n_prompt, CONV_W - 1, d_rnn), f32)],
        scratch_shapes=[pltpu.VMEM((tt + 8, d_rnn), f32), pltpu.VMEM((tt, d_rnn), f32), pltpu.VMEM((tt, d_rnn), f32),
                        pltpu.VMEM((tt, d_rnn), f32), pltpu.VMEM((1, d_rnn), f32)],
        compiler_params=_cparams(("arbitrary", "arbitrary")),
        name="lru_prompt",
    )(xa, gg, *lru_w)

    bt = 32
    rows_s = bt * dec_seq
    s_off = mp // rows_s
    scp = jnp.pad(conv_s, ((0, 0), (dec_seq - (CONV_W - 1), 0), (0, 0)))
    samp_rows = lambda i: (s_off + i, 0)
    samp3 = lambda i: (i, 0, 0)
    oa, hl_s, cn_s = pl.pallas_call(
        _lru_sample_kernel,
        grid=(n_sample // bt,),
        in_specs=[pl.BlockSpec((rows_s, d_rnn), samp_rows), pl.BlockSpec((rows_s, d_rnn), samp_rows),
                  pl.BlockSpec((bt, dec_seq, d_rnn), samp3), pl.BlockSpec((bt, 1, d_rnn), samp3)] + lru_w_specs + [_ANY],
        out_specs=[pl.BlockSpec((rows_s, d_rnn), samp_rows), pl.BlockSpec((bt, 1, d_rnn), samp3),
                   pl.BlockSpec((bt, CONV_W - 1, d_rnn), samp3)],
        out_shape=[jax.ShapeDtypeStruct((m, d_rnn), bf16), jax.ShapeDtypeStruct((n_sample, 1, d_rnn), f32),
                   jax.ShapeDtypeStruct((n_sample, CONV_W - 1, d_rnn), f32)],
        input_output_aliases={11: 0},
        compiler_params=_cparams(("parallel",)),
        name="lru_sample",
    )(xa, gg, scp, h0_s.reshape(n_sample, 1, d_rnn), *lru_w, oa)

    rn = lp['ret_norm'].reshape(hh, 1, dv)
    ct = RET_CHUNK
    nct = seq // ct
    tabs = _decay_tables(ct, 1)
    full3 = lambda shape: pl.BlockSpec(shape, lambda *_: (0, 0, 0))
    tab_specs = lambda r: [full3((hh, 1, dv)), full3((hh, r, r)), full3((hh, r, 1)), full3((hh, r, 1)), full3((hh, 1, 1))]
    assert ms % ct == 0 and ms // ct <= nct
    chunk_row = lambda b, c: jnp.minimum(b * nct + c, mp // ct - 1)
    ob, so_p = pl.pallas_call(
        functools.partial(_ret_prompt_kernel, n_seq=n_prompt),
        grid=(n_prompt + 1, nct),
        in_specs=[pl.BlockSpec((ct, qk_dim), lambda b, c: (chunk_row(b, c), 0)),
                  pl.BlockSpec((ct, qk_dim), lambda b, c: (chunk_row(b, c), 1)),
                  pl.BlockSpec((ct, v_dim), lambda b, c: (chunk_row(b, c), 0)),
                  pl.BlockSpec((ct, v_dim), lambda b, c: (chunk_row(b, c), 0))] + tab_specs(ct),
        out_specs=[pl.BlockSpec((ct, v_dim), lambda b, c: (jnp.minimum(b * nct + c, m // ct - 1), 0)),
                   pl.BlockSpec((1, hh, dk, dv), lambda b, c: (jnp.minimum(b, n_prompt - 1), 0, 0, 0))],
        out_shape=[jax.ShapeDtypeStruct((m, v_dim), bf16), jax.ShapeDtypeStruct((n_prompt, hh, dk, dv), f32)],
        compiler_params=_cparams(("arbitrary", "arbitrary")),
        name="ret_prompt",
    )(qk, qk, v, sg, rn, *tabs)

    pr = SAMPLE_PAIR
    rows_r = pr * dec_seq
    r_off = mp // rows_r
    tabs = _decay_tables(dec_seq, pr)
    pair_rows = lambda p: (r_off + p, 0)
    pair_state = lambda p: (p, 0, 0, 0)
    ob, so_s = pl.pallas_call(
        functools.partial(_ret_sample_kernel, t_len=dec_seq),
        grid=(n_sample // pr,),
        in_specs=[pl.BlockSpec((rows_r, qk_dim), pair_rows), pl.BlockSpec((rows_r, qk_dim), lambda p: (r_off + p, 1)),
                  pl.BlockSpec((rows_r, v_dim), pair_rows), pl.BlockSpec((rows_r, v_dim), pair_rows),
                  pl.BlockSpec((pr, hh, dk, dv), pair_state)] + tab_specs(rows_r) + [_ANY],
        out_specs=[pl.BlockSpec((rows_r, v_dim), pair_rows), pl.BlockSpec((pr, hh, dk, dv), pair_state)],
        out_shape=[jax.ShapeDtypeStruct((m, v_dim), bf16), jax.ShapeDtypeStruct((n_sample, hh, dk, dv), f32)],
        input_output_aliases={10: 0},
        compiler_params=_cparams(("parallel",)),
        name="ret_sample",
    )(qk, qk, v, sg, ret_s, rn, *tabs, ob)

    tile = lambda j, i: (i, j)
    pa = _mm(oa, lp['proj_a'], col0=0, n_cols=d, out_dtype=f32, epilogue=lambda acc: acc, name="proj_a")
    tnb = 512
    merged = _mm(ob, lp['proj_b'], col0=0, n_cols=d, out_dtype=bf16, tn=tnb,
                 epilogue=lambda acc, pa_ref, ga_ref, gb_ref: (ga_ref[...].astype(f32) * pa_ref[...]
                                                               + gb_ref[...].astype(f32) * acc),
                 extras=[(pa, (tm, tnb), tile), (gates, (tm, tnb), tile),
                         (gates, (tm, tnb), lambda j, i: (i, j + d // tnb))], name="proj_b_merge")
    x2 = _mm(merged, lp['w_out'], col0=0, n_cols=d, out_dtype=f32,
             epilogue=lambda acc, x_ref: x_ref[...] + acc, extras=[(x1, (tm, tn), tile)], name="w_out")

    x3 = _ffn([(x2, 0)], m, lp['ffn2_norm'], lp['ffn2_wg'], lp['ffn2_wu'], lp['ffn2_wd'])

    tp = 512
    ple_dim = pe_p.shape[1]
    fin = final_norm is not None
    fn = (final_norm if fin else lp['ple_norm']).reshape(1, d)
    npt = mp // tp
    c2 = lambda i: (0, 0)
    prow = lambda i: (jnp.minimum(i, npt - 1), 0)
    srow = lambda i: (jnp.maximum(i - npt, 0), 0)
    y_p, y_s = pl.pallas_call(
        functools.partial(_ple_kernel, n_prompt_tiles=npt, final=fin),
        grid=(m // tp,),
        in_specs=[pl.BlockSpec((tp, d), lambda i: (i, 0)), pl.BlockSpec((tp, ple_dim), prow),
                  pl.BlockSpec((tp, ple_dim), srow), pl.BlockSpec((1, d), c2), pl.BlockSpec((d, d), c2),
                  pl.BlockSpec((1, d), c2), pl.BlockSpec((ple_dim, d), c2), pl.BlockSpec((1, d), c2)],
        out_specs=[pl.BlockSpec((tp, d), prow), pl.BlockSpec((tp, d), srow)],
        out_shape=[jax.ShapeDtypeStruct((mp, d), f32), jax.ShapeDtypeStruct((ms, d), f32)],
        compiler_params=_cparams(("arbitrary",)),
        name="ple",
    )(x3, pe_p, pe_s, lp['ple_norm'].reshape(1, d), _to_bf16(lp['ple_wg']), lp['ple_bg'].reshape(1, d),
      _to_bf16(lp['ple_proj']), fn)

    states = (hl_p.reshape(n_prompt, d_rnn), cn_p, so_p, hl_s.reshape(n_sample, d_rnn), cn_s, so_s)
    return y_p, y_s, states


def kernel(x_prompt, x_sample, p_prompt, p_sample, state_lru, state_conv, state_ret, ffn1_norm, ffn1_wg, ffn1_wu, ffn1_wd, mix_norm, w_in, b_in, conv_w, conv_b, lru_wa, lru_ba, lru_wx, lru_bx, lru_lambda, ret_norm, proj_a, proj_b, w_out, ffn2_norm, ffn2_wg, ffn2_wu, ffn2_wd, ple_norm, ple_wg, ple_bg, ple_proj, final_norm):
    params = dict(ffn1_norm=ffn1_norm, ffn1_wg=ffn1_wg, ffn1_wu=ffn1_wu, ffn1_wd=ffn1_wd, mix_norm=mix_norm,
                  w_in=w_in, b_in=b_in, conv_w=conv_w, conv_b=conv_b, lru_wa=lru_wa, lru_ba=lru_ba, lru_wx=lru_wx,
                  lru_bx=lru_bx, lru_lambda=lru_lambda, ret_norm=ret_norm, proj_a=proj_a, proj_b=proj_b,
                  w_out=w_out, ffn2_norm=ffn2_norm, ffn2_wg=ffn2_wg, ffn2_wu=ffn2_wu, ffn2_wd=ffn2_wd,
                  ple_norm=ple_norm, ple_wg=ple_wg, ple_bg=ple_bg, ple_proj=ple_proj)
    depth = w_in.shape[0]
    n_prompt, seq, d = x_prompt.shape
    n_sample, dec_seq, _ = x_sample.shape
    mp, ms = n_prompt * seq, n_sample * dec_seq
    dk = d // N_RET_HEADS

    y_p = x_prompt.astype(f32).reshape(mp, d)
    y_s = x_sample.astype(f32).reshape(ms, d)
    cos_t, sin_t = _rope_table(seq, dec_seq, ms, dk // 2)
    outs = [[] for _ in range(6)]
    for i in range(depth):
        lp = {k: v[i].astype(f32) for k, v in params.items()}
        y_p, y_s, st = _layer([(y_p, 0), (y_s, mp // ROW_TILE)], p_prompt[i].astype(f32).reshape(mp, -1),
                              p_sample[i].astype(f32).reshape(ms, -1), n_prompt, seq, n_sample, dec_seq,
                              state_lru[i].astype(f32), state_conv[i].astype(f32), state_ret[i].astype(f32),
                              cos_t, sin_t, lp, final_norm.astype(f32) if i == depth - 1 else None)
        for o, s in zip(outs, st):
            o.append(s)
    y_prompt = y_p.reshape(n_prompt, seq, d).astype(x_prompt.dtype)
    y_sample = y_s.reshape(n_sample, dec_seq, d).astype(x_sample.dtype)
    lru_p, conv_p, ret_p, lru_s, conv_s, ret_s = (jnp.stack(o) for o in outs)
    return (y_prompt, y_sample, lru_p.astype(state_lru.dtype), conv_p.astype(state_conv.dtype),
            ret_p.astype(state_ret.dtype), lru_s.astype(state_lru.dtype), conv_s.astype(state_conv.dtype),
            ret_s.astype(state_ret.dtype))
```

```python
import functools

import jax
import jax.numpy as jnp
from jax import lax
from jax.experimental import pallas as pl
from jax.experimental.pallas import tpu as pltpu

f32 = jnp.float32
bf16 = jnp.bfloat16

N_LRU_BLOCKS = 8
CONV_W = 4
LRU_C = 8.0
N_RET_HEADS = 8
ROPE_BASE = 10000.0
EPS = 1e-6
PAST_LEN = 16384

RET_CHUNK = 256
SAMPLE_PAIR = 2
ROW_TILE = 1024
VMEM_LIMIT = 58 * 1024 * 1024


def _cparams(sem):
    return pltpu.CompilerParams(dimension_semantics=sem, vmem_limit_bytes=VMEM_LIMIT)


def _sigmoid(x):
    return 1.0 / (1.0 + jnp.exp(-x))


def _rms(x, g):
    return x * lax.rsqrt(jnp.mean(x * x, axis=-1, keepdims=True) + EPS) * g


def _dot(a, b):
    return jnp.dot(a, b, preferred_element_type=f32)


def _dot_nt(a, b):
    return lax.dot_general(a, b, (((1,), (1,)), ((), ())), preferred_element_type=f32)


def _dot_tn(a, b):
    return lax.dot_general(a, b, (((0,), (0,)), ((), ())), preferred_element_type=f32)


_ANY = pl.BlockSpec(memory_space=pl.ANY)


def _cast_kernel(x_ref, o_ref):
    o_ref[...] = x_ref[...].astype(o_ref.dtype)


def _to_bf16(w):
    w2 = w.reshape(-1, w.shape[-1])
    return pl.pallas_call(_cast_kernel, out_shape=jax.ShapeDtypeStruct(w2.shape, bf16), name="to_bf16")(w2).reshape(w.shape)


def _rope_table_kernel(inv_ref, cos_ref, sin_ref, *, seq, dec_seq):
    rows = cos_ref.shape[0]
    r = lax.broadcasted_iota(jnp.int32, (rows, inv_ref.shape[1]), 0)
    pos = jnp.where(r < seq, r, PAST_LEN + lax.rem(r - seq, dec_seq))
    ang = pos.astype(f32) * inv_ref[...]
    cos_ref[...] = jnp.cos(ang)
    sin_ref[...] = jnp.sin(ang)


def _rope_table(seq, dec_seq, n_sample_rows, half):
    inv = (ROPE_BASE ** (-jnp.arange(half, dtype=f32) / half)).reshape(1, half)
    rows = seq + n_sample_rows
    return pl.pallas_call(
        functools.partial(_rope_table_kernel, seq=seq, dec_seq=dec_seq),
        out_shape=(jax.ShapeDtypeStruct((rows, half), f32), jax.ShapeDtypeStruct((rows, half), f32)),
        name="rope_table",
    )(inv)


def _ffn_kernel(*refs, group_rows, n_tiles, emit_norm):
    n_groups = len(group_rows)
    x_hbm = refs[:n_groups]
    g_ref, wg_ref, wu_ref, wd_ref = refs[n_groups:n_groups + 4]
    rest = list(refs[n_groups + 4:])
    g2_ref = rest.pop(0) if emit_norm else None
    xo_ref = rest.pop(0)
    u2_ref = rest.pop(0) if emit_norm else None
    (u_scr,) = rest
    i = pl.program_id(0)
    f = pl.program_id(1)
    tm = xo_ref.shape[0]

    @pl.when(f == 0)
    def _():
        for t in range(n_tiles):
            @pl.when(i == t)
            def _(t=t):
                for gi, (r0, nr) in enumerate(group_rows):
                    lo, hi = max(t * tm, r0), min((t + 1) * tm, r0 + nr)
                    if lo < hi:
                        pltpu.sync_copy(x_hbm[gi].at[lo - r0:hi - r0, :], xo_ref.at[lo - t * tm:hi - t * tm, :])

        u_scr[...] = _rms(xo_ref[...], g_ref[...]).astype(bf16)

    u = u_scr[...]
    hg = _dot(u, wg_ref[...].astype(bf16))
    hu = _dot(u, wu_ref[...].astype(bf16))
    h = (0.5 * hg * _sigmoid(hg) * hu).astype(bf16)
    xo_ref[...] += _dot(h, wd_ref[...].astype(bf16))

    if emit_norm:
        @pl.when(f == pl.num_programs(1) - 1)
        def _():
            u2_ref[...] = _rms(xo_ref[...], g2_ref[...]).astype(bf16)


FFN_COL_TILE = 256
FFN_VMEM_BUDGET = 52 * 1024 * 1024


def _ffn_row_tile(m, d, with_norm, tf=FFN_COL_TILE):
    best = None
    for tm in range(256, m + 1, 256):
        if m % tm == 0:
            rows = tm * d * (4 + 2 + (2 if with_norm else 0))
            weights = 3 * d * tf * 4 * 2
            temps = 2 * tm * tf * 4 + tm * tf * 2 + 3 * d * tf * 2
            if rows + weights + temps <= FFN_VMEM_BUDGET:
                best = tm
    return best


def _ffn(parts, m_total, g, wg, wu, wd, g2=None, *, tm, tf=FFN_COL_TILE):
    d = wg.shape[0]
    ff = wg.shape[1]
    emit_norm = g2 is not None
    row = lambda i, f: (i, 0)
    out_mode = dict(pipeline_mode=pl.Buffered(1))
    in_specs = [_ANY] * len(parts) + [
        pl.BlockSpec((1, d), lambda i, f: (0, 0)),
        pl.BlockSpec((d, tf), lambda i, f: (0, f)),
        pl.BlockSpec((d, tf), lambda i, f: (0, f)),
        pl.BlockSpec((tf, d), lambda i, f: (f, 0)),
    ]
    args = [xp for xp, _ in parts] + [g.reshape(1, d), wg, wu, wd]
    out_shape = [jax.ShapeDtypeStruct((m_total, d), f32)]
    out_specs = [pl.BlockSpec((tm, d), row, **out_mode)]
    if emit_norm:
        in_specs.append(pl.BlockSpec((1, d), lambda i, f: (0, 0)))
        args.append(g2.reshape(1, d))
        out_shape.append(jax.ShapeDtypeStruct((m_total, d), bf16))
        out_specs.append(pl.BlockSpec((tm, d), row, **out_mode))
    outs = pl.pallas_call(
        functools.partial(_ffn_kernel, group_rows=tuple((r0, xp.shape[0]) for xp, r0 in parts),
                          n_tiles=m_total // tm, emit_norm=emit_norm),
        grid=(m_total // tm, ff // tf),
        in_specs=in_specs,
        out_specs=out_specs,
        out_shape=out_shape,
        scratch_shapes=[pltpu.VMEM((tm, d), bf16)],
        compiler_params=_cparams(("arbitrary", "arbitrary")),
        name="ffn",
    )(*args)
    return outs if emit_norm else outs[0]


def _mm_kernel(a_ref, b_ref, *rest, epilogue):
    *extra, o_ref, bw_scr = rest
    j = pl.program_id(1)

    @pl.when(pl.program_id(0) == 0)
    def _():
        bw_scr[j] = b_ref[...].astype(bf16)

    o_ref[...] = epilogue(_dot(a_ref[...], bw_scr[j]), *extra).astype(o_ref.dtype)


def _mm(a, b, *, col0, n_cols, out_dtype, epilogue, extras=(), tm=ROW_TILE, tn=1024, name="mm"):
    m, k = a.shape
    off = col0 // tn
    nj = n_cols // tn
    in_specs = [
        pl.BlockSpec((tm, k), lambda i, j: (i, 0)),
        pl.BlockSpec((k, tn), lambda i, j: (0, jnp.where(i == 0, j, nj - 1) + off)),
    ]
    args = [a, b]
    for arr, blk, imap in extras:
        in_specs.append(pl.BlockSpec(blk, lambda i, j, imap=imap: imap(j, i)))
        args.append(arr)
    return pl.pallas_call(
        functools.partial(_mm_kernel, epilogue=epilogue),
        grid=(m // tm, nj),
        in_specs=in_specs,
        out_specs=pl.BlockSpec((tm, tn), lambda i, j: (i, j)),
        out_shape=jax.ShapeDtypeStruct((m, n_cols), out_dtype),
        scratch_shapes=[pltpu.VMEM((nj, k, tn), bf16)],
        compiler_params=_cparams(("arbitrary", "arbitrary")),
        name=name,
    )(*args)


def _gelu_tanh(x):
    return 0.5 * x * (1.0 + jnp.tanh(0.7978845608028654 * (x + 0.044715 * (x * x * x))))


def _rope_epilogue(acc, bias_ref, cos_ref, sin_ref, *, k_tile0, k_scale, head_dim):
    z = acc + bias_ref[...]
    cos = cos_ref[...]
    sin = sin_ref[...]
    half = head_dim // 2
    parts = []
    for h0 in range(0, z.shape[1], head_dim):
        x1 = z[:, h0:h0 + half]
        x2 = z[:, h0 + half:h0 + head_dim]
        parts += [x1 * cos - x2 * sin, x2 * cos + x1 * sin]
    scale = jnp.where(pl.program_id(1) >= k_tile0, k_scale, 1.0).astype(f32)
    return jnp.concatenate(parts, axis=1) * scale


def _lru_gates(xc, n, cs, wa_ref, wx_ref, ba_ref, bx_ref, lam_ref):
    xcb = xc.astype(bf16)
    r = _sigmoid(_dot(xcb, wa_ref[n]) + ba_ref[:, cs])
    gi = _sigmoid(_dot(xcb, wx_ref[n]) + bx_ref[:, cs])
    log_a = -LRU_C * r * jax.nn.softplus(-lam_ref[:, cs])
    a = jnp.exp(log_a)
    a2 = a * a
    one_minus = jnp.where(log_a < -0.25, 1.0 - a2, -jnp.tanh(log_a) * (a2 + 1.0))
    return a, one_minus * lax.rsqrt(jnp.maximum(one_minus, 1e-36)), gi


def _lru_prompt_kernel(*refs, n_seq):
    oa_ref = refs[9]

    @pl.when(pl.program_id(0) >= n_seq)
    def _():
        oa_ref[...] = jnp.zeros_like(oa_ref)

    @pl.when(pl.program_id(0) < n_seq)
    def _():
        _lru_prompt_body(*refs)


def _lru_prompt_body(xa_ref, gg_ref, cw_ref, cb_ref, wa_ref, wx_ref, ba_ref, bx_ref, lam_ref,
                     oa_ref, hl_ref, cn_ref, xe_scr, a_scr, b_scr, h_scr, hc_scr):
    tt, c = xa_ref.shape
    blk = c // N_LRU_BLOCKS
    hist = 8
    t = pl.program_id(1)

    @pl.when(t == 0)
    def _():
        xe_scr[0:hist, :] = jnp.zeros((hist, c), f32)
        hc_scr[...] = jnp.zeros_like(hc_scr)

    xe_scr[hist:hist + tt, :] = xa_ref[...]
    pos = lax.broadcasted_iota(jnp.int32, (tt, 1), 0) + t * tt
    for n in range(N_LRU_BLOCKS):
        cs = slice(n * blk, (n + 1) * blk)
        ext = xe_scr[:, cs]
        xc = cb_ref[:, cs] + cw_ref[CONV_W - 1:CONV_W, cs] * ext[hist:, :]
        for s in range(1, CONV_W):
            xc = xc + cw_ref[CONV_W - 1 - s:CONV_W - s, cs] * pltpu.roll(ext, s, axis=0)[hist:, :]
        a, mult, gi = _lru_gates(xc, n, cs, wa_ref, wx_ref, ba_ref, bx_ref, lam_ref)
        mult = jnp.where(pos == 0, 1.0, mult)
        a_scr[:, cs] = a
        b_scr[:, cs] = mult * (gi * xc)

    def step(i, h):
        h = a_scr[pl.ds(i, 1), :] * h + b_scr[pl.ds(i, 1), :]
        h_scr[pl.ds(i, 1), :] = h
        return h

    h = lax.fori_loop(0, tt, step, hc_scr[...], unroll=8)
    hc_scr[...] = h
    hl_ref[0] = h
    for n in range(N_LRU_BLOCKS):
        cs = slice(n * blk, (n + 1) * blk)
        oa_ref[:, cs] = (h_scr[:, cs] * gg_ref[:, cs].astype(f32)).astype(bf16)
    cn_ref[0] = xe_scr[pl.ds(hist + tt - (CONV_W - 1), CONV_W - 1), :]
    xe_scr[0:hist, :] = xe_scr[pl.ds(tt, hist), :]


def _lru_sample_kernel(xa_ref, gg_ref, scp_ref, h0_ref, cw_ref, cb_ref, wa_ref, wx_ref, ba_ref, bx_ref, lam_ref,
                       oa_prev_ref, oa_ref, hl_ref, cn_ref):
    del oa_prev_ref
    rows, c = xa_ref.shape
    bt, t_len = scp_ref.shape[0], scp_ref.shape[1]
    blk = c // N_LRU_BLOCKS
    tpos = lax.broadcasted_iota(jnp.int32, (bt, t_len, blk), 1)
    for n in range(N_LRU_BLOCKS):
        cs = slice(n * blk, (n + 1) * blk)
        x3 = xa_ref[:, cs].reshape(bt, t_len, blk)
        ext = jnp.concatenate([scp_ref[:, :, cs], x3], axis=1)
        xc3 = cb_ref[:, cs] + cw_ref[CONV_W - 1:CONV_W, cs] * x3
        for s in range(1, CONV_W):
            xc3 = xc3 + cw_ref[CONV_W - 1 - s:CONV_W - s, cs] * pltpu.roll(ext, s, axis=1)[:, t_len:, :]
        xc = xc3.reshape(rows, blk)
        a, mult, gi = _lru_gates(xc, n, cs, wa_ref, wx_ref, ba_ref, bx_ref, lam_ref)
        a3 = a.reshape(bt, t_len, blk)
        b3 = (mult * (gi * xc)).reshape(bt, t_len, blk)
        d = 1
        while d < t_len:
            keep = tpos >= d
            b3 = jnp.where(keep, a3 * pltpu.roll(b3, d, axis=1) + b3, b3)
            a3 = jnp.where(keep, a3 * pltpu.roll(a3, d, axis=1), a3)
            d *= 2
        h3 = b3 + a3 * h0_ref[:, :, cs]
        oa_ref[:, cs] = (h3.reshape(rows, blk) * gg_ref[:, cs].astype(f32)).astype(bf16)
        hl_ref[:, :, cs] = h3[:, t_len - 1:t_len, :]
        cn_ref[:, :, cs] = pltpu.roll(x3, CONV_W - 1, axis=1)[:, 0:CONV_W - 1, :]


def _ret_norm_gate(o, rn, sg):
    return (o * lax.rsqrt(jnp.mean(o * o, axis=-1, keepdims=True) + EPS) * rn * sg.astype(f32)).astype(bf16)


def _ret_prompt_kernel(*refs, n_seq):
    ob_ref = refs[9]

    @pl.when(pl.program_id(0) >= n_seq)
    def _():
        ob_ref[...] = jnp.zeros_like(ob_ref)

    @pl.when(pl.program_id(0) < n_seq)
    def _():
        _ret_prompt_body(*refs)


def _ret_prompt_body(q_ref, k_ref, v_ref, sg_ref, rn_ref, dm_ref, cd_ref, sd_ref, chd_ref, ob_ref, so_ref):
    c = pl.program_id(1)
    dk, dv = so_ref.shape[2], so_ref.shape[3]

    @pl.when(c == 0)
    def _():
        so_ref[...] = jnp.zeros_like(so_ref)

    for h in range(N_RET_HEADS):
        ks = slice(h * dk, (h + 1) * dk)
        vs = slice(h * dv, (h + 1) * dv)
        q = q_ref[:, ks]
        k = k_ref[:, ks]
        v = v_ref[:, vs]
        s_old = so_ref[0, h]
        scores = _dot_nt(q, k) * dm_ref[h]
        inner = _dot(scores.astype(bf16), v)
        cross = _dot(q, s_old.astype(bf16)) * cd_ref[h]
        kd = (k.astype(f32) * sd_ref[h]).astype(bf16)
        so_ref[0, h] = s_old * chd_ref[h] + _dot_tn(kd, v)
        ob_ref[:, vs] = _ret_norm_gate(inner + cross, rn_ref[h], sg_ref[:, vs])


def _ret_sample_kernel(q_ref, k_ref, v_ref, sg_ref, s_ref, rn_ref, dm_ref, cd_ref, sd_ref, chd_ref, ob_prev_ref,
                       ob_ref, so_ref, *, t_len):
    del ob_prev_ref
    rows = q_ref.shape[0]
    dk, dv = so_ref.shape[2], so_ref.shape[3]
    row = lax.broadcasted_iota(jnp.int32, (rows, 1), 0)
    for h in range(N_RET_HEADS):
        ks = slice(h * dk, (h + 1) * dk)
        vs = slice(h * dv, (h + 1) * dv)
        q = q_ref[:, ks]
        k = k_ref[:, ks]
        v = v_ref[:, vs]
        scores = _dot_nt(q, k) * dm_ref[h]
        o = _dot(scores.astype(bf16), v)
        kd = k.astype(f32) * sd_ref[h]
        for bb in range(rows // t_len):
            mine = (row >= bb * t_len) & (row < (bb + 1) * t_len)
            s_old = s_ref[bb, h]
            cross = _dot(q, s_old.astype(bf16)) * cd_ref[h]
            o = o + jnp.where(mine, cross, 0.0)
            so_ref[bb, h] = s_old * chd_ref[h] + _dot_tn(jnp.where(mine, kd, 0.0).astype(bf16), v)
        ob_ref[:, vs] = _ret_norm_gate(o, rn_ref[h], sg_ref[:, vs])


def _decay_tables(chunk, n_seq):
    log_g = jnp.log1p(-jnp.exp2(-5.0 - jnp.arange(N_RET_HEADS, dtype=f32)))
    r = jnp.arange(chunk * n_seq)
    idx = (r % chunk).astype(f32)
    same = (r[:, None] // chunk) == (r[None, :] // chunk)
    diff = idx[:, None] - idx[None, :]
    dmask = jnp.where(same & (diff >= 0), jnp.exp(jnp.maximum(diff, 0.0)[None] * log_g[:, None, None]), 0.0)
    cross_decay = jnp.exp((idx[None] + 1.0) * log_g[:, None])[..., None]
    state_decay = jnp.exp((chunk - 1.0 - idx[None]) * log_g[:, None])[..., None]
    chunk_decay = jnp.exp(chunk * log_g)[:, None, None]
    return dmask, cross_decay, state_decay, chunk_decay


def _ple_kernel(x_ref, pep_ref, pes_ref, gn_ref, wg_ref, bg_ref, pp_ref, fn_ref, yp_ref, ys_ref, *, n_prompt_tiles, final):
    i = pl.program_id(0)
    x = x_ref[...]
    u = _rms(x, gn_ref[...]).astype(bf16)
    gate = _sigmoid(_dot(u, wg_ref[...]) + bg_ref[...])
    pe = jnp.where(i < n_prompt_tiles, pep_ref[...], pes_ref[...]).astype(bf16)
    x = x + gate * _dot(pe, pp_ref[...])
    y = _rms(x, fn_ref[...]) if final else x

    @pl.when(i < n_prompt_tiles)
    def _():
        yp_ref[...] = y

    @pl.when(i >= n_prompt_tiles)
    def _():
        ys_ref[...] = y


def _layer(x_parts, pe_p, pe_s, n_prompt, seq, n_sample, dec_seq, h0_s, conv_s, ret_s, cos_t, sin_t, lp, final_norm):
    d = lp['mix_norm'].shape[0]
    mp, ms = n_prompt * seq, n_sample * dec_seq
    m = mp + ms
    d_rnn = lp['conv_w'].shape[1]
    hh = N_RET_HEADS
    dk = d // hh
    dv = lp['ret_norm'].shape[1]
    qk_dim, v_dim = hh * dk, hh * dv
    tm, tn = ROW_TILE, 1024

    x1, u = _ffn(x_parts, m, lp['ffn1_norm'], lp['ffn1_wg'], lp['ffn1_wu'], lp['ffn1_wd'], lp['mix_norm'],
                 tm=_ffn_row_tile(m, d, with_norm=True))

    w_in = lp['w_in']
    b_in = lp['b_in'].reshape(1, -1)

    def bias_extra(col0):
        return (b_in, (1, tn), lambda j, i, o=col0 // tn: (0, j + o))

    def with_bias(fn):
        return lambda acc, b_ref: fn(acc + b_ref[...])

    c_xa, c_ga, c_q, c_v = 0, d_rnn, 2 * d_rnn, 2 * d_rnn + 2 * qk_dim
    c_gr, c_gate = c_v + v_dim, c_v + 2 * v_dim
    proj = functools.partial(_mm, u, w_in, tm=tm, tn=tn)
    xa = proj(col0=c_xa, n_cols=d_rnn, out_dtype=f32, epilogue=with_bias(lambda z: z),
              extras=[bias_extra(c_xa)], name="inproj_xa")
    gg = proj(col0=c_ga, n_cols=d_rnn, out_dtype=bf16, epilogue=with_bias(_gelu_tanh),
              extras=[bias_extra(c_ga)], name="inproj_ga")
    n_pt = mp // tm
    half = dk // 2
    rope_map = lambda j, i: (jnp.where(i < n_pt, i % (seq // tm), seq // tm + i - n_pt), 0)
    qk = proj(col0=c_q, n_cols=2 * qk_dim, out_dtype=bf16,
              epilogue=functools.partial(_rope_epilogue, k_tile0=qk_dim // tn, k_scale=dk ** -0.5, head_dim=dk),
              extras=[bias_extra(c_q), (cos_t, (tm, half), rope_map), (sin_t, (tm, half), rope_map)],
              name="inproj_qk")
    v = proj(col0=c_v, n_cols=v_dim, out_dtype=bf16, epilogue=with_bias(lambda z: z),
             extras=[bias_extra(c_v)], name="inproj_v")
    sg = proj(col0=c_gr, n_cols=v_dim, out_dtype=bf16, epilogue=with_bias(lambda z: z * _sigmoid(z)),
              extras=[bias_extra(c_gr)], name="inproj_gr")
    gates = proj(col0=c_gate, n_cols=2 * d, out_dtype=bf16, epilogue=with_bias(_sigmoid),
                 extras=[bias_extra(c_gate)], name="inproj_gates")

    row2 = lambda a: a.reshape(1, -1)
    lru_w = [lp['conv_w'], row2(lp['conv_b']), _to_bf16(lp['lru_wa']), _to_bf16(lp['lru_wx']), row2(lp['lru_ba']),
             row2(lp['lru_bx']), row2(lp['lru_lambda'])]
    nb = N_LRU_BLOCKS
    blk = d_rnn // nb
    z2 = lambda *_: (0, 0)
    z3 = lambda *_: (0, 0, 0)
    lru_w_specs = [pl.BlockSpec((CONV_W, d_rnn), z2), pl.BlockSpec((1, d_rnn), z2),
                   pl.BlockSpec((nb, blk, blk), z3), pl.BlockSpec((nb, blk, blk), z3),
                   pl.BlockSpec((1, d_rnn), z2), pl.BlockSpec((1, d_rnn), z2), pl.BlockSpec((1, d_rnn), z2)]

    tt = 256
    ntt = seq // tt
    assert ms % tt == 0 and ms // tt <= ntt
    seq_rows = lambda b, t: (jnp.minimum(b * ntt + t, mp // tt - 1), 0)
    all_rows = lambda b, t: (jnp.minimum(b * ntt + t, m // tt - 1), 0)
    per_seq = lambda b, t: (jnp.minimum(b, n_prompt - 1), 0, 0)
    oa, hl_p, cn_p = pl.pallas_call(
        functools.partial(_lru_prompt_kernel, n_seq=n_prompt),
        grid=(n_prompt + 1, ntt),
        in_specs=[pl.BlockSpec((tt, d_rnn), seq_rows), pl.BlockSpec((tt, d_rnn), seq_rows)] + lru_w_specs,
        out_specs=[pl.BlockSpec((tt, d_rnn), all_rows), pl.BlockSpec((1, 1, d_rnn), per_seq),
                   pl.BlockSpec((1, CONV_W - 1, d_rnn), per_seq)],
        out_shape=[jax.ShapeDtypeStruct((m, d_rnn), bf16), jax.ShapeDtypeStruct((n_prompt, 1, d_rnn), f32),
                   jax.ShapeDtypeStruct((n_prompt, CONV_W - 1, d_rnn), f32)],
        scratch_shapes=[pltpu.VMEM((tt + 8, d_rnn), f32), pltpu.VMEM((tt, d_rnn), f32), pltpu.VMEM((tt, d_rnn), f32),
                        pltpu.VMEM((tt, d_rnn), f32), pltpu.VMEM((1, d_rnn), f32)],
        compiler_params=_cparams(("arbitrary", "arbitrary")),
        name="lru_prompt",
    )(xa, gg, *lru_w)

    bt = 32
    rows_s = bt * dec_seq
    s_off = mp // rows_s
    scp = jnp.pad(conv_s, ((0, 0), (dec_seq - (CONV_W - 1), 0), (0, 0)))
    samp_rows = lambda i: (s_off + i, 0)
    samp3 = lambda i: (i, 0, 0)
    oa, hl_s, cn_s = pl.pallas_call(
        _lru_sample_kernel,
        grid=(n_sample // bt,),
        in_specs=[pl.BlockSpec((rows_s, d_rnn), samp_rows), pl.BlockSpec((rows_s, d_rnn), samp_rows),
                  pl.BlockSpec((bt, dec_seq, d_rnn), samp3), pl.BlockSpec((bt, 1, d_rnn), samp3)] + lru_w_specs + [_ANY],
        out_specs=[pl.BlockSpec((rows_s, d_rnn), samp_rows), pl.BlockSpec((bt, 1, d_rnn), samp3),
                   pl.BlockSpec((bt, CONV_W - 1, d_rnn), samp3)],
        out_shape=[jax.ShapeDtypeStruct((m, d_rnn), bf16), jax.ShapeDtypeStruct((n_sample, 1, d_rnn), f32),
                   jax.ShapeDtypeStruct((n_sample, CONV_W - 1, d_rnn), f32)],
        input_output_aliases={11: 0},
        compiler_params=_cparams(("parallel",)),
        name="lru_sample",
    )(xa, gg, scp, h0_s.reshape(n_sample, 1, d_rnn), *lru_w, oa)

    rn = lp['ret_norm'].reshape(hh, 1, dv)
    ct = RET_CHUNK
    nct = seq // ct
    tabs = _decay_tables(ct, 1)
    full3 = lambda shape: pl.BlockSpec(shape, lambda *_: (0, 0, 0))
    tab_specs = lambda r: [full3((hh, 1, dv)), full3((hh, r, r)), full3((hh, r, 1)), full3((hh, r, 1)), full3((hh, 1, 1))]
    assert ms % ct == 0 and ms // ct <= nct
    chunk_row = lambda b, c: jnp.minimum(b * nct + c, mp // ct - 1)
    ob, so_p = pl.pallas_call(
        functools.partial(_ret_prompt_kernel, n_seq=n_prompt),
        grid=(n_prompt + 1, nct),
        in_specs=[pl.BlockSpec((ct, qk_dim), lambda b, c: (chunk_row(b, c), 0)),
                  pl.BlockSpec((ct, qk_dim), lambda b, c: (chunk_row(b, c), 1)),
                  pl.BlockSpec((ct, v_dim), lambda b, c: (chunk_row(b, c), 0)),
                  pl.BlockSpec((ct, v_dim), lambda b, c: (chunk_row(b, c), 0))] + tab_specs(ct),
        out_specs=[pl.BlockSpec((ct, v_dim), lambda b, c: (jnp.minimum(b * nct + c, m // ct - 1), 0)),
                   pl.BlockSpec((1, hh, dk, dv), lambda b, c: (jnp.minimum(b, n_prompt - 1), 0, 0, 0))],
        out_shape=[jax.ShapeDtypeStruct((m, v_dim), bf16), jax.ShapeDtypeStruct((n_prompt, hh, dk, dv), f32)],
        compiler_params=_cparams(("arbitrary", "arbitrary")),
        name="ret_prompt",
    )(qk, qk, v, sg, rn, *tabs)

    pr = SAMPLE_PAIR
    rows_r = pr * dec_seq
    r_off = mp // rows_r
    tabs = _decay_tables(dec_seq, pr)
    pair_rows = lambda p: (r_off + p, 0)
    pair_state = lambda p: (p, 0, 0, 0)
    ob, so_s = pl.pallas_call(
        functools.partial(_ret_sample_kernel, t_len=dec_seq),
        grid=(n_sample // pr,),
        in_specs=[pl.BlockSpec((rows_r, qk_dim), pair_rows), pl.BlockSpec((rows_r, qk_dim), lambda p: (r_off + p, 1)),
                  pl.BlockSpec((rows_r, v_dim), pair_rows), pl.BlockSpec((rows_r, v_dim), pair_rows),
                  pl.BlockSpec((pr, hh, dk, dv), pair_state)] + tab_specs(rows_r) + [_ANY],
        out_specs=[pl.BlockSpec((rows_r, v_dim), pair_rows), pl.BlockSpec((pr, hh, dk, dv), pair_state)],
        out_shape=[jax.ShapeDtypeStruct((m, v_dim), bf16), jax.ShapeDtypeStruct((n_sample, hh, dk, dv), f32)],
        input_output_aliases={10: 0},
        compiler_params=_cparams(("parallel",)),
        name="ret_sample",
    )(qk, qk, v, sg, ret_s, rn, *tabs, ob)

    tile = lambda j, i: (i, j)
    pa = _mm(oa, lp['proj_a'], col0=0, n_cols=d, out_dtype=f32, epilogue=lambda acc: acc, name="proj_a")
    tnb = 256
    merged = _mm(ob, lp['proj_b'], col0=0, n_cols=d, out_dtype=bf16, tn=tnb,
                 epilogue=lambda acc, pa_ref, ga_ref, gb_ref: (ga_ref[...].astype(f32) * pa_ref[...]
                                                               + gb_ref[...].astype(f32) * acc),
                 extras=[(pa, (tm, tnb), tile), (gates, (tm, tnb), tile),
                         (gates, (tm, tnb), lambda j, i: (i, j + d // tnb))], name="proj_b_merge")
    x2 = _mm(merged, lp['w_out'], col0=0, n_cols=d, out_dtype=f32,
             epilogue=lambda acc, x_ref: x_ref[...] + acc, extras=[(x1, (tm, tn), tile)], name="w_out")

    x3 = _ffn([(x2, 0)], m, lp['ffn2_norm'], lp['ffn2_wg'], lp['ffn2_wu'], lp['ffn2_wd'],
              tm=_ffn_row_tile(m, d, with_norm=False))

    tp = 512
    ple_dim = pe_p.shape[1]
    fin = final_norm is not None
    fn = (final_norm if fin else lp['ple_norm']).reshape(1, d)
    npt = mp // tp
    c2 = lambda i: (0, 0)
    prow = lambda i: (jnp.minimum(i, npt - 1), 0)
    srow = lambda i: (jnp.maximum(i - npt, 0), 0)
    y_p, y_s = pl.pallas_call(
        functools.partial(_ple_kernel, n_prompt_tiles=npt, final=fin),
        grid=(m // tp,),
        in_specs=[pl.BlockSpec((tp, d), lambda i: (i, 0)), pl.BlockSpec((tp, ple_dim), prow),
                  pl.BlockSpec((tp, ple_dim), srow), pl.BlockSpec((1, d), c2), pl.BlockSpec((d, d), c2),
                  pl.BlockSpec((1, d), c2), pl.BlockSpec((ple_dim, d), c2), pl.BlockSpec((1, d), c2)],
        out_specs=[pl.BlockSpec((tp, d), prow), pl.BlockSpec((tp, d), srow)],
        out_shape=[jax.ShapeDtypeStruct((mp, d), f32), jax.ShapeDtypeStruct((ms, d), f32)],
        compiler_params=_cparams(("arbitrary",)),
        name="ple",
    )(x3, pe_p, pe_s, lp['ple_norm'].reshape(1, d), _to_bf16(lp['ple_wg']), lp['ple_bg'].reshape(1, d),
      _to_bf16(lp['ple_proj']), fn)

    states = (hl_p.reshape(n_prompt, d_rnn), cn_p, so_p, hl_s.reshape(n_sample, d_rnn), cn_s, so_s)
    return y_p, y_s, states


def kernel(x_prompt, x_sample, p_prompt, p_sample, state_lru, state_conv, state_ret, ffn1_norm, ffn1_wg, ffn1_wu, ffn1_wd, mix_norm, w_in, b_in, conv_w, conv_b, lru_wa, lru_ba, lru_wx, lru_bx, lru_lambda, ret_norm, proj_a, proj_b, w_out, ffn2_norm, ffn2_wg, ffn2_wu, ffn2_wd, ple_norm, ple_wg, ple_bg, ple_proj, final_norm):
    params = dict(ffn1_norm=ffn1_norm, ffn1_wg=ffn1_wg, ffn1_wu=ffn1_wu, ffn1_wd=ffn1_wd, mix_norm=mix_norm,
                  w_in=w_in, b_in=b_in, conv_w=conv_w, conv_b=conv_b, lru_wa=lru_wa, lru_ba=lru_ba, lru_wx=lru_wx,
                  lru_bx=lru_bx, lru_lambda=lru_lambda, ret_norm=ret_norm, proj_a=proj_a, proj_b=proj_b,
                  w_out=w_out, ffn2_norm=ffn2_norm, ffn2_wg=ffn2_wg, ffn2_wu=ffn2_wu, ffn2_wd=ffn2_wd,
                  ple_norm=ple_norm, ple_wg=ple_wg, ple_bg=ple_bg, ple_proj=ple_proj)
    depth = w_in.shape[0]
    n_prompt, seq, d = x_prompt.shape
    n_sample, dec_seq, _ = x_sample.shape
    mp, ms = n_prompt * seq, n_sample * dec_seq
    dk = d // N_RET_HEADS

    y_p = x_prompt.astype(f32).reshape(mp, d)
    y_s = x_sample.astype(f32).reshape(ms, d)
    cos_t, sin_t = _rope_table(seq, dec_seq, ms, dk // 2)
    outs = [[] for _ in range(6)]
    for i in range(depth):
        lp = {k: v[i].astype(f32) for k, v in params.items()}
        y_p, y_s, st = _layer([(y_p, 0), (y_s, mp)], p_prompt[i].astype(f32).reshape(mp, -1),
                              p_sample[i].astype(f32).reshape(ms, -1), n_prompt, seq, n_sample, dec_seq,
                              state_lru[i].astype(f32), state_conv[i].astype(f32), state_ret[i].astype(f32),
                              cos_t, sin_t, lp, final_norm.astype(f32) if i == depth - 1 else None)
        for o, s in zip(outs, st):
            o.append(s)
    y_prompt = y_p.reshape(n_prompt, seq, d).astype(x_prompt.dtype)
    y_sample = y_s.reshape(n_sample, dec_seq, d).astype(x_sample.dtype)
    lru_p, conv_p, ret_p, lru_s, conv_s, ret_s = (jnp.stack(o) for o in outs)
    return (y_prompt, y_sample, lru_p.astype(state_lru.dtype), conv_p.astype(state_conv.dtype),
            ret_p.astype(state_ret.dtype), lru_s.astype(state_lru.dtype), conv_s.astype(state_conv.dtype),
            ret_s.astype(state_ret.dtype))
```

```python
import functools

import jax
import jax.numpy as jnp
from jax import lax
from jax.experimental import pallas as pl
from jax.experimental.pallas import tpu as pltpu

f32 = jnp.float32
bf16 = jnp.bfloat16

N_LRU_BLOCKS = 8
CONV_W = 4
LRU_C = 8.0
N_RET_HEADS = 8
ROPE_BASE = 10000.0
EPS = 1e-6
PAST_LEN = 16384

RET_CHUNK = 256
SAMPLE_PAIR = 2
ROW_TILE = 1024
IN_ROW_TILE = 1536
VMEM_LIMIT = 58 * 1024 * 1024


def _cparams(sem):
    return pltpu.CompilerParams(dimension_semantics=sem, vmem_limit_bytes=VMEM_LIMIT)


def _sigmoid(x):
    return 1.0 / (1.0 + jnp.exp(-x))


def _rms(x, g):
    return x * lax.rsqrt(jnp.mean(x * x, axis=-1, keepdims=True) + EPS) * g


def _dot(a, b):
    return jnp.dot(a, b, preferred_element_type=f32)


def _dot_nt(a, b):
    return lax.dot_general(a, b, (((1,), (1,)), ((), ())), preferred_element_type=f32)


def _dot_tn(a, b):
    return lax.dot_general(a, b, (((0,), (0,)), ((), ())), preferred_element_type=f32)


_ANY = pl.BlockSpec(memory_space=pl.ANY)


def _cast_kernel(x_ref, o_ref):
    o_ref[...] = x_ref[...].astype(o_ref.dtype)


def _to_bf16(w):
    w2 = w.reshape(-1, w.shape[-1])
    return pl.pallas_call(_cast_kernel, out_shape=jax.ShapeDtypeStruct(w2.shape, bf16), name="to_bf16")(w2).reshape(w.shape)


def _rope_table_kernel(inv_ref, cos_ref, sin_ref, *, seq, dec_seq):
    rows = cos_ref.shape[0]
    r = lax.broadcasted_iota(jnp.int32, (rows, inv_ref.shape[1]), 0)
    pos = jnp.where(r < seq, r, PAST_LEN + lax.rem(r - seq, dec_seq))
    ang = pos.astype(f32) * inv_ref[...]
    cos_ref[...] = jnp.cos(ang)
    sin_ref[...] = jnp.sin(ang)


def _rope_table(seq, dec_seq, n_sample_rows, half):
    inv = (ROPE_BASE ** (-jnp.arange(half, dtype=f32) / half)).reshape(1, half)
    rows = seq + n_sample_rows
    return pl.pallas_call(
        functools.partial(_rope_table_kernel, seq=seq, dec_seq=dec_seq),
        out_shape=(jax.ShapeDtypeStruct((rows, half), f32), jax.ShapeDtypeStruct((rows, half), f32)),
        name="rope_table",
    )(inv)


def _ffn_kernel(*refs, group_rows, n_tiles, emit_norm):
    n_groups = len(group_rows)
    x_hbm = refs[:n_groups]
    g_ref, wg_ref, wu_ref, wd_ref = refs[n_groups:n_groups + 4]
    rest = list(refs[n_groups + 4:])
    g2_ref = rest.pop(0) if emit_norm else None
    xo_ref = rest.pop(0)
    u2_ref = rest.pop(0) if emit_norm else None
    (u_scr,) = rest
    i = pl.program_id(0)
    f = pl.program_id(1)
    tm = xo_ref.shape[0]

    @pl.when(f == 0)
    def _():
        for t in range(n_tiles):
            @pl.when(i == t)
            def _(t=t):
                for gi, (r0, nr) in enumerate(group_rows):
                    lo, hi = max(t * tm, r0), min((t + 1) * tm, r0 + nr)
                    if lo < hi:
                        pltpu.sync_copy(x_hbm[gi].at[lo - r0:hi - r0, :], xo_ref.at[lo - t * tm:hi - t * tm, :])

        u_scr[...] = _rms(xo_ref[...], g_ref[...]).astype(bf16)

    u = u_scr[...]
    hs = []
    for c0 in range(0, wg_ref.shape[1], FFN_SUB_TILE):
        cols = slice(c0, c0 + FFN_SUB_TILE)
        hg = _dot(u, wg_ref[:, cols].astype(bf16))
        hu = _dot(u, wu_ref[:, cols].astype(bf16))
        hs.append((0.5 * hg * _sigmoid(hg) * hu).astype(bf16))
    h = hs[0] if len(hs) == 1 else jnp.concatenate(hs, axis=1)
    xo_ref[...] += _dot(h, wd_ref[...].astype(bf16))

    if emit_norm:
        @pl.when(f == pl.num_programs(1) - 1)
        def _():
            u2_ref[...] = _rms(xo_ref[...], g2_ref[...]).astype(bf16)


FFN_SUB_TILE = 256
FFN_VMEM_BUDGET = 60 * 1024 * 1024


def _ffn_tiles(m, d, ff, with_norm):
    sub = FFN_SUB_TILE
    for tf in (4 * sub, 2 * sub, sub):
        if ff % tf:
            continue
        for tm in range(m, 1023, -256):
            if m % tm == 0:
                rows = tm * d * (4 + 2 + (2 if with_norm else 0))
                weights = 3 * d * tf * 4 * 2
                temps = 2 * tm * sub * 4 + tm * tf * 2 + (2 * d * sub + tf * d) * 2
                if rows + weights + temps <= FFN_VMEM_BUDGET:
                    return tm, tf
    raise ValueError("no FFN tiling fits VMEM")


def _ffn(parts, m_total, g, wg, wu, wd, g2=None, *, tm, tf):
    d = wg.shape[0]
    ff = wg.shape[1]
    emit_norm = g2 is not None
    row = lambda i, f: (i, 0)
    out_mode = dict(pipeline_mode=pl.Buffered(1))
    in_specs = [_ANY] * len(parts) + [
        pl.BlockSpec((1, d), lambda i, f: (0, 0)),
        pl.BlockSpec((d, tf), lambda i, f: (0, f)),
        pl.BlockSpec((d, tf), lambda i, f: (0, f)),
        pl.BlockSpec((tf, d), lambda i, f: (f, 0)),
    ]
    args = [xp for xp, _ in parts] + [g.reshape(1, d), wg, wu, wd]
    out_shape = [jax.ShapeDtypeStruct((m_total, d), f32)]
    out_specs = [pl.BlockSpec((tm, d), row, **out_mode)]
    if emit_norm:
        in_specs.append(pl.BlockSpec((1, d), lambda i, f: (0, 0)))
        args.append(g2.reshape(1, d))
        out_shape.append(jax.ShapeDtypeStruct((m_total, d), bf16))
        out_specs.append(pl.BlockSpec((tm, d), row, **out_mode))
    outs = pl.pallas_call(
        functools.partial(_ffn_kernel, group_rows=tuple((r0, xp.shape[0]) for xp, r0 in parts),
                          n_tiles=m_total // tm, emit_norm=emit_norm),
        grid=(m_total // tm, ff // tf),
        in_specs=in_specs,
        out_specs=out_specs,
        out_shape=out_shape,
        scratch_shapes=[pltpu.VMEM((tm, d), bf16)],
        compiler_params=_cparams(("arbitrary", "arbitrary")),
        name="ffn",
    )(*args)
    return outs if emit_norm else outs[0]


MM_ROW_SPLIT = 2


def _mm_kernel(a_ref, b_ref, *rest, epilogue, col_axis):
    *extra, o_ref, bw_scr = rest
    j = pl.program_id(col_axis)
    slot = j if col_axis == 1 else 0

    @pl.when(pl.program_id(1 - col_axis) == 0)
    def _():
        bw_scr[slot] = b_ref[...].astype(bf16)

    rows = a_ref.shape[0] // MM_ROW_SPLIT
    for r in range(MM_ROW_SPLIT):
        rs = slice(r * rows, (r + 1) * rows)
        o_ref[rs, :] = epilogue(_dot(a_ref[rs, :], bw_scr[slot]), *extra, rows=rs, col_tile=j).astype(o_ref.dtype)


def _mm(a, b, *, col0, n_cols, out_dtype, epilogue, extras=(), tm=ROW_TILE, tn=1024, resident=True, name="mm"):
    m, k = a.shape
    off = col0 // tn
    nj = n_cols // tn
    if resident:
        grid, order = (m // tm, nj), (lambda f: (lambda i, j: f(j, i)))
        b_map = lambda j, i: (0, jnp.where(i == 0, j, nj - 1) + off)
    else:
        grid, order = (nj, m // tm), (lambda f: f)
        b_map = lambda j, i: (0, j + off)
    in_specs = [pl.BlockSpec((tm, k), order(lambda j, i: (i, 0))), pl.BlockSpec((k, tn), order(b_map))]
    args = [a, b]
    for arr, blk, imap in extras:
        in_specs.append(pl.BlockSpec(blk, order(imap)))
        args.append(arr)
    return pl.pallas_call(
        functools.partial(_mm_kernel, epilogue=epilogue, col_axis=1 if resident else 0),
        grid=grid,
        in_specs=in_specs,
        out_specs=pl.BlockSpec((tm, tn), order(lambda j, i: (i, j))),
        out_shape=jax.ShapeDtypeStruct((m, n_cols), out_dtype),
        scratch_shapes=[pltpu.VMEM((nj if resident else 1, k, tn), bf16)],
        compiler_params=_cparams(("arbitrary", "arbitrary")),
        name=name,
    )(*args)


def _gelu_tanh(x):
    return 0.5 * x * (1.0 + jnp.tanh(0.7978845608028654 * (x + 0.044715 * (x * x * x))))


def _rope_epilogue(acc, bias_ref, cos_ref, sin_ref, *, rows, col_tile, k_tile0, k_scale, head_dim):
    z = acc + bias_ref[...]
    cos = cos_ref[rows, :]
    sin = sin_ref[rows, :]
    half = head_dim // 2
    parts = []
    for h0 in range(0, z.shape[1], head_dim):
        x1 = z[:, h0:h0 + half]
        x2 = z[:, h0 + half:h0 + head_dim]
        parts += [x1 * cos - x2 * sin, x2 * cos + x1 * sin]
    scale = jnp.where(col_tile >= k_tile0, k_scale, 1.0).astype(f32)
    return jnp.concatenate(parts, axis=1) * scale


def _lru_gates(xc, n, cs, wa_ref, wx_ref, ba_ref, bx_ref, lam_ref):
    xcb = xc.astype(bf16)
    r = _sigmoid(_dot(xcb, wa_ref[n]) + ba_ref[:, cs])
    gi = _sigmoid(_dot(xcb, wx_ref[n]) + bx_ref[:, cs])
    log_a = -LRU_C * r * jax.nn.softplus(-lam_ref[:, cs])
    a = jnp.exp(log_a)
    a2 = a * a
    one_minus = jnp.where(log_a < -0.25, 1.0 - a2, -jnp.tanh(log_a) * (a2 + 1.0))
    return a, one_minus * lax.rsqrt(jnp.maximum(one_minus, 1e-36)), gi


def _lru_prompt_kernel(*refs, n_seq):
    oa_ref = refs[9]

    @pl.when(pl.program_id(0) >= n_seq)
    def _():
        oa_ref[...] = jnp.zeros_like(oa_ref)

    @pl.when(pl.program_id(0) < n_seq)
    def _():
        _lru_prompt_body(*refs)


def _lru_prompt_body(xa_ref, gg_ref, cw_ref, cb_ref, wa_ref, wx_ref, ba_ref, bx_ref, lam_ref,
                     oa_ref, hl_ref, cn_ref, xe_scr, a_scr, b_scr, h_scr, hc_scr):
    tt, c = xa_ref.shape
    blk = c // N_LRU_BLOCKS
    hist = 8
    t = pl.program_id(1)

    @pl.when(t == 0)
    def _():
        xe_scr[0:hist, :] = jnp.zeros((hist, c), f32)
        hc_scr[...] = jnp.zeros_like(hc_scr)

    xe_scr[hist:hist + tt, :] = xa_ref[...]
    pos = lax.broadcasted_iota(jnp.int32, (tt, 1), 0) + t * tt
    for n in range(N_LRU_BLOCKS):
        cs = slice(n * blk, (n + 1) * blk)
        ext = xe_scr[:, cs]
        xc = cb_ref[:, cs] + cw_ref[CONV_W - 1:CONV_W, cs] * ext[hist:, :]
        for s in range(1, CONV_W):
            xc = xc + cw_ref[CONV_W - 1 - s:CONV_W - s, cs] * pltpu.roll(ext, s, axis=0)[hist:, :]
        a, mult, gi = _lru_gates(xc, n, cs, wa_ref, wx_ref, ba_ref, bx_ref, lam_ref)
        mult = jnp.where(pos == 0, 1.0, mult)
        a_scr[:, cs] = a
        b_scr[:, cs] = mult * (gi * xc)

    def step(i, h):
        h = a_scr[pl.ds(i, 1), :] * h + b_scr[pl.ds(i, 1), :]
        h_scr[pl.ds(i, 1), :] = h
        return h

    h = lax.fori_loop(0, tt, step, hc_scr[...], unroll=8)
    hc_scr[...] = h
    hl_ref[0] = h
    for n in range(N_LRU_BLOCKS):
        cs = slice(n * blk, (n + 1) * blk)
        oa_ref[:, cs] = (h_scr[:, cs] * gg_ref[:, cs].astype(f32)).astype(bf16)
    cn_ref[0] = xe_scr[pl.ds(hist + tt - (CONV_W - 1), CONV_W - 1), :]
    xe_scr[0:hist, :] = xe_scr[pl.ds(tt, hist), :]


def _lru_sample_kernel(xa_ref, gg_ref, scp_ref, h0_ref, cw_ref, cb_ref, wa_ref, wx_ref, ba_ref, bx_ref, lam_ref,
                       oa_prev_ref, oa_ref, hl_ref, cn_ref):
    del oa_prev_ref
    rows, c = xa_ref.shape
    bt, t_len = scp_ref.shape[0], scp_ref.shape[1]
    blk = c // N_LRU_BLOCKS
    tpos = lax.broadcasted_iota(jnp.int32, (bt, t_len, blk), 1)
    for n in range(N_LRU_BLOCKS):
        cs = slice(n * blk, (n + 1) * blk)
        x3 = xa_ref[:, cs].reshape(bt, t_len, blk)
        ext = jnp.concatenate([scp_ref[:, :, cs], x3], axis=1)
        xc3 = cb_ref[:, cs] + cw_ref[CONV_W - 1:CONV_W, cs] * x3
        for s in range(1, CONV_W):
            xc3 = xc3 + cw_ref[CONV_W - 1 - s:CONV_W - s, cs] * pltpu.roll(ext, s, axis=1)[:, t_len:, :]
        xc = xc3.reshape(rows, blk)
        a, mult, gi = _lru_gates(xc, n, cs, wa_ref, wx_ref, ba_ref, bx_ref, lam_ref)
        a3 = a.reshape(bt, t_len, blk)
        b3 = (mult * (gi * xc)).reshape(bt, t_len, blk)
        d = 1
        while d < t_len:
            keep = tpos >= d
            b3 = jnp.where(keep, a3 * pltpu.roll(b3, d, axis=1) + b3, b3)
            a3 = jnp.where(keep, a3 * pltpu.roll(a3, d, axis=1), a3)
            d *= 2
        h3 = b3 + a3 * h0_ref[:, :, cs]
        oa_ref[:, cs] = (h3.reshape(rows, blk) * gg_ref[:, cs].astype(f32)).astype(bf16)
        hl_ref[:, :, cs] = h3[:, t_len - 1:t_len, :]
        cn_ref[:, :, cs] = pltpu.roll(x3, CONV_W - 1, axis=1)[:, 0:CONV_W - 1, :]


def _ret_norm_gate(o, rn, sg):
    return (o * lax.rsqrt(jnp.mean(o * o, axis=-1, keepdims=True) + EPS) * rn * sg.astype(f32)).astype(bf16)


def _ret_prompt_kernel(*refs, n_seq):
    ob_ref = refs[9]

    @pl.when(pl.program_id(0) >= n_seq)
    def _():
        ob_ref[...] = jnp.zeros_like(ob_ref)

    @pl.when(pl.program_id(0) < n_seq)
    def _():
        _ret_prompt_body(*refs)


def _ret_prompt_body(q_ref, k_ref, v_ref, sg_ref, rn_ref, dm_ref, cd_ref, sd_ref, chd_ref, ob_ref, so_ref):
    c = pl.program_id(1)
    dk, dv = so_ref.shape[2], so_ref.shape[3]

    @pl.when(c == 0)
    def _():
        so_ref[...] = jnp.zeros_like(so_ref)

    for h in range(N_RET_HEADS):
        ks = slice(h * dk, (h + 1) * dk)
        vs = slice(h * dv, (h + 1) * dv)
        q = q_ref[:, ks]
        k = k_ref[:, ks]
        v = v_ref[:, vs]
        s_old = so_ref[0, h]
        scores = _dot_nt(q, k) * dm_ref[h]
        inner = _dot(scores.astype(bf16), v)
        cross = _dot(q, s_old.astype(bf16)) * cd_ref[h]
        kd = (k.astype(f32) * sd_ref[h]).astype(bf16)
        so_ref[0, h] = s_old * chd_ref[h] + _dot_tn(kd, v)
        ob_ref[:, vs] = _ret_norm_gate(inner + cross, rn_ref[h], sg_ref[:, vs])


def _ret_sample_kernel(q_ref, k_ref, v_ref, sg_ref, s_ref, rn_ref, dm_ref, cd_ref, sd_ref, chd_ref, ob_prev_ref,
                       ob_ref, so_ref, *, t_len):
    del ob_prev_ref
    rows = q_ref.shape[0]
    dk, dv = so_ref.shape[2], so_ref.shape[3]
    row = lax.broadcasted_iota(jnp.int32, (rows, 1), 0)
    for h in range(N_RET_HEADS):
        ks = slice(h * dk, (h + 1) * dk)
        vs = slice(h * dv, (h + 1) * dv)
        q = q_ref[:, ks]
        k = k_ref[:, ks]
        v = v_ref[:, vs]
        scores = _dot_nt(q, k) * dm_ref[h]
        o = _dot(scores.astype(bf16), v)
        kd = k.astype(f32) * sd_ref[h]
        for bb in range(rows // t_len):
            mine = (row >= bb * t_len) & (row < (bb + 1) * t_len)
            s_old = s_ref[bb, h]
            cross = _dot(q, s_old.astype(bf16)) * cd_ref[h]
            o = o + jnp.where(mine, cross, 0.0)
            so_ref[bb, h] = s_old * chd_ref[h] + _dot_tn(jnp.where(mine, kd, 0.0).astype(bf16), v)
        ob_ref[:, vs] = _ret_norm_gate(o, rn_ref[h], sg_ref[:, vs])


def _decay_tables(chunk, n_seq):
    log_g = jnp.log1p(-jnp.exp2(-5.0 - jnp.arange(N_RET_HEADS, dtype=f32)))
    r = jnp.arange(chunk * n_seq)
    idx = (r % chunk).astype(f32)
    same = (r[:, None] // chunk) == (r[None, :] // chunk)
    diff = idx[:, None] - idx[None, :]
    dmask = jnp.where(same & (diff >= 0), jnp.exp(jnp.maximum(diff, 0.0)[None] * log_g[:, None, None]), 0.0)
    cross_decay = jnp.exp((idx[None] + 1.0) * log_g[:, None])[..., None]
    state_decay = jnp.exp((chunk - 1.0 - idx[None]) * log_g[:, None])[..., None]
    chunk_decay = jnp.exp(chunk * log_g)[:, None, None]
    return dmask, cross_decay, state_decay, chunk_decay


def _ple_kernel(x_ref, pep_ref, pes_ref, gn_ref, wg_ref, bg_ref, pp_ref, fn_ref, yp_ref, ys_ref, *, n_prompt_tiles, final):
    i = pl.program_id(0)
    x = x_ref[...]
    u = _rms(x, gn_ref[...]).astype(bf16)
    gate = _sigmoid(_dot(u, wg_ref[...]) + bg_ref[...])
    pe = jnp.where(i < n_prompt_tiles, pep_ref[...], pes_ref[...]).astype(bf16)
    x = x + gate * _dot(pe, pp_ref[...])
    y = _rms(x, fn_ref[...]) if final else x

    @pl.when(i < n_prompt_tiles)
    def _():
        yp_ref[...] = y

    @pl.when(i >= n_prompt_tiles)
    def _():
        ys_ref[...] = y


def _layer(x_parts, pe_p, pe_s, n_prompt, seq, n_sample, dec_seq, h0_s, conv_s, ret_s, cos_t, sin_t, lp, final_norm):
    d = lp['mix_norm'].shape[0]
    mp, ms = n_prompt * seq, n_sample * dec_seq
    m = mp + ms
    d_rnn = lp['conv_w'].shape[1]
    hh = N_RET_HEADS
    dk = d // hh
    dv = lp['ret_norm'].shape[1]
    qk_dim, v_dim = hh * dk, hh * dv
    tm, tn = ROW_TILE, 1024

    ff = lp['ffn1_wg'].shape[1]
    tm1, tf1 = _ffn_tiles(m, d, ff, with_norm=True)
    x1, u = _ffn(x_parts, m, lp['ffn1_norm'], lp['ffn1_wg'], lp['ffn1_wu'], lp['ffn1_wd'], lp['mix_norm'],
                 tm=tm1, tf=tf1)

    w_in = lp['w_in']
    b_in = lp['b_in'].reshape(1, -1)

    def bias_extra(col0):
        return (b_in, (1, tn), lambda j, i, o=col0 // tn: (0, j + o))

    def with_bias(fn):
        return lambda acc, b_ref, rows, col_tile: fn(acc + b_ref[...])

    c_xa, c_ga, c_q, c_v = 0, d_rnn, 2 * d_rnn, 2 * d_rnn + 2 * qk_dim
    c_gr, c_gate = c_v + v_dim, c_v + 2 * v_dim
    tmi = IN_ROW_TILE if m % IN_ROW_TILE == 0 else tm
    proj = functools.partial(_mm, u, w_in, tm=tmi, tn=tn)
    xa = proj(col0=c_xa, n_cols=d_rnn, out_dtype=f32, epilogue=with_bias(lambda z: z),
              extras=[bias_extra(c_xa)], name="inproj_xa")
    gg = proj(col0=c_ga, n_cols=d_rnn, out_dtype=bf16, epilogue=with_bias(_gelu_tanh),
              extras=[bias_extra(c_ga)], name="inproj_ga")
    half = dk // 2
    rope_map = lambda j, i: (i, 0)
    qk = proj(col0=c_q, n_cols=2 * qk_dim, out_dtype=bf16,
              epilogue=functools.partial(_rope_epilogue, k_tile0=qk_dim // tn, k_scale=dk ** -0.5, head_dim=dk),
              extras=[bias_extra(c_q), (cos_t, (tmi, half), rope_map), (sin_t, (tmi, half), rope_map)],
              name="inproj_qk")
    v = proj(col0=c_v, n_cols=v_dim, out_dtype=bf16, epilogue=with_bias(lambda z: z),
             extras=[bias_extra(c_v)], name="inproj_v")
    sg = proj(col0=c_gr, n_cols=v_dim, out_dtype=bf16, epilogue=with_bias(lambda z: z * _sigmoid(z)),
              extras=[bias_extra(c_gr)], name="inproj_gr")
    gates = proj(col0=c_gate, n_cols=2 * d, out_dtype=bf16, epilogue=with_bias(_sigmoid),
                 extras=[bias_extra(c_gate)], name="inproj_gates")

    row2 = lambda a: a.reshape(1, -1)
    lru_w = [lp['conv_w'], row2(lp['conv_b']), _to_bf16(lp['lru_wa']), _to_bf16(lp['lru_wx']), row2(lp['lru_ba']),
             row2(lp['lru_bx']), row2(lp['lru_lambda'])]
    nb = N_LRU_BLOCKS
    blk = d_rnn // nb
    z2 = lambda *_: (0, 0)
    z3 = lambda *_: (0, 0, 0)
    lru_w_specs = [pl.BlockSpec((CONV_W, d_rnn), z2), pl.BlockSpec((1, d_rnn), z2),
                   pl.BlockSpec((nb, blk, blk), z3), pl.BlockSpec((nb, blk, blk), z3),
                   pl.BlockSpec((1, d_rnn), z2), pl.BlockSpec((1, d_rnn), z2), pl.BlockSpec((1, d_rnn), z2)]

    tt = 512
    ntt = seq // tt
    assert ms % tt == 0 and ms // tt <= ntt
    seq_rows = lambda b, t: (jnp.minimum(b * ntt + t, mp // tt - 1), 0)
    all_rows = lambda b, t: (jnp.minimum(b * ntt + t, m // tt - 1), 0)
    per_seq = lambda b, t: (jnp.minimum(b, n_prompt - 1), 0, 0)
    oa, hl_p, cn_p = pl.pallas_call(
        functools.partial(_lru_prompt_kernel, n_seq=n_prompt),
        grid=(n_prompt + 1, ntt),
        in_specs=[pl.BlockSpec((tt, d_rnn), seq_rows), pl.BlockSpec((tt, d_rnn), seq_rows)] + lru_w_specs,
        out_specs=[pl.BlockSpec((tt, d_rnn), all_rows), pl.BlockSpec((1, 1, d_rnn), per_seq),
                   pl.BlockSpec((1, CONV_W - 1, d_rnn), per_seq)],
        out_shape=[jax.ShapeDtypeStruct((m, d_rnn), bf16), jax.ShapeDtypeStruct((n_prompt, 1, d_rnn), f32),
                   jax.ShapeDtypeStruct((n_prompt, CONV_W - 1, d_rnn), f32)],
        scratch_shapes=[pltpu.VMEM((tt + 8, d_rnn), f32), pltpu.VMEM((tt, d_rnn), f32), pltpu.VMEM((tt, d_rnn), f32),
                        pltpu.VMEM((tt, d_rnn), f32), pltpu.VMEM((1, d_rnn), f32)],
        compiler_params=_cparams(("arbitrary", "arbitrary")),
        name="lru_prompt",
    )(xa, gg, *lru_w)

    bt = 32
    rows_s = bt * dec_seq
    s_off = mp // rows_s
    scp = jnp.pad(conv_s, ((0, 0), (dec_seq - (CONV_W - 1), 0), (0, 0)))
    samp_rows = lambda i: (s_off + i, 0)
    samp3 = lambda i: (i, 0, 0)
    oa, hl_s, cn_s = pl.pallas_call(
        _lru_sample_kernel,
        grid=(n_sample // bt,),
        in_specs=[pl.BlockSpec((rows_s, d_rnn), samp_rows), pl.BlockSpec((rows_s, d_rnn), samp_rows),
                  pl.BlockSpec((bt, dec_seq, d_rnn), samp3), pl.BlockSpec((bt, 1, d_rnn), samp3)] + lru_w_specs + [_ANY],
        out_specs=[pl.BlockSpec((rows_s, d_rnn), samp_rows), pl.BlockSpec((bt, 1, d_rnn), samp3),
                   pl.BlockSpec((bt, CONV_W - 1, d_rnn), samp3)],
        out_shape=[jax.ShapeDtypeStruct((m, d_rnn), bf16), jax.ShapeDtypeStruct((n_sample, 1, d_rnn), f32),
                   jax.ShapeDtypeStruct((n_sample, CONV_W - 1, d_rnn), f32)],
        input_output_aliases={11: 0},
        compiler_params=_cparams(("parallel",)),
        name="lru_sample",
    )(xa, gg, scp, h0_s.reshape(n_sample, 1, d_rnn), *lru_w, oa)

    rn = lp['ret_norm'].reshape(hh, 1, dv)
    ct = RET_CHUNK
    nct = seq // ct
    tabs = _decay_tables(ct, 1)
    full3 = lambda shape: pl.BlockSpec(shape, lambda *_: (0, 0, 0))
    tab_specs = lambda r: [full3((hh, 1, dv)), full3((hh, r, r)), full3((hh, r, 1)), full3((hh, r, 1)), full3((hh, 1, 1))]
    assert ms % ct == 0 and ms // ct <= nct
    chunk_row = lambda b, c: jnp.minimum(b * nct + c, mp // ct - 1)
    ob, so_p = pl.pallas_call(
        functools.partial(_ret_prompt_kernel, n_seq=n_prompt),
        grid=(n_prompt + 1, nct),
        in_specs=[pl.BlockSpec((ct, qk_dim), lambda b, c: (chunk_row(b, c), 0)),
                  pl.BlockSpec((ct, qk_dim), lambda b, c: (chunk_row(b, c), 1)),
                  pl.BlockSpec((ct, v_dim), lambda b, c: (chunk_row(b, c), 0)),
                  pl.BlockSpec((ct, v_dim), lambda b, c: (chunk_row(b, c), 0))] + tab_specs(ct),
        out_specs=[pl.BlockSpec((ct, v_dim), lambda b, c: (jnp.minimum(b * nct + c, m // ct - 1), 0)),
                   pl.BlockSpec((1, hh, dk, dv), lambda b, c: (jnp.minimum(b, n_prompt - 1), 0, 0, 0))],
        out_shape=[jax.ShapeDtypeStruct((m, v_dim), bf16), jax.ShapeDtypeStruct((n_prompt, hh, dk, dv), f32)],
        compiler_params=_cparams(("arbitrary", "arbitrary")),
        name="ret_prompt",
    )(qk, qk, v, sg, rn, *tabs)

    pr = SAMPLE_PAIR
    rows_r = pr * dec_seq
    r_off = mp // rows_r
    tabs = _decay_tables(dec_seq, pr)
    pair_rows = lambda p: (r_off + p, 0)
    pair_state = lambda p: (p, 0, 0, 0)
    ob, so_s = pl.pallas_call(
        functools.partial(_ret_sample_kernel, t_len=dec_seq),
        grid=(n_sample // pr,),
        in_specs=[pl.BlockSpec((rows_r, qk_dim), pair_rows), pl.BlockSpec((rows_r, qk_dim), lambda p: (r_off + p, 1)),
                  pl.BlockSpec((rows_r, v_dim), pair_rows), pl.BlockSpec((rows_r, v_dim), pair_rows),
                  pl.BlockSpec((pr, hh, dk, dv), pair_state)] + tab_specs(rows_r) + [_ANY],
        out_specs=[pl.BlockSpec((rows_r, v_dim), pair_rows), pl.BlockSpec((pr, hh, dk, dv), pair_state)],
        out_shape=[jax.ShapeDtypeStruct((m, v_dim), bf16), jax.ShapeDtypeStruct((n_sample, hh, dk, dv), f32)],
        input_output_aliases={10: 0},
        compiler_params=_cparams(("parallel",)),
        name="ret_sample",
    )(qk, qk, v, sg, ret_s, rn, *tabs, ob)

    tile = lambda j, i: (i, j)
    pa = _mm(oa, lp['proj_a'], col0=0, n_cols=d, out_dtype=f32, epilogue=lambda acc, rows, col_tile: acc,
             name="proj_a")
    tnb = 512
    merged = _mm(ob, lp['proj_b'], col0=0, n_cols=d, out_dtype=bf16, tn=tnb, resident=False,
                 epilogue=lambda acc, pa_ref, ga_ref, gb_ref, rows, col_tile: (
                     ga_ref[rows, :].astype(f32) * pa_ref[rows, :] + gb_ref[rows, :].astype(f32) * acc),
                 extras=[(pa, (tm, tnb), tile), (gates, (tm, tnb), tile),
                         (gates, (tm, tnb), lambda j, i: (i, j + d // tnb))], name="proj_b_merge")
    x2 = _mm(merged, lp['w_out'], col0=0, n_cols=d, out_dtype=f32,
             epilogue=lambda acc, x_ref, rows, col_tile: x_ref[rows, :] + acc,
             extras=[(x1, (tm, tn), tile)], name="w_out")

    tm2, tf2 = _ffn_tiles(m, d, ff, with_norm=False)
    x3 = _ffn([(x2, 0)], m, lp['ffn2_norm'], lp['ffn2_wg'], lp['ffn2_wu'], lp['ffn2_wd'], tm=tm2, tf=tf2)

    tp = 512
    ple_dim = pe_p.shape[1]
    fin = final_norm is not None
    fn = (final_norm if fin else lp['ple_norm']).reshape(1, d)
    npt = mp // tp
    c2 = lambda i: (0, 0)
    prow = lambda i: (jnp.minimum(i, npt - 1), 0)
    srow = lambda i: (jnp.maximum(i - npt, 0), 0)
    y_p, y_s = pl.pallas_call(
        functools.partial(_ple_kernel, n_prompt_tiles=npt, final=fin),
        grid=(m // tp,),
        in_specs=[pl.BlockSpec((tp, d), lambda i: (i, 0)), pl.BlockSpec((tp, ple_dim), prow),
                  pl.BlockSpec((tp, ple_dim), srow), pl.BlockSpec((1, d), c2), pl.BlockSpec((d, d), c2),
                  pl.BlockSpec((1, d), c2), pl.BlockSpec((ple_dim, d), c2), pl.BlockSpec((1, d), c2)],
        out_specs=[pl.BlockSpec((tp, d), prow), pl.BlockSpec((tp, d), srow)],
        out_shape=[jax.ShapeDtypeStruct((mp, d), f32), jax.ShapeDtypeStruct((ms, d), f32)],
        compiler_params=_cparams(("arbitrary",)),
        name="ple",
    )(x3, pe_p, pe_s, lp['ple_norm'].reshape(1, d), _to_bf16(lp['ple_wg']), lp['ple_bg'].reshape(1, d),
      _to_bf16(lp['ple_proj']), fn)

    states = (hl_p.reshape(n_prompt, d_rnn), cn_p, so_p, hl_s.reshape(n_sample, d_rnn), cn_s, so_s)
    return y_p, y_s, states


def kernel(x_prompt, x_sample, p_prompt, p_sample, state_lru, state_conv, state_ret, ffn1_norm, ffn1_wg, ffn1_wu, ffn1_wd, mix_norm, w_in, b_in, conv_w, conv_b, lru_wa, lru_ba, lru_wx, lru_bx, lru_lambda, ret_norm, proj_a, proj_b, w_out, ffn2_norm, ffn2_wg, ffn2_wu, ffn2_wd, ple_norm, ple_wg, ple_bg, ple_proj, final_norm):
    params = dict(ffn1_norm=ffn1_norm, ffn1_wg=ffn1_wg, ffn1_wu=ffn1_wu, ffn1_wd=ffn1_wd, mix_norm=mix_norm,
                  w_in=w_in, b_in=b_in, conv_w=conv_w, conv_b=conv_b, lru_wa=lru_wa, lru_ba=lru_ba, lru_wx=lru_wx,
                  lru_bx=lru_bx, lru_lambda=lru_lambda, ret_norm=ret_norm, proj_a=proj_a, proj_b=proj_b,
                  w_out=w_out, ffn2_norm=ffn2_norm, ffn2_wg=ffn2_wg, ffn2_wu=ffn2_wu, ffn2_wd=ffn2_wd,
                  ple_norm=ple_norm, ple_wg=ple_wg, ple_bg=ple_bg, ple_proj=ple_proj)
    depth = w_in.shape[0]
    n_prompt, seq, d = x_prompt.shape
    n_sample, dec_seq, _ = x_sample.shape
    mp, ms = n_prompt * seq, n_sample * dec_seq
    dk = d // N_RET_HEADS

    y_p = x_prompt.astype(f32).reshape(mp, d)
    y_s = x_sample.astype(f32).reshape(ms, d)
    cos_t, sin_t = _rope_table(seq, dec_seq, ms, dk // 2)
    cos_t, sin_t = (jnp.concatenate([jnp.tile(t[:seq], (n_prompt, 1)), t[seq:]], axis=0) for t in (cos_t, sin_t))
    outs = [[] for _ in range(6)]
    for i in range(depth):
        lp = {k: v[i].astype(f32) for k, v in params.items()}
        y_p, y_s, st = _layer([(y_p, 0), (y_s, mp)], p_prompt[i].astype(f32).reshape(mp, -1),
                              p_sample[i].astype(f32).reshape(ms, -1), n_prompt, seq, n_sample, dec_seq,
                              state_lru[i].astype(f32), state_conv[i].astype(f32), state_ret[i].astype(f32),
                              cos_t, sin_t, lp, final_norm.astype(f32) if i == depth - 1 else None)
        for o, s in zip(outs, st):
            o.append(s)
    y_prompt = y_p.reshape(n_prompt, seq, d).astype(x_prompt.dtype)
    y_sample = y_s.reshape(n_sample, dec_seq, d).astype(x_sample.dtype)
    lru_p, conv_p, ret_p, lru_s, conv_s, ret_s = (jnp.stack(o) for o in outs)
    return (y_prompt, y_sample, lru_p.astype(state_lru.dtype), conv_p.astype(state_conv.dtype),
            ret_p.astype(state_ret.dtype), lru_s.astype(state_lru.dtype), conv_s.astype(state_conv.dtype),
            ret_s.astype(state_ret.dtype))
```

```python
import functools

import jax
import jax.numpy as jnp
from jax import lax
from jax.experimental import pallas as pl
from jax.experimental.pallas import tpu as pltpu

f32 = jnp.float32
bf16 = jnp.bfloat16

N_LRU_BLOCKS = 8
CONV_W = 4
LRU_C = 8.0
N_RET_HEADS = 8
ROPE_BASE = 10000.0
EPS = 1e-6
PAST_LEN = 16384

RET_CHUNK = 256
SAMPLE_PAIR = 2
ROW_TILE = 1024
IN_ROW_TILE = 1536
GATE_COL_TILE = 256
VMEM_LIMIT = 58 * 1024 * 1024


def _cparams(sem):
    return pltpu.CompilerParams(dimension_semantics=sem, vmem_limit_bytes=VMEM_LIMIT)


def _sigmoid(x):
    return 1.0 / (1.0 + jnp.exp(-x))


def _rms(x, g):
    return x * lax.rsqrt(jnp.mean(x * x, axis=-1, keepdims=True) + EPS) * g


def _dot(a, b):
    return jnp.dot(a, b, preferred_element_type=f32)


def _dot_nt(a, b):
    return lax.dot_general(a, b, (((1,), (1,)), ((), ())), preferred_element_type=f32)


def _dot_tn(a, b):
    return lax.dot_general(a, b, (((0,), (0,)), ((), ())), preferred_element_type=f32)


_ANY = pl.BlockSpec(memory_space=pl.ANY)


def _cast_kernel(x_ref, o_ref):
    o_ref[...] = x_ref[...].astype(o_ref.dtype)


def _to_bf16(w):
    w2 = w.reshape(-1, w.shape[-1])
    return pl.pallas_call(_cast_kernel, out_shape=jax.ShapeDtypeStruct(w2.shape, bf16), name="to_bf16")(w2).reshape(w.shape)


def _rope_table_kernel(inv_ref, cos_ref, sin_ref, *, seq, dec_seq):
    rows = cos_ref.shape[0]
    r = lax.broadcasted_iota(jnp.int32, (rows, inv_ref.shape[1]), 0)
    pos = jnp.where(r < seq, r, PAST_LEN + lax.rem(r - seq, dec_seq))
    ang = pos.astype(f32) * inv_ref[...]
    cos_ref[...] = jnp.cos(ang)
    sin_ref[...] = jnp.sin(ang)


def _rope_table(seq, dec_seq, n_sample_rows, half):
    inv = (ROPE_BASE ** (-jnp.arange(half, dtype=f32) / half)).reshape(1, half)
    rows = seq + n_sample_rows
    return pl.pallas_call(
        functools.partial(_rope_table_kernel, seq=seq, dec_seq=dec_seq),
        out_shape=(jax.ShapeDtypeStruct((rows, half), f32), jax.ShapeDtypeStruct((rows, half), f32)),
        name="rope_table",
    )(inv)


def _ffn_kernel(*refs, group_rows, n_tiles, emit_norm):
    n_groups = len(group_rows)
    x_hbm = refs[:n_groups]
    g_ref, wg_ref, wu_ref, wd_ref = refs[n_groups:n_groups + 4]
    rest = list(refs[n_groups + 4:])
    g2_ref = rest.pop(0) if emit_norm else None
    xo_ref = rest.pop(0)
    u2_ref = rest.pop(0) if emit_norm else None
    (u_scr,) = rest
    i = pl.program_id(0)
    f = pl.program_id(1)
    tm = xo_ref.shape[0]

    @pl.when(f == 0)
    def _():
        for t in range(n_tiles):
            @pl.when(i == t)
            def _(t=t):
                for gi, (r0, nr) in enumerate(group_rows):
                    lo, hi = max(t * tm, r0), min((t + 1) * tm, r0 + nr)
                    if lo < hi:
                        pltpu.sync_copy(x_hbm[gi].at[lo - r0:hi - r0, :], xo_ref.at[lo - t * tm:hi - t * tm, :])

        u_scr[...] = _rms(xo_ref[...], g_ref[...]).astype(bf16)

    u = u_scr[...]
    hs = []
    for c0 in range(0, wg_ref.shape[1], FFN_SUB_TILE):
        cols = slice(c0, c0 + FFN_SUB_TILE)
        hg = _dot(u, wg_ref[:, cols].astype(bf16))
        hu = _dot(u, wu_ref[:, cols].astype(bf16))
        hs.append((0.5 * hg * _sigmoid(hg) * hu).astype(bf16))
    h = hs[0] if len(hs) == 1 else jnp.concatenate(hs, axis=1)
    xo_ref[...] += _dot(h, wd_ref[...].astype(bf16))

    if emit_norm:
        @pl.when(f == pl.num_programs(1) - 1)
        def _():
            u2_ref[...] = _rms(xo_ref[...], g2_ref[...]).astype(bf16)


FFN_SUB_TILE = 256
FFN_VMEM_BUDGET = 60 * 1024 * 1024


def _ffn_tiles(m, d, ff, with_norm):
    sub = FFN_SUB_TILE
    for tf in (4 * sub, 2 * sub, sub):
        if ff % tf:
            continue
        for tm in range(m, 1023, -256):
            if m % tm == 0:
                rows = tm * d * (4 + 2 + (2 if with_norm else 0))
                weights = 3 * d * tf * 4 * 2
                temps = 2 * tm * sub * 4 + tm * tf * 2 + (2 * d * sub + tf * d) * 2
                if rows + weights + temps <= FFN_VMEM_BUDGET:
                    return tm, tf
    raise ValueError("no FFN tiling fits VMEM")


def _ffn(parts, m_total, g, wg, wu, wd, g2=None, *, tm, tf):
    d = wg.shape[0]
    ff = wg.shape[1]
    emit_norm = g2 is not None
    row = lambda i, f: (i, 0)
    out_mode = dict(pipeline_mode=pl.Buffered(1))
    in_specs = [_ANY] * len(parts) + [
        pl.BlockSpec((1, d), lambda i, f: (0, 0)),
        pl.BlockSpec((d, tf), lambda i, f: (0, f)),
        pl.BlockSpec((d, tf), lambda i, f: (0, f)),
        pl.BlockSpec((tf, d), lambda i, f: (f, 0)),
    ]
    args = [xp for xp, _ in parts] + [g.reshape(1, d), wg, wu, wd]
    out_shape = [jax.ShapeDtypeStruct((m_total, d), f32)]
    out_specs = [pl.BlockSpec((tm, d), row, **out_mode)]
    if emit_norm:
        in_specs.append(pl.BlockSpec((1, d), lambda i, f: (0, 0)))
        args.append(g2.reshape(1, d))
        out_shape.append(jax.ShapeDtypeStruct((m_total, d), bf16))
        out_specs.append(pl.BlockSpec((tm, d), row, **out_mode))
    outs = pl.pallas_call(
        functools.partial(_ffn_kernel, group_rows=tuple((r0, xp.shape[0]) for xp, r0 in parts),
                          n_tiles=m_total // tm, emit_norm=emit_norm),
        grid=(m_total // tm, ff // tf),
        in_specs=in_specs,
        out_specs=out_specs,
        out_shape=out_shape,
        scratch_shapes=[pltpu.VMEM((tm, d), bf16)],
        compiler_params=_cparams(("arbitrary", "arbitrary")),
        name="ffn",
    )(*args)
    return outs if emit_norm else outs[0]


MM_ROW_SPLIT = 2


def _mm_kernel(a_ref, b_ref, *rest, epilogue, col_axis):
    *extra, o_ref, bw_scr = rest
    j = pl.program_id(col_axis)
    slot = j if col_axis == 1 else 0

    @pl.when(pl.program_id(1 - col_axis) == 0)
    def _():
        bw_scr[slot] = b_ref[...].astype(bf16)

    rows = a_ref.shape[0] // MM_ROW_SPLIT
    for r in range(MM_ROW_SPLIT):
        rs = slice(r * rows, (r + 1) * rows)
        o_ref[rs, :] = epilogue(_dot(a_ref[rs, :], bw_scr[slot]), *extra, rows=rs, col_tile=j).astype(o_ref.dtype)


def _mm(a, b, *, col0, n_cols, out_dtype, epilogue, extras=(), tm=ROW_TILE, tn=1024, resident=True, name="mm"):
    m, k = a.shape
    off = col0 // tn
    nj = n_cols // tn
    if resident:
        grid, order = (m // tm, nj), (lambda f: (lambda i, j: f(j, i)))
        b_map = lambda j, i: (0, jnp.where(i == 0, j, nj - 1) + off)
    else:
        grid, order = (nj, m // tm), (lambda f: f)
        b_map = lambda j, i: (0, j + off)
    in_specs = [pl.BlockSpec((tm, k), order(lambda j, i: (i, 0))), pl.BlockSpec((k, tn), order(b_map))]
    args = [a, b]
    for arr, blk, imap in extras:
        in_specs.append(pl.BlockSpec(blk, order(imap)))
        args.append(arr)
    return pl.pallas_call(
        functools.partial(_mm_kernel, epilogue=epilogue, col_axis=1 if resident else 0),
        grid=grid,
        in_specs=in_specs,
        out_specs=pl.BlockSpec((tm, tn), order(lambda j, i: (i, j))),
        out_shape=jax.ShapeDtypeStruct((m, n_cols), out_dtype),
        scratch_shapes=[pltpu.VMEM((nj if resident else 1, k, tn), bf16)],
        compiler_params=_cparams(("arbitrary", "arbitrary")),
        name=name,
    )(*args)


def _gelu_tanh(x):
    return 0.5 * x * (1.0 + jnp.tanh(0.7978845608028654 * (x + 0.044715 * (x * x * x))))


def _rope_epilogue(acc, bias_ref, cos_ref, sin_ref, *, rows, col_tile, k_tile0, k_scale, head_dim):
    z = acc + bias_ref[...]
    cos = cos_ref[rows, :]
    sin = sin_ref[rows, :]
    half = head_dim // 2
    parts = []
    for h0 in range(0, z.shape[1], head_dim):
        x1 = z[:, h0:h0 + half]
        x2 = z[:, h0 + half:h0 + head_dim]
        parts += [x1 * cos - x2 * sin, x2 * cos + x1 * sin]
    scale = jnp.where(col_tile >= k_tile0, k_scale, 1.0).astype(f32)
    return jnp.concatenate(parts, axis=1) * scale


def _lru_gates(xc, n, cs, wa_ref, wx_ref, ba_ref, bx_ref, lam_ref):
    xcb = xc.astype(bf16)
    r = _sigmoid(_dot(xcb, wa_ref[n]) + ba_ref[:, cs])
    gi = _sigmoid(_dot(xcb, wx_ref[n]) + bx_ref[:, cs])
    log_a = -LRU_C * r * jax.nn.softplus(-lam_ref[:, cs])
    a = jnp.exp(log_a)
    a2 = a * a
    one_minus = jnp.where(log_a < -0.25, 1.0 - a2, -jnp.tanh(log_a) * (a2 + 1.0))
    return a, one_minus * lax.rsqrt(jnp.maximum(one_minus, 1e-36)), gi


def _lru_prompt_kernel(*refs, n_seq):
    oa_ref = refs[9]

    @pl.when(pl.program_id(0) >= n_seq)
    def _():
        oa_ref[...] = jnp.zeros_like(oa_ref)

    @pl.when(pl.program_id(0) < n_seq)
    def _():
        _lru_prompt_body(*refs)


def _lru_prompt_body(xa_ref, gg_ref, cw_ref, cb_ref, wa_ref, wx_ref, ba_ref, bx_ref, lam_ref,
                     oa_ref, hl_ref, cn_ref, xe_scr, a_scr, b_scr, h_scr, hc_scr):
    tt, c = xa_ref.shape
    blk = c // N_LRU_BLOCKS
    hist = 8
    t = pl.program_id(1)

    @pl.when(t == 0)
    def _():
        xe_scr[0:hist, :] = jnp.zeros((hist, c), f32)
        hc_scr[...] = jnp.zeros_like(hc_scr)

    xe_scr[hist:hist + tt, :] = xa_ref[...]
    pos = lax.broadcasted_iota(jnp.int32, (tt, 1), 0) + t * tt
    for n in range(N_LRU_BLOCKS):
        cs = slice(n * blk, (n + 1) * blk)
        ext = xe_scr[:, cs]
        xc = cb_ref[:, cs] + cw_ref[CONV_W - 1:CONV_W, cs] * ext[hist:, :]
        for s in range(1, CONV_W):
            xc = xc + cw_ref[CONV_W - 1 - s:CONV_W - s, cs] * pltpu.roll(ext, s, axis=0)[hist:, :]
        a, mult, gi = _lru_gates(xc, n, cs, wa_ref, wx_ref, ba_ref, bx_ref, lam_ref)
        mult = jnp.where(pos == 0, 1.0, mult)
        a_scr[:, cs] = a
        b_scr[:, cs] = mult * (gi * xc)

    def step(i, h):
        h = a_scr[pl.ds(i, 1), :] * h + b_scr[pl.ds(i, 1), :]
        h_scr[pl.ds(i, 1), :] = h
        return h

    h = lax.fori_loop(0, tt, step, hc_scr[...], unroll=8)
    hc_scr[...] = h
    hl_ref[0] = h
    for n in range(N_LRU_BLOCKS):
        cs = slice(n * blk, (n + 1) * blk)
        oa_ref[:, cs] = (h_scr[:, cs] * gg_ref[:, cs].astype(f32)).astype(bf16)
    cn_ref[0] = xe_scr[pl.ds(hist + tt - (CONV_W - 1), CONV_W - 1), :]
    xe_scr[0:hist, :] = xe_scr[pl.ds(tt, hist), :]


def _lru_sample_kernel(xa_ref, gg_ref, scp_ref, h0_ref, cw_ref, cb_ref, wa_ref, wx_ref, ba_ref, bx_ref, lam_ref,
                       oa_prev_ref, oa_ref, hl_ref, cn_ref):
    del oa_prev_ref
    rows, c = xa_ref.shape
    bt, t_len = scp_ref.shape[0], scp_ref.shape[1]
    blk = c // N_LRU_BLOCKS
    tpos = lax.broadcasted_iota(jnp.int32, (bt, t_len, blk), 1)
    for n in range(N_LRU_BLOCKS):
        cs = slice(n * blk, (n + 1) * blk)
        x3 = xa_ref[:, cs].reshape(bt, t_len, blk)
        ext = jnp.concatenate([scp_ref[:, :, cs], x3], axis=1)
        xc3 = cb_ref[:, cs] + cw_ref[CONV_W - 1:CONV_W, cs] * x3
        for s in range(1, CONV_W):
            xc3 = xc3 + cw_ref[CONV_W - 1 - s:CONV_W - s, cs] * pltpu.roll(ext, s, axis=1)[:, t_len:, :]
        xc = xc3.reshape(rows, blk)
        a, mult, gi = _lru_gates(xc, n, cs, wa_ref, wx_ref, ba_ref, bx_ref, lam_ref)
        a3 = a.reshape(bt, t_len, blk)
        b3 = (mult * (gi * xc)).reshape(bt, t_len, blk)
        d = 1
        while d < t_len:
            keep = tpos >= d
            b3 = jnp.where(keep, a3 * pltpu.roll(b3, d, axis=1) + b3, b3)
            a3 = jnp.where(keep, a3 * pltpu.roll(a3, d, axis=1), a3)
            d *= 2
        h3 = b3 + a3 * h0_ref[:, :, cs]
        oa_ref[:, cs] = (h3.reshape(rows, blk) * gg_ref[:, cs].astype(f32)).astype(bf16)
        hl_ref[:, :, cs] = h3[:, t_len - 1:t_len, :]
        cn_ref[:, :, cs] = pltpu.roll(x3, CONV_W - 1, axis=1)[:, 0:CONV_W - 1, :]


def _ret_norm_gate(o, rn, sg):
    return (o * lax.rsqrt(jnp.mean(o * o, axis=-1, keepdims=True) + EPS) * rn * sg.astype(f32)).astype(bf16)


def _ret_prompt_kernel(*refs, n_seq):
    ob_ref = refs[9]

    @pl.when(pl.program_id(0) >= n_seq)
    def _():
        ob_ref[...] = jnp.zeros_like(ob_ref)

    @pl.when(pl.program_id(0) < n_seq)
    def _():
        _ret_prompt_body(*refs)


def _ret_prompt_body(q_ref, k_ref, v_ref, sg_ref, rn_ref, dm_ref, cd_ref, sd_ref, chd_ref, ob_ref, so_ref):
    c = pl.program_id(1)
    dk, dv = so_ref.shape[2], so_ref.shape[3]

    @pl.when(c == 0)
    def _():
        so_ref[...] = jnp.zeros_like(so_ref)

    for h in range(N_RET_HEADS):
        ks = slice(h * dk, (h + 1) * dk)
        vs = slice(h * dv, (h + 1) * dv)
        q = q_ref[:, ks]
        k = k_ref[:, ks]
        v = v_ref[:, vs]
        s_old = so_ref[0, h]
        scores = _dot_nt(q, k) * dm_ref[h]
        inner = _dot(scores.astype(bf16), v)
        cross = _dot(q, s_old.astype(bf16)) * cd_ref[h]
        kd = (k.astype(f32) * sd_ref[h]).astype(bf16)
        so_ref[0, h] = s_old * chd_ref[h] + _dot_tn(kd, v)
        ob_ref[:, vs] = _ret_norm_gate(inner + cross, rn_ref[h], sg_ref[:, vs])


def _ret_sample_body(q_ref, k_ref, v_ref, sg_ref, s_ref, rn_ref, dm_ref, cd_ref, sd_ref, chd_ref, ob_ref, so_ref,
                     *, t_len, head0):
    rows = q_ref.shape[0]
    n_heads, dk, dv = so_ref.shape[1], so_ref.shape[2], so_ref.shape[3]
    row = lax.broadcasted_iota(jnp.int32, (rows, 1), 0)
    for hl in range(n_heads):
        h = head0 + hl
        ks = slice(hl * dk, (hl + 1) * dk)
        vs = slice(hl * dv, (hl + 1) * dv)
        q = q_ref[:, ks]
        k = k_ref[:, ks]
        v = v_ref[:, vs]
        scores = _dot_nt(q, k) * dm_ref[h]
        o = _dot(scores.astype(bf16), v)
        kd = k.astype(f32) * sd_ref[h]
        for bb in range(rows // t_len):
            mine = (row >= bb * t_len) & (row < (bb + 1) * t_len)
            s_old = s_ref[bb, hl]
            cross = _dot(q, s_old.astype(bf16)) * cd_ref[h]
            o = o + jnp.where(mine, cross, 0.0)
            so_ref[bb, hl] = s_old * chd_ref[h] + _dot_tn(jnp.where(mine, kd, 0.0).astype(bf16), v)
        ob_ref[:, vs] = _ret_norm_gate(o, rn_ref[h], sg_ref[:, vs])


def _gates_ret_kernel(a_ref, b_ref, bias_ref, q_ref, k_ref, v_ref, sg_ref, s_ref, rn_ref, dm_ref, cd_ref, sd_ref,
                      chd_ref, ob_prev_ref, g_ref, ob_ref, so_ref, bw_scr, *, t_len):
    del ob_prev_ref
    i, j = pl.program_id(0), pl.program_id(1)

    @pl.when(i == 0)
    def _():
        bw_scr[j] = b_ref[...].astype(bf16)

    rows = a_ref.shape[0] // MM_ROW_SPLIT
    for r in range(MM_ROW_SPLIT):
        rs = slice(r * rows, (r + 1) * rows)
        g_ref[rs, :] = _sigmoid(_dot(a_ref[rs, :], bw_scr[j]) + bias_ref[...]).astype(g_ref.dtype)

    n_parts = N_RET_HEADS // so_ref.shape[1]
    part = lax.rem(i * pl.num_programs(1) + j, n_parts)
    _ret_sample_body(q_ref, k_ref, v_ref, sg_ref, s_ref, rn_ref, dm_ref, cd_ref, sd_ref, chd_ref, ob_ref, so_ref,
                     t_len=t_len, head0=part * so_ref.shape[1])


def _decay_tables(chunk, n_seq):
    log_g = jnp.log1p(-jnp.exp2(-5.0 - jnp.arange(N_RET_HEADS, dtype=f32)))
    r = jnp.arange(chunk * n_seq)
    idx = (r % chunk).astype(f32)
    same = (r[:, None] // chunk) == (r[None, :] // chunk)
    diff = idx[:, None] - idx[None, :]
    dmask = jnp.where(same & (diff >= 0), jnp.exp(jnp.maximum(diff, 0.0)[None] * log_g[:, None, None]), 0.0)
    cross_decay = jnp.exp((idx[None] + 1.0) * log_g[:, None])[..., None]
    state_decay = jnp.exp((chunk - 1.0 - idx[None]) * log_g[:, None])[..., None]
    chunk_decay = jnp.exp(chunk * log_g)[:, None, None]
    return dmask, cross_decay, state_decay, chunk_decay


def _ple_kernel(x_ref, pep_ref, pes_ref, gn_ref, wg_ref, bg_ref, pp_ref, fn_ref, yp_ref, ys_ref, *, n_prompt_tiles, final):
    i = pl.program_id(0)
    x = x_ref[...]
    u = _rms(x, gn_ref[...]).astype(bf16)
    gate = _sigmoid(_dot(u, wg_ref[...]) + bg_ref[...])
    pe = jnp.where(i < n_prompt_tiles, pep_ref[...], pes_ref[...]).astype(bf16)
    x = x + gate * _dot(pe, pp_ref[...])
    y = _rms(x, fn_ref[...]) if final else x

    @pl.when(i < n_prompt_tiles)
    def _():
        yp_ref[...] = y

    @pl.when(i >= n_prompt_tiles)
    def _():
        ys_ref[...] = y


def _layer(x_parts, pe_p, pe_s, n_prompt, seq, n_sample, dec_seq, h0_s, conv_s, ret_s, cos_t, sin_t, lp, final_norm):
    d = lp['mix_norm'].shape[0]
    mp, ms = n_prompt * seq, n_sample * dec_seq
    m = mp + ms
    d_rnn = lp['conv_w'].shape[1]
    hh = N_RET_HEADS
    dk = d // hh
    dv = lp['ret_norm'].shape[1]
    qk_dim, v_dim = hh * dk, hh * dv
    tm, tn = ROW_TILE, 1024

    ff = lp['ffn1_wg'].shape[1]
    tm1, tf1 = _ffn_tiles(m, d, ff, with_norm=True)
    x1, u = _ffn(x_parts, m, lp['ffn1_norm'], lp['ffn1_wg'], lp['ffn1_wu'], lp['ffn1_wd'], lp['mix_norm'],
                 tm=tm1, tf=tf1)

    w_in = lp['w_in']
    b_in = lp['b_in'].reshape(1, -1)

    def bias_extra(col0):
        return (b_in, (1, tn), lambda j, i, o=col0 // tn: (0, j + o))

    def with_bias(fn):
        return lambda acc, b_ref, rows, col_tile: fn(acc + b_ref[...])

    c_xa, c_ga, c_q, c_v = 0, d_rnn, 2 * d_rnn, 2 * d_rnn + 2 * qk_dim
    c_gr, c_gate = c_v + v_dim, c_v + 2 * v_dim
    tmi = IN_ROW_TILE if m % IN_ROW_TILE == 0 else tm
    proj = functools.partial(_mm, u, w_in, tm=tmi, tn=tn)
    xa = proj(col0=c_xa, n_cols=d_rnn, out_dtype=f32, epilogue=with_bias(lambda z: z),
              extras=[bias_extra(c_xa)], name="inproj_xa")
    gg = proj(col0=c_ga, n_cols=d_rnn, out_dtype=bf16, epilogue=with_bias(_gelu_tanh),
              extras=[bias_extra(c_ga)], name="inproj_ga")
    half = dk // 2
    rope_map = lambda j, i: (i, 0)
    qk = proj(col0=c_q, n_cols=2 * qk_dim, out_dtype=bf16,
              epilogue=functools.partial(_rope_epilogue, k_tile0=qk_dim // tn, k_scale=dk ** -0.5, head_dim=dk),
              extras=[bias_extra(c_q), (cos_t, (tmi, half), rope_map), (sin_t, (tmi, half), rope_map)],
              name="inproj_qk")
    v = proj(col0=c_v, n_cols=v_dim, out_dtype=bf16, epilogue=with_bias(lambda z: z),
             extras=[bias_extra(c_v)], name="inproj_v")
    sg = proj(col0=c_gr, n_cols=v_dim, out_dtype=bf16, epilogue=with_bias(lambda z: z * _sigmoid(z)),
              extras=[bias_extra(c_gr)], name="inproj_gr")

    row2 = lambda a: a.reshape(1, -1)
    lru_w = [lp['conv_w'], row2(lp['conv_b']), _to_bf16(lp['lru_wa']), _to_bf16(lp['lru_wx']), row2(lp['lru_ba']),
             row2(lp['lru_bx']), row2(lp['lru_lambda'])]
    nb = N_LRU_BLOCKS
    blk = d_rnn // nb
    z2 = lambda *_: (0, 0)
    z3 = lambda *_: (0, 0, 0)
    lru_w_specs = [pl.BlockSpec((CONV_W, d_rnn), z2), pl.BlockSpec((1, d_rnn), z2),
                   pl.BlockSpec((nb, blk, blk), z3), pl.BlockSpec((nb, blk, blk), z3),
                   pl.BlockSpec((1, d_rnn), z2), pl.BlockSpec((1, d_rnn), z2), pl.BlockSpec((1, d_rnn), z2)]

    tt = 512
    ntt = seq // tt
    assert ms % tt == 0 and ms // tt <= ntt
    seq_rows = lambda b, t: (jnp.minimum(b * ntt + t, mp // tt - 1), 0)
    all_rows = lambda b, t: (jnp.minimum(b * ntt + t, m // tt - 1), 0)
    per_seq = lambda b, t: (jnp.minimum(b, n_prompt - 1), 0, 0)
    oa, hl_p, cn_p = pl.pallas_call(
        functools.partial(_lru_prompt_kernel, n_seq=n_prompt),
        grid=(n_prompt + 1, ntt),
        in_specs=[pl.BlockSpec((tt, d_rnn), seq_rows), pl.BlockSpec((tt, d_rnn), seq_rows)] + lru_w_specs,
        out_specs=[pl.BlockSpec((tt, d_rnn), all_rows), pl.BlockSpec((1, 1, d_rnn), per_seq),
                   pl.BlockSpec((1, CONV_W - 1, d_rnn), per_seq)],
        out_shape=[jax.ShapeDtypeStruct((m, d_rnn), bf16), jax.ShapeDtypeStruct((n_prompt, 1, d_rnn), f32),
                   jax.ShapeDtypeStruct((n_prompt, CONV_W - 1, d_rnn), f32)],
        scratch_shapes=[pltpu.VMEM((tt + 8, d_rnn), f32), pltpu.VMEM((tt, d_rnn), f32), pltpu.VMEM((tt, d_rnn), f32),
                        pltpu.VMEM((tt, d_rnn), f32), pltpu.VMEM((1, d_rnn), f32)],
        compiler_params=_cparams(("arbitrary", "arbitrary")),
        name="lru_prompt",
    )(xa, gg, *lru_w)

    bt = 32
    rows_s = bt * dec_seq
    s_off = mp // rows_s
    scp = jnp.pad(conv_s, ((0, 0), (dec_seq - (CONV_W - 1), 0), (0, 0)))
    samp_rows = lambda i: (s_off + i, 0)
    samp3 = lambda i: (i, 0, 0)
    oa, hl_s, cn_s = pl.pallas_call(
        _lru_sample_kernel,
        grid=(n_sample // bt,),
        in_specs=[pl.BlockSpec((rows_s, d_rnn), samp_rows), pl.BlockSpec((rows_s, d_rnn), samp_rows),
                  pl.BlockSpec((bt, dec_seq, d_rnn), samp3), pl.BlockSpec((bt, 1, d_rnn), samp3)] + lru_w_specs + [_ANY],
        out_specs=[pl.BlockSpec((rows_s, d_rnn), samp_rows), pl.BlockSpec((bt, 1, d_rnn), samp3),
                   pl.BlockSpec((bt, CONV_W - 1, d_rnn), samp3)],
        out_shape=[jax.ShapeDtypeStruct((m, d_rnn), bf16), jax.ShapeDtypeStruct((n_sample, 1, d_rnn), f32),
                   jax.ShapeDtypeStruct((n_sample, CONV_W - 1, d_rnn), f32)],
        input_output_aliases={11: 0},
        compiler_params=_cparams(("parallel",)),
        name="lru_sample",
    )(xa, gg, scp, h0_s.reshape(n_sample, 1, d_rnn), *lru_w, oa)

    rn = lp['ret_norm'].reshape(hh, 1, dv)
    ct = RET_CHUNK
    nct = seq // ct
    tabs = _decay_tables(ct, 1)
    full3 = lambda shape: pl.BlockSpec(shape, lambda *_: (0, 0, 0))
    tab_specs = lambda r: [full3((hh, 1, dv)), full3((hh, r, r)), full3((hh, r, 1)), full3((hh, r, 1)), full3((hh, 1, 1))]
    assert ms % ct == 0 and ms // ct <= nct
    chunk_row = lambda b, c: jnp.minimum(b * nct + c, mp // ct - 1)
    ob, so_p = pl.pallas_call(
        functools.partial(_ret_prompt_kernel, n_seq=n_prompt),
        grid=(n_prompt + 1, nct),
        in_specs=[pl.BlockSpec((ct, qk_dim), lambda b, c: (chunk_row(b, c), 0)),
                  pl.BlockSpec((ct, qk_dim), lambda b, c: (chunk_row(b, c), 1)),
                  pl.BlockSpec((ct, v_dim), lambda b, c: (chunk_row(b, c), 0)),
                  pl.BlockSpec((ct, v_dim), lambda b, c: (chunk_row(b, c), 0))] + tab_specs(ct),
        out_specs=[pl.BlockSpec((ct, v_dim), lambda b, c: (jnp.minimum(b * nct + c, m // ct - 1), 0)),
                   pl.BlockSpec((1, hh, dk, dv), lambda b, c: (jnp.minimum(b, n_prompt - 1), 0, 0, 0))],
        out_shape=[jax.ShapeDtypeStruct((m, v_dim), bf16), jax.ShapeDtypeStruct((n_prompt, hh, dk, dv), f32)],
        compiler_params=_cparams(("arbitrary", "arbitrary")),
        name="ret_prompt",
    )(qk, qk, v, sg, rn, *tabs)

    pr = SAMPLE_PAIR
    rows_r = pr * dec_seq
    r_off = mp // rows_r
    tabs = _decay_tables(dec_seq, pr)
    n_parts = 2
    hp = hh // n_parts
    n_units = (n_sample // pr) * n_parts
    tng = GATE_COL_TILE
    njg = 2 * d // tng
    nig = n_units // njg
    tmg = m // nig
    assert nig * njg == n_units and nig * tmg == m and tmg % 16 == 0 and njg % n_parts == 0
    unit = lambda i, j: i * njg + j
    seqs = lambda i, j: unit(i, j) // n_parts
    part = lambda i, j: unit(i, j) % n_parts
    qk_parts = qk_dim // (hp * dk)
    gates, ob, so_s = pl.pallas_call(
        functools.partial(_gates_ret_kernel, t_len=dec_seq),
        grid=(nig, njg),
        in_specs=[pl.BlockSpec((tmg, d), lambda i, j: (i, 0)),
                  pl.BlockSpec((d, tng), lambda i, j: (0, jnp.where(i == 0, j, njg - 1) + c_gate // tng)),
                  pl.BlockSpec((1, tng), lambda i, j: (0, j + c_gate // tng)),
                  pl.BlockSpec((rows_r, hp * dk), lambda i, j: (r_off + seqs(i, j), part(i, j))),
                  pl.BlockSpec((rows_r, hp * dk), lambda i, j: (r_off + seqs(i, j), qk_parts + part(i, j))),
                  pl.BlockSpec((rows_r, hp * dv), lambda i, j: (r_off + seqs(i, j), part(i, j))),
                  pl.BlockSpec((rows_r, hp * dv), lambda i, j: (r_off + seqs(i, j), part(i, j))),
                  pl.BlockSpec((pr, hp, dk, dv), lambda i, j: (seqs(i, j), part(i, j), 0, 0))]
                 + tab_specs(rows_r) + [_ANY],
        out_specs=[pl.BlockSpec((tmg, tng), lambda i, j: (i, j)),
                   pl.BlockSpec((rows_r, hp * dv), lambda i, j: (r_off + seqs(i, j), part(i, j))),
                   pl.BlockSpec((pr, hp, dk, dv), lambda i, j: (seqs(i, j), part(i, j), 0, 0))],
        out_shape=[jax.ShapeDtypeStruct((m, 2 * d), bf16), jax.ShapeDtypeStruct((m, v_dim), bf16),
                   jax.ShapeDtypeStruct((n_sample, hh, dk, dv), f32)],
        scratch_shapes=[pltpu.VMEM((njg, d, tng), bf16)],
        input_output_aliases={13: 1},
        compiler_params=_cparams(("arbitrary", "arbitrary")),
        name="gates_ret_sample",
    )(u, w_in, b_in, qk, qk, v, sg, ret_s, rn, *tabs, ob)

    tile = lambda j, i: (i, j)
    pa = _mm(oa, lp['proj_a'], col0=0, n_cols=d, out_dtype=f32, epilogue=lambda acc, rows, col_tile: acc,
             name="proj_a")
    tnb = 512
    merged = _mm(ob, lp['proj_b'], col0=0, n_cols=d, out_dtype=bf16, tn=tnb, resident=False,
                 epilogue=lambda acc, pa_ref, ga_ref, gb_ref, rows, col_tile: (
                     ga_ref[rows, :].astype(f32) * pa_ref[rows, :] + gb_ref[rows, :].astype(f32) * acc),
                 extras=[(pa, (tm, tnb), tile), (gates, (tm, tnb), tile),
                         (gates, (tm, tnb), lambda j, i: (i, j + d // tnb))], name="proj_b_merge")
    x2 = _mm(merged, lp['w_out'], col0=0, n_cols=d, out_dtype=f32,
             epilogue=lambda acc, x_ref, rows, col_tile: x_ref[rows, :] + acc,
             extras=[(x1, (tm, tn), tile)], name="w_out")

    tm2, tf2 = _ffn_tiles(m, d, ff, with_norm=False)
    x3 = _ffn([(x2, 0)], m, lp['ffn2_norm'], lp['ffn2_wg'], lp['ffn2_wu'], lp['ffn2_wd'], tm=tm2, tf=tf2)

    tp = 512
    ple_dim = pe_p.shape[1]
    fin = final_norm is not None
    fn = (final_norm if fin else lp['ple_norm']).reshape(1, d)
    npt = mp // tp
    c2 = lambda i: (0, 0)
    prow = lambda i: (jnp.minimum(i, npt - 1), 0)
    srow = lambda i: (jnp.maximum(i - npt, 0), 0)
    y_p, y_s = pl.pallas_call(
        functools.partial(_ple_kernel, n_prompt_tiles=npt, final=fin),
        grid=(m // tp,),
        in_specs=[pl.BlockSpec((tp, d), lambda i: (i, 0)), pl.BlockSpec((tp, ple_dim), prow),
                  pl.BlockSpec((tp, ple_dim), srow), pl.BlockSpec((1, d), c2), pl.BlockSpec((d, d), c2),
                  pl.BlockSpec((1, d), c2), pl.BlockSpec((ple_dim, d), c2), pl.BlockSpec((1, d), c2)],
        out_specs=[pl.BlockSpec((tp, d), prow), pl.BlockSpec((tp, d), srow)],
        out_shape=[jax.ShapeDtypeStruct((mp, d), f32), jax.ShapeDtypeStruct((ms, d), f32)],
        compiler_params=_cparams(("arbitrary",)),
        name="ple",
    )(x3, pe_p, pe_s, lp['ple_norm'].reshape(1, d), _to_bf16(lp['ple_wg']), lp['ple_bg'].reshape(1, d),
      _to_bf16(lp['ple_proj']), fn)

    states = (hl_p.reshape(n_prompt, d_rnn), cn_p, so_p, hl_s.reshape(n_sample, d_rnn), cn_s, so_s)
    return y_p, y_s, states


def kernel(x_prompt, x_sample, p_prompt, p_sample, state_lru, state_conv, state_ret, ffn1_norm, ffn1_wg, ffn1_wu, ffn1_wd, mix_norm, w_in, b_in, conv_w, conv_b, lru_wa, lru_ba, lru_wx, lru_bx, lru_lambda, ret_norm, proj_a, proj_b, w_out, ffn2_norm, ffn2_wg, ffn2_wu, ffn2_wd, ple_norm, ple_wg, ple_bg, ple_proj, final_norm):
    params = dict(ffn1_norm=ffn1_norm, ffn1_wg=ffn1_wg, ffn1_wu=ffn1_wu, ffn1_wd=ffn1_wd, mix_norm=mix_norm,
                  w_in=w_in, b_in=b_in, conv_w=conv_w, conv_b=conv_b, lru_wa=lru_wa, lru_ba=lru_ba, lru_wx=lru_wx,
                  lru_bx=lru_bx, lru_lambda=lru_lambda, ret_norm=ret_norm, proj_a=proj_a, proj_b=proj_b,
                  w_out=w_out, ffn2_norm=ffn2_norm, ffn2_wg=ffn2_wg, ffn2_wu=ffn2_wu, ffn2_wd=ffn2_wd,
                  ple_norm=ple_norm, ple_wg=ple_wg, ple_bg=ple_bg, ple_proj=ple_proj)
    depth = w_in.shape[0]
    n_prompt, seq, d = x_prompt.shape
    n_sample, dec_seq, _ = x_sample.shape
    mp, ms = n_prompt * seq, n_sample * dec_seq
    dk = d // N_RET_HEADS

    y_p = x_prompt.astype(f32).reshape(mp, d)
    y_s = x_sample.astype(f32).reshape(ms, d)
    cos_t, sin_t = _rope_table(seq, dec_seq, ms, dk // 2)
    cos_t, sin_t = (jnp.concatenate([jnp.tile(t[:seq], (n_prompt, 1)), t[seq:]], axis=0) for t in (cos_t, sin_t))
    outs = [[] for _ in range(6)]
    for i in range(depth):
        lp = {k: v[i].astype(f32) for k, v in params.items()}
        y_p, y_s, st = _layer([(y_p, 0), (y_s, mp)], p_prompt[i].astype(f32).reshape(mp, -1),
                              p_sample[i].astype(f32).reshape(ms, -1), n_prompt, seq, n_sample, dec_seq,
                              state_lru[i].astype(f32), state_conv[i].astype(f32), state_ret[i].astype(f32),
                              cos_t, sin_t, lp, final_norm.astype(f32) if i == depth - 1 else None)
        for o, s in zip(outs, st):
            o.append(s)
    y_prompt = y_p.reshape(n_prompt, seq, d).astype(x_prompt.dtype)
    y_sample = y_s.reshape(n_sample, dec_seq, d).astype(x_sample.dtype)
    lru_p, conv_p, ret_p, lru_s, conv_s, ret_s = (jnp.stack(o) for o in outs)
    return (y_prompt, y_sample, lru_p.astype(state_lru.dtype), conv_p.astype(state_conv.dtype),
            ret_p.astype(state_ret.dtype), lru_s.astype(state_lru.dtype), conv_s.astype(state_conv.dtype),
            ret_s.astype(state_ret.dtype))
```

```python
import functools

import jax
import jax.numpy as jnp
from jax import lax
from jax.experimental import pallas as pl
from jax.experimental.pallas import tpu as pltpu

f32 = jnp.float32
bf16 = jnp.bfloat16

N_LRU_BLOCKS = 8
CONV_W = 4
LRU_C = 8.0
N_RET_HEADS = 8
ROPE_BASE = 10000.0
EPS = 1e-6
PAST_LEN = 16384

RET_CHUNK = 256
SAMPLE_PAIR = 2
ROW_TILE = 1024
IN_ROW_TILE = 1536
GATE_COL_TILE = 256
S_RING = 3
VMEM_LIMIT = 58 * 1024 * 1024


def _cparams(sem):
    return pltpu.CompilerParams(dimension_semantics=sem, vmem_limit_bytes=VMEM_LIMIT)


def _sigmoid(x):
    return 1.0 / (1.0 + jnp.exp(-x))


def _rms(x, g):
    return x * lax.rsqrt(jnp.mean(x * x, axis=-1, keepdims=True) + EPS) * g


def _dot(a, b):
    return jnp.dot(a, b, preferred_element_type=f32)


def _dot_nt(a, b):
    return lax.dot_general(a, b, (((1,), (1,)), ((), ())), preferred_element_type=f32)


def _dot_tn(a, b):
    return lax.dot_general(a, b, (((0,), (0,)), ((), ())), preferred_element_type=f32)


_ANY = pl.BlockSpec(memory_space=pl.ANY)


def _cast_kernel(x_ref, o_ref):
    o_ref[...] = x_ref[...].astype(o_ref.dtype)


def _to_bf16(w):
    w2 = w.reshape(-1, w.shape[-1])
    return pl.pallas_call(_cast_kernel, out_shape=jax.ShapeDtypeStruct(w2.shape, bf16), name="to_bf16")(w2).reshape(w.shape)


def _rope_table_kernel(inv_ref, cos_ref, sin_ref, *, seq, dec_seq):
    rows = cos_ref.shape[0]
    r = lax.broadcasted_iota(jnp.int32, (rows, inv_ref.shape[1]), 0)
    pos = jnp.where(r < seq, r, PAST_LEN + lax.rem(r - seq, dec_seq))
    ang = pos.astype(f32) * inv_ref[...]
    cos_ref[...] = jnp.cos(ang)
    sin_ref[...] = jnp.sin(ang)


def _rope_table(seq, dec_seq, n_sample_rows, half):
    inv = (ROPE_BASE ** (-jnp.arange(half, dtype=f32) / half)).reshape(1, half)
    rows = seq + n_sample_rows
    return pl.pallas_call(
        functools.partial(_rope_table_kernel, seq=seq, dec_seq=dec_seq),
        out_shape=(jax.ShapeDtypeStruct((rows, half), f32), jax.ShapeDtypeStruct((rows, half), f32)),
        name="rope_table",
    )(inv)


def _ffn_kernel(*refs, group_rows, n_tiles, emit_norm):
    n_groups = len(group_rows)
    x_hbm = refs[:n_groups]
    g_ref, wg_ref, wu_ref, wd_ref = refs[n_groups:n_groups + 4]
    rest = list(refs[n_groups + 4:])
    g2_ref = rest.pop(0) if emit_norm else None
    xo_ref = rest.pop(0)
    u2_ref = rest.pop(0) if emit_norm else None
    (u_scr,) = rest
    i = pl.program_id(0)
    f = pl.program_id(1)
    tm = xo_ref.shape[0]

    @pl.when(f == 0)
    def _():
        for t in range(n_tiles):
            @pl.when(i == t)
            def _(t=t):
                for gi, (r0, nr) in enumerate(group_rows):
                    lo, hi = max(t * tm, r0), min((t + 1) * tm, r0 + nr)
                    if lo < hi:
                        pltpu.sync_copy(x_hbm[gi].at[lo - r0:hi - r0, :], xo_ref.at[lo - t * tm:hi - t * tm, :])

        u_scr[...] = _rms(xo_ref[...], g_ref[...]).astype(bf16)

    u = u_scr[...]
    hs = []
    for c0 in range(0, wg_ref.shape[1], FFN_SUB_TILE):
        cols = slice(c0, c0 + FFN_SUB_TILE)
        hg = _dot(u, wg_ref[:, cols].astype(bf16))
        hu = _dot(u, wu_ref[:, cols].astype(bf16))
        hs.append((0.5 * hg * _sigmoid(hg) * hu).astype(bf16))
    h = hs[0] if len(hs) == 1 else jnp.concatenate(hs, axis=1)
    xo_ref[...] += _dot(h, wd_ref[...].astype(bf16))

    if emit_norm:
        @pl.when(f == pl.num_programs(1) - 1)
        def _():
            u2_ref[...] = _rms(xo_ref[...], g2_ref[...]).astype(bf16)


FFN_SUB_TILE = 256
FFN_VMEM_BUDGET = 60 * 1024 * 1024


def _ffn_tiles(m, d, ff, with_norm):
    sub = FFN_SUB_TILE
    for tf in (4 * sub, 2 * sub, sub):
        if ff % tf:
            continue
        for tm in range(m, 1023, -256):
            if m % tm == 0:
                rows = tm * d * (4 + 2 + (2 if with_norm else 0))
                weights = 3 * d * tf * 4 * 2
                temps = 2 * tm * sub * 4 + tm * tf * 2 + (2 * d * sub + tf * d) * 2
                if rows + weights + temps <= FFN_VMEM_BUDGET:
                    return tm, tf
    raise ValueError("no FFN tiling fits VMEM")


def _ffn(parts, m_total, g, wg, wu, wd, g2=None, *, tm, tf):
    d = wg.shape[0]
    ff = wg.shape[1]
    emit_norm = g2 is not None
    row = lambda i, f: (i, 0)
    out_mode = dict(pipeline_mode=pl.Buffered(1))
    in_specs = [_ANY] * len(parts) + [
        pl.BlockSpec((1, d), lambda i, f: (0, 0)),
        pl.BlockSpec((d, tf), lambda i, f: (0, f)),
        pl.BlockSpec((d, tf), lambda i, f: (0, f)),
        pl.BlockSpec((tf, d), lambda i, f: (f, 0)),
    ]
    args = [xp for xp, _ in parts] + [g.reshape(1, d), wg, wu, wd]
    out_shape = [jax.ShapeDtypeStruct((m_total, d), f32)]
    out_specs = [pl.BlockSpec((tm, d), row, **out_mode)]
    if emit_norm:
        in_specs.append(pl.BlockSpec((1, d), lambda i, f: (0, 0)))
        args.append(g2.reshape(1, d))
        out_shape.append(jax.ShapeDtypeStruct((m_total, d), bf16))
        out_specs.append(pl.BlockSpec((tm, d), row, **out_mode))
    outs = pl.pallas_call(
        functools.partial(_ffn_kernel, group_rows=tuple((r0, xp.shape[0]) for xp, r0 in parts),
                          n_tiles=m_total // tm, emit_norm=emit_norm),
        grid=(m_total // tm, ff // tf),
        in_specs=in_specs,
        out_specs=out_specs,
        out_shape=out_shape,
        scratch_shapes=[pltpu.VMEM((tm, d), bf16)],
        compiler_params=_cparams(("arbitrary", "arbitrary")),
        name="ffn",
    )(*args)
    return outs if emit_norm else outs[0]


MM_ROW_SPLIT = 2


def _mm_kernel(a_ref, b_ref, *rest, epilogue, col_axis):
    *extra, o_ref, bw_scr = rest
    j = pl.program_id(col_axis)
    slot = j if col_axis == 1 else 0

    @pl.when(pl.program_id(1 - col_axis) == 0)
    def _():
        bw_scr[slot] = b_ref[...].astype(bf16)

    rows = a_ref.shape[0] // MM_ROW_SPLIT
    for r in range(MM_ROW_SPLIT):
        rs = slice(r * rows, (r + 1) * rows)
        o_ref[rs, :] = epilogue(_dot(a_ref[rs, :], bw_scr[slot]), *extra, rows=rs, col_tile=j).astype(o_ref.dtype)


def _mm(a, b, *, col0, n_cols, out_dtype, epilogue, extras=(), tm=ROW_TILE, tn=1024, resident=True, name="mm"):
    m, k = a.shape
    off = col0 // tn
    nj = n_cols // tn
    if resident:
        grid, order = (m // tm, nj), (lambda f: (lambda i, j: f(j, i)))
        b_map = lambda j, i: (0, jnp.where(i == 0, j, nj - 1) + off)
    else:
        grid, order = (nj, m // tm), (lambda f: f)
        b_map = lambda j, i: (0, j + off)
    in_specs = [pl.BlockSpec((tm, k), order(lambda j, i: (i, 0))), pl.BlockSpec((k, tn), order(b_map))]
    args = [a, b]
    for arr, blk, imap in extras:
        in_specs.append(pl.BlockSpec(blk, order(imap)))
        args.append(arr)
    return pl.pallas_call(
        functools.partial(_mm_kernel, epilogue=epilogue, col_axis=1 if resident else 0),
        grid=grid,
        in_specs=in_specs,
        out_specs=pl.BlockSpec((tm, tn), order(lambda j, i: (i, j))),
        out_shape=jax.ShapeDtypeStruct((m, n_cols), out_dtype),
        scratch_shapes=[pltpu.VMEM((nj if resident else 1, k, tn), bf16)],
        compiler_params=_cparams(("arbitrary", "arbitrary")),
        name=name,
    )(*args)


def _gelu_tanh(x):
    return 0.5 * x * (1.0 + jnp.tanh(0.7978845608028654 * (x + 0.044715 * (x * x * x))))


def _rope_epilogue(acc, bias_ref, cos_ref, sin_ref, *, rows, col_tile, k_tile0, k_scale, head_dim):
    z = acc + bias_ref[...]
    cos = cos_ref[rows, :]
    sin = sin_ref[rows, :]
    half = head_dim // 2
    parts = []
    for h0 in range(0, z.shape[1], head_dim):
        x1 = z[:, h0:h0 + half]
        x2 = z[:, h0 + half:h0 + head_dim]
        parts += [x1 * cos - x2 * sin, x2 * cos + x1 * sin]
    scale = jnp.where(col_tile >= k_tile0, k_scale, 1.0).astype(f32)
    return jnp.concatenate(parts, axis=1) * scale


def _lru_gates(xc, n, cs, wa_ref, wx_ref, ba_ref, bx_ref, lam_ref):
    xcb = xc.astype(bf16)
    r = _sigmoid(_dot(xcb, wa_ref[n].astype(bf16)) + ba_ref[:, cs])
    gi = _sigmoid(_dot(xcb, wx_ref[n].astype(bf16)) + bx_ref[:, cs])
    log_a = -LRU_C * r * jax.nn.softplus(-lam_ref[:, cs])
    a = jnp.exp(log_a)
    a2 = a * a
    one_minus = jnp.where(log_a < -0.25, 1.0 - a2, -jnp.tanh(log_a) * (a2 + 1.0))
    return a, one_minus * lax.rsqrt(jnp.maximum(one_minus, 1e-36)), gi


def _lru_prompt_kernel(*refs, n_seq):
    oa_ref = refs[9]

    @pl.when(pl.program_id(0) >= n_seq)
    def _():
        oa_ref[...] = jnp.zeros_like(oa_ref)

    @pl.when(pl.program_id(0) < n_seq)
    def _():
        _lru_prompt_body(*refs)


def _lru_prompt_body(xa_ref, gg_ref, cw_ref, cb_ref, wa_ref, wx_ref, ba_ref, bx_ref, lam_ref,
                     oa_ref, hl_ref, cn_ref, xe_scr, a_scr, b_scr, h_scr, hc_scr):
    tt, c = xa_ref.shape
    blk = c // N_LRU_BLOCKS
    hist = 8
    t = pl.program_id(1)

    @pl.when(t == 0)
    def _():
        xe_scr[0:hist, :] = jnp.zeros((hist, c), f32)
        hc_scr[...] = jnp.zeros_like(hc_scr)

    xe_scr[hist:hist + tt, :] = xa_ref[...]
    pos = lax.broadcasted_iota(jnp.int32, (tt, 1), 0) + t * tt
    for n in range(N_LRU_BLOCKS):
        cs = slice(n * blk, (n + 1) * blk)
        ext = xe_scr[:, cs]
        xc = cb_ref[:, cs] + cw_ref[CONV_W - 1:CONV_W, cs] * ext[hist:, :]
        for s in range(1, CONV_W):
            xc = xc + cw_ref[CONV_W - 1 - s:CONV_W - s, cs] * pltpu.roll(ext, s, axis=0)[hist:, :]
        a, mult, gi = _lru_gates(xc, n, cs, wa_ref, wx_ref, ba_ref, bx_ref, lam_ref)
        mult = jnp.where(pos == 0, 1.0, mult)
        a_scr[:, cs] = a
        b_scr[:, cs] = mult * (gi * xc)

    def step(i, h):
        h = a_scr[pl.ds(i, 1), :] * h + b_scr[pl.ds(i, 1), :]
        h_scr[pl.ds(i, 1), :] = h
        return h

    h = lax.fori_loop(0, tt, step, hc_scr[...], unroll=8)
    hc_scr[...] = h
    hl_ref[0] = h
    for n in range(N_LRU_BLOCKS):
        cs = slice(n * blk, (n + 1) * blk)
        oa_ref[:, cs] = (h_scr[:, cs] * gg_ref[:, cs].astype(f32)).astype(bf16)
    cn_ref[0] = xe_scr[pl.ds(hist + tt - (CONV_W - 1), CONV_W - 1), :]
    xe_scr[0:hist, :] = xe_scr[pl.ds(tt, hist), :]


def _lru_sample_kernel(xa_ref, gg_ref, scp_ref, h0_ref, cw_ref, cb_ref, wa_ref, wx_ref, ba_ref, bx_ref, lam_ref,
                       oa_prev_ref, oa_ref, hl_ref, cn_ref):
    del oa_prev_ref
    rows, c = xa_ref.shape
    bt, t_len = scp_ref.shape[0], scp_ref.shape[1]
    blk = c // N_LRU_BLOCKS
    tpos = lax.broadcasted_iota(jnp.int32, (bt, t_len, blk), 1)
    for n in range(N_LRU_BLOCKS):
        cs = slice(n * blk, (n + 1) * blk)
        x3 = xa_ref[:, cs].reshape(bt, t_len, blk)
        ext = jnp.concatenate([scp_ref[:, :, cs], x3], axis=1)
        xc3 = cb_ref[:, cs] + cw_ref[CONV_W - 1:CONV_W, cs] * x3
        for s in range(1, CONV_W):
            xc3 = xc3 + cw_ref[CONV_W - 1 - s:CONV_W - s, cs] * pltpu.roll(ext, s, axis=1)[:, t_len:, :]
        xc = xc3.reshape(rows, blk)
        a, mult, gi = _lru_gates(xc, n, cs, wa_ref, wx_ref, ba_ref, bx_ref, lam_ref)
        a3 = a.reshape(bt, t_len, blk)
        b3 = (mult * (gi * xc)).reshape(bt, t_len, blk)
        d = 1
        while d < t_len:
            keep = tpos >= d
            b3 = jnp.where(keep, a3 * pltpu.roll(b3, d, axis=1) + b3, b3)
            a3 = jnp.where(keep, a3 * pltpu.roll(a3, d, axis=1), a3)
            d *= 2
        h3 = b3 + a3 * h0_ref[:, :, cs]
        oa_ref[:, cs] = (h3.reshape(rows, blk) * gg_ref[:, cs].astype(f32)).astype(bf16)
        hl_ref[:, :, cs] = h3[:, t_len - 1:t_len, :]
        cn_ref[:, :, cs] = pltpu.roll(x3, CONV_W - 1, axis=1)[:, 0:CONV_W - 1, :]


def _ret_norm_gate(o, rn, sg):
    return (o * lax.rsqrt(jnp.mean(o * o, axis=-1, keepdims=True) + EPS) * rn * sg.astype(f32)).astype(bf16)


def _ret_prompt_kernel(*refs, n_seq):
    ob_ref = refs[9]

    @pl.when(pl.program_id(0) >= n_seq)
    def _():
        ob_ref[...] = jnp.zeros_like(ob_ref)

    @pl.when(pl.program_id(0) < n_seq)
    def _():
        _ret_prompt_body(*refs)


def _ret_prompt_body(q_ref, k_ref, v_ref, sg_ref, rn_ref, dm_ref, cd_ref, sd_ref, chd_ref, ob_ref, so_ref):
    c = pl.program_id(1)
    dk, dv = so_ref.shape[2], so_ref.shape[3]

    @pl.when(c == 0)
    def _():
        so_ref[...] = jnp.zeros_like(so_ref)

    for h in range(N_RET_HEADS):
        ks = slice(h * dk, (h + 1) * dk)
        vs = slice(h * dv, (h + 1) * dv)
        q = q_ref[:, ks]
        k = k_ref[:, ks]
        v = v_ref[:, vs]
        s_old = so_ref[0, h]
        scores = _dot_nt(q, k) * dm_ref[h]
        inner = _dot(scores.astype(bf16), v)
        cross = _dot(q, s_old.astype(bf16)) * cd_ref[h]
        kd = (k.astype(f32) * sd_ref[h]).astype(bf16)
        so_ref[0, h] = s_old * chd_ref[h] + _dot_tn(kd, v)
        ob_ref[:, vs] = _ret_norm_gate(inner + cross, rn_ref[h], sg_ref[:, vs])


def _ret_sample_body(q_ref, k_ref, v_ref, sg_ref, s_ref, rn_ref, dm_ref, cd_ref, sd_ref, chd_ref, ob_ref, so_ref,
                     *, t_len, head0):
    rows = q_ref.shape[0]
    n_heads, dk, dv = so_ref.shape[1], so_ref.shape[2], so_ref.shape[3]
    row = lax.broadcasted_iota(jnp.int32, (rows, 1), 0)
    for hl in range(n_heads):
        h = head0 + hl
        ks = slice(hl * dk, (hl + 1) * dk)
        vs = slice(hl * dv, (hl + 1) * dv)
        q = q_ref[:, ks]
        k = k_ref[:, ks]
        v = v_ref[:, vs]
        scores = _dot_nt(q, k) * dm_ref[h]
        o = _dot(scores.astype(bf16), v)
        kd = k.astype(f32) * sd_ref[h]
        for bb in range(rows // t_len):
            mine = (row >= bb * t_len) & (row < (bb + 1) * t_len)
            s_old = s_ref[bb, hl]
            cross = _dot(q, s_old.astype(bf16)) * cd_ref[h]
            o = o + jnp.where(mine, cross, 0.0)
            so_ref[bb, hl] = s_old * chd_ref[h] + _dot_tn(jnp.where(mine, kd, 0.0).astype(bf16), v)
        ob_ref[:, vs] = _ret_norm_gate(o, rn_ref[h], sg_ref[:, vs])


def _gates_ret_kernel(a_ref, b_ref, bias_ref, q_ref, k_ref, v_ref, sg_ref, s_hbm, rn_ref, dm_ref, cd_ref, sd_ref,
                      chd_ref, ob_prev_ref, g_ref, ob_ref, so_ref, bw_scr, s_buf, s_sem, *, t_len, n_units):
    del ob_prev_ref
    i, j = pl.program_id(0), pl.program_id(1)
    n_seq, n_heads = s_buf.shape[1], s_buf.shape[2]
    n_parts = N_RET_HEADS // n_heads
    unit = i * pl.num_programs(1) + j

    def state_copy(u):
        slot = lax.rem(u, S_RING)
        src = s_hbm.at[pl.ds((u // n_parts) * n_seq, n_seq), pl.ds(lax.rem(u, n_parts) * n_heads, n_heads)]
        return pltpu.make_async_copy(src, s_buf.at[slot], s_sem.at[slot])

    @pl.when(unit == 0)
    def _():
        for u0 in range(S_RING - 1):
            state_copy(jnp.int32(u0)).start()

    @pl.when(unit + (S_RING - 1) < n_units)
    def _():
        state_copy(unit + (S_RING - 1)).start()

    @pl.when(i == 0)
    def _():
        bw_scr[j] = b_ref[...].astype(bf16)

    rows = a_ref.shape[0] // MM_ROW_SPLIT
    for r in range(MM_ROW_SPLIT):
        rs = slice(r * rows, (r + 1) * rows)
        g_ref[rs, :] = _sigmoid(_dot(a_ref[rs, :], bw_scr[j]) + bias_ref[...]).astype(g_ref.dtype)

    state_copy(unit).wait()
    _ret_sample_body(q_ref, k_ref, v_ref, sg_ref, s_buf.at[lax.rem(unit, S_RING)], rn_ref, dm_ref, cd_ref, sd_ref,
                     chd_ref, ob_ref, so_ref, t_len=t_len, head0=lax.rem(unit, n_parts) * n_heads)


def _decay_tables(chunk, n_seq):
    log_g = jnp.log1p(-jnp.exp2(-5.0 - jnp.arange(N_RET_HEADS, dtype=f32)))
    r = jnp.arange(chunk * n_seq)
    idx = (r % chunk).astype(f32)
    same = (r[:, None] // chunk) == (r[None, :] // chunk)
    diff = idx[:, None] - idx[None, :]
    dmask = jnp.where(same & (diff >= 0), jnp.exp(jnp.maximum(diff, 0.0)[None] * log_g[:, None, None]), 0.0)
    cross_decay = jnp.exp((idx[None] + 1.0) * log_g[:, None])[..., None]
    state_decay = jnp.exp((chunk - 1.0 - idx[None]) * log_g[:, None])[..., None]
    chunk_decay = jnp.exp(chunk * log_g)[:, None, None]
    return dmask, cross_decay, state_decay, chunk_decay


def _ple_kernel(x_ref, pep_ref, pes_ref, gn_ref, wg_ref, bg_ref, pp_ref, fn_ref, yp_ref, ys_ref, *, n_prompt_tiles, final):
    i = pl.program_id(0)
    x = x_ref[...]
    u = _rms(x, gn_ref[...]).astype(bf16)
    gate = _sigmoid(_dot(u, wg_ref[...]) + bg_ref[...])
    pe = jnp.where(i < n_prompt_tiles, pep_ref[...], pes_ref[...]).astype(bf16)
    x = x + gate * _dot(pe, pp_ref[...].astype(bf16))
    y = _rms(x, fn_ref[...]) if final else x

    @pl.when(i < n_prompt_tiles)
    def _():
        yp_ref[...] = y

    @pl.when(i >= n_prompt_tiles)
    def _():
        ys_ref[...] = y


def _layer(x_parts, pe_p, pe_s, n_prompt, seq, n_sample, dec_seq, h0_s, conv_s, ret_s, cos_t, sin_t, lp, final_norm):
    d = lp['mix_norm'].shape[0]
    mp, ms = n_prompt * seq, n_sample * dec_seq
    m = mp + ms
    d_rnn = lp['conv_w'].shape[1]
    hh = N_RET_HEADS
    dk = d // hh
    dv = lp['ret_norm'].shape[1]
    qk_dim, v_dim = hh * dk, hh * dv
    tm, tn = ROW_TILE, 1024

    ff = lp['ffn1_wg'].shape[1]
    tm1, tf1 = _ffn_tiles(m, d, ff, with_norm=True)
    x1, u = _ffn(x_parts, m, lp['ffn1_norm'], lp['ffn1_wg'], lp['ffn1_wu'], lp['ffn1_wd'], lp['mix_norm'],
                 tm=tm1, tf=tf1)

    w_in = lp['w_in']
    b_in = lp['b_in'].reshape(1, -1)

    def bias_extra(col0):
        return (b_in, (1, tn), lambda j, i, o=col0 // tn: (0, j + o))

    def with_bias(fn):
        return lambda acc, b_ref, rows, col_tile: fn(acc + b_ref[...])

    c_xa, c_ga, c_q, c_v = 0, d_rnn, 2 * d_rnn, 2 * d_rnn + 2 * qk_dim
    c_gr, c_gate = c_v + v_dim, c_v + 2 * v_dim
    tmi = IN_ROW_TILE if m % IN_ROW_TILE == 0 else tm
    proj = functools.partial(_mm, u, w_in, tm=tmi, tn=tn)
    xa = proj(col0=c_xa, n_cols=d_rnn, out_dtype=f32, epilogue=with_bias(lambda z: z),
              extras=[bias_extra(c_xa)], name="inproj_xa")
    gg = proj(col0=c_ga, n_cols=d_rnn, out_dtype=bf16, epilogue=with_bias(_gelu_tanh),
              extras=[bias_extra(c_ga)], name="inproj_ga")
    half = dk // 2
    rope_map = lambda j, i: (i, 0)
    qk = proj(col0=c_q, n_cols=2 * qk_dim, out_dtype=bf16,
              epilogue=functools.partial(_rope_epilogue, k_tile0=qk_dim // tn, k_scale=dk ** -0.5, head_dim=dk),
              extras=[bias_extra(c_q), (cos_t, (tmi, half), rope_map), (sin_t, (tmi, half), rope_map)],
              name="inproj_qk")
    v = proj(col0=c_v, n_cols=v_dim, out_dtype=bf16, epilogue=with_bias(lambda z: z),
             extras=[bias_extra(c_v)], name="inproj_v")
    sg = proj(col0=c_gr, n_cols=v_dim, out_dtype=bf16, epilogue=with_bias(lambda z: z * _sigmoid(z)),
              extras=[bias_extra(c_gr)], name="inproj_gr")

    row2 = lambda a: a.reshape(1, -1)
    lru_w = [lp['conv_w'], row2(lp['conv_b']), lp['lru_wa'], lp['lru_wx'], row2(lp['lru_ba']), row2(lp['lru_bx']),
             row2(lp['lru_lambda'])]
    nb = N_LRU_BLOCKS
    blk = d_rnn // nb
    z2 = lambda *_: (0, 0)
    z3 = lambda *_: (0, 0, 0)
    lru_w_specs = [pl.BlockSpec((CONV_W, d_rnn), z2), pl.BlockSpec((1, d_rnn), z2),
                   pl.BlockSpec((nb, blk, blk), z3), pl.BlockSpec((nb, blk, blk), z3),
                   pl.BlockSpec((1, d_rnn), z2), pl.BlockSpec((1, d_rnn), z2), pl.BlockSpec((1, d_rnn), z2)]

    tt = 512
    ntt = seq // tt
    assert ms % tt == 0 and ms // tt <= ntt
    seq_rows = lambda b, t: (jnp.minimum(b * ntt + t, mp // tt - 1), 0)
    all_rows = lambda b, t: (jnp.minimum(b * ntt + t, m // tt - 1), 0)
    per_seq = lambda b, t: (jnp.minimum(b, n_prompt - 1), 0, 0)
    oa, hl_p, cn_p = pl.pallas_call(
        functools.partial(_lru_prompt_kernel, n_seq=n_prompt),
        grid=(n_prompt + 1, ntt),
        in_specs=[pl.BlockSpec((tt, d_rnn), seq_rows), pl.BlockSpec((tt, d_rnn), seq_rows)] + lru_w_specs,
        out_specs=[pl.BlockSpec((tt, d_rnn), all_rows), pl.BlockSpec((1, 1, d_rnn), per_seq),
                   pl.BlockSpec((1, CONV_W - 1, d_rnn), per_seq)],
        out_shape=[jax.ShapeDtypeStruct((m, d_rnn), bf16), jax.ShapeDtypeStruct((n_prompt, 1, d_rnn), f32),
                   jax.ShapeDtypeStruct((n_prompt, CONV_W - 1, d_rnn), f32)],
        scratch_shapes=[pltpu.VMEM((tt + 8, d_rnn), f32), pltpu.VMEM((tt, d_rnn), f32), pltpu.VMEM((tt, d_rnn), f32),
                        pltpu.VMEM((tt, d_rnn), f32), pltpu.VMEM((1, d_rnn), f32)],
        compiler_params=_cparams(("arbitrary", "arbitrary")),
        name="lru_prompt",
    )(xa, gg, *lru_w)

    bt = 32
    rows_s = bt * dec_seq
    s_off = mp // rows_s
    scp = jnp.pad(conv_s, ((0, 0), (dec_seq - (CONV_W - 1), 0), (0, 0)))
    samp_rows = lambda i: (s_off + i, 0)
    samp3 = lambda i: (i, 0, 0)
    oa, hl_s, cn_s = pl.pallas_call(
        _lru_sample_kernel,
        grid=(n_sample // bt,),
        in_specs=[pl.BlockSpec((rows_s, d_rnn), samp_rows), pl.BlockSpec((rows_s, d_rnn), samp_rows),
                  pl.BlockSpec((bt, dec_seq, d_rnn), samp3), pl.BlockSpec((bt, 1, d_rnn), samp3)] + lru_w_specs + [_ANY],
        out_specs=[pl.BlockSpec((rows_s, d_rnn), samp_rows), pl.BlockSpec((bt, 1, d_rnn), samp3),
                   pl.BlockSpec((bt, CONV_W - 1, d_rnn), samp3)],
        out_shape=[jax.ShapeDtypeStruct((m, d_rnn), bf16), jax.ShapeDtypeStruct((n_sample, 1, d_rnn), f32),
                   jax.ShapeDtypeStruct((n_sample, CONV_W - 1, d_rnn), f32)],
        input_output_aliases={11: 0},
        compiler_params=_cparams(("parallel",)),
        name="lru_sample",
    )(xa, gg, scp, h0_s.reshape(n_sample, 1, d_rnn), *lru_w, oa)

    rn = lp['ret_norm'].reshape(hh, 1, dv)
    ct = RET_CHUNK
    nct = seq // ct
    tabs = _decay_tables(ct, 1)
    full3 = lambda shape: pl.BlockSpec(shape, lambda *_: (0, 0, 0))
    tab_specs = lambda r: [full3((hh, 1, dv)), full3((hh, r, r)), full3((hh, r, 1)), full3((hh, r, 1)), full3((hh, 1, 1))]
    assert ms % ct == 0 and ms // ct <= nct
    chunk_row = lambda b, c: jnp.minimum(b * nct + c, mp // ct - 1)
    ob, so_p = pl.pallas_call(
        functools.partial(_ret_prompt_kernel, n_seq=n_prompt),
        grid=(n_prompt + 1, nct),
        in_specs=[pl.BlockSpec((ct, qk_dim), lambda b, c: (chunk_row(b, c), 0)),
                  pl.BlockSpec((ct, qk_dim), lambda b, c: (chunk_row(b, c), 1)),
                  pl.BlockSpec((ct, v_dim), lambda b, c: (chunk_row(b, c), 0)),
                  pl.BlockSpec((ct, v_dim), lambda b, c: (chunk_row(b, c), 0))] + tab_specs(ct),
        out_specs=[pl.BlockSpec((ct, v_dim), lambda b, c: (jnp.minimum(b * nct + c, m // ct - 1), 0)),
                   pl.BlockSpec((1, hh, dk, dv), lambda b, c: (jnp.minimum(b, n_prompt - 1), 0, 0, 0))],
        out_shape=[jax.ShapeDtypeStruct((m, v_dim), bf16), jax.ShapeDtypeStruct((n_prompt, hh, dk, dv), f32)],
        compiler_params=_cparams(("arbitrary", "arbitrary")),
        name="ret_prompt",
    )(qk, qk, v, sg, rn, *tabs)

    pr = SAMPLE_PAIR
    rows_r = pr * dec_seq
    r_off = mp // rows_r
    tabs = _decay_tables(dec_seq, pr)
    n_parts = 2
    hp = hh // n_parts
    n_units = (n_sample // pr) * n_parts
    tng = GATE_COL_TILE
    njg = 2 * d // tng
    nig = n_units // njg
    tmg = m // nig
    assert nig * njg == n_units >= S_RING and nig * tmg == m and tmg % 16 == 0 and njg % n_parts == 0
    unit = lambda i, j: i * njg + j
    seqs = lambda i, j: unit(i, j) // n_parts
    part = lambda i, j: unit(i, j) % n_parts
    qk_parts = qk_dim // (hp * dk)
    gates, ob, so_s = pl.pallas_call(
        functools.partial(_gates_ret_kernel, t_len=dec_seq, n_units=n_units),
        grid=(nig, njg),
        in_specs=[pl.BlockSpec((tmg, d), lambda i, j: (i, 0)),
                  pl.BlockSpec((d, tng), lambda i, j: (0, jnp.where(i == 0, j, njg - 1) + c_gate // tng)),
                  pl.BlockSpec((1, tng), lambda i, j: (0, j + c_gate // tng)),
                  pl.BlockSpec((rows_r, hp * dk), lambda i, j: (r_off + seqs(i, j), part(i, j))),
                  pl.BlockSpec((rows_r, hp * dk), lambda i, j: (r_off + seqs(i, j), qk_parts + part(i, j))),
                  pl.BlockSpec((rows_r, hp * dv), lambda i, j: (r_off + seqs(i, j), part(i, j))),
                  pl.BlockSpec((rows_r, hp * dv), lambda i, j: (r_off + seqs(i, j), part(i, j))),
                  _ANY] + tab_specs(rows_r) + [_ANY],
        out_specs=[pl.BlockSpec((tmg, tng), lambda i, j: (i, j)),
                   pl.BlockSpec((rows_r, hp * dv), lambda i, j: (r_off + seqs(i, j), part(i, j))),
                   pl.BlockSpec((pr, hp, dk, dv), lambda i, j: (seqs(i, j), part(i, j), 0, 0))],
        out_shape=[jax.ShapeDtypeStruct((m, 2 * d), bf16), jax.ShapeDtypeStruct((m, v_dim), bf16),
                   jax.ShapeDtypeStruct((n_sample, hh, dk, dv), f32)],
        scratch_shapes=[pltpu.VMEM((njg, d, tng), bf16), pltpu.VMEM((S_RING, pr, hp, dk, dv), f32),
                        pltpu.SemaphoreType.DMA((S_RING,))],
        input_output_aliases={13: 1},
        compiler_params=_cparams(("arbitrary", "arbitrary")),
        name="gates_ret_sample",
    )(u, w_in, b_in, qk, qk, v, sg, ret_s, rn, *tabs, ob)

    tile = lambda j, i: (i, j)
    pa = _mm(oa, lp['proj_a'], col0=0, n_cols=d, out_dtype=f32, epilogue=lambda acc, rows, col_tile: acc,
             name="proj_a")
    tnb = 512
    merged = _mm(ob, lp['proj_b'], col0=0, n_cols=d, out_dtype=bf16, tn=tnb, resident=False,
                 epilogue=lambda acc, pa_ref, ga_ref, gb_ref, rows, col_tile: (
                     ga_ref[rows, :].astype(f32) * pa_ref[rows, :] + gb_ref[rows, :].astype(f32) * acc),
                 extras=[(pa, (tm, tnb), tile), (gates, (tm, tnb), tile),
                         (gates, (tm, tnb), lambda j, i: (i, j + d // tnb))], name="proj_b_merge")
    x2 = _mm(merged, lp['w_out'], col0=0, n_cols=d, out_dtype=f32,
             epilogue=lambda acc, x_ref, rows, col_tile: x_ref[rows, :] + acc,
             extras=[(x1, (tm, tn), tile)], name="w_out")

    tm2, tf2 = _ffn_tiles(m, d, ff, with_norm=False)
    x3 = _ffn([(x2, 0)], m, lp['ffn2_norm'], lp['ffn2_wg'], lp['ffn2_wu'], lp['ffn2_wd'], tm=tm2, tf=tf2)

    tp = 512
    ple_dim = pe_p.shape[1]
    fin = final_norm is not None
    fn = (final_norm if fin else lp['ple_norm']).reshape(1, d)
    npt = mp // tp
    c2 = lambda i: (0, 0)
    prow = lambda i: (jnp.minimum(i, npt - 1), 0)
    srow = lambda i: (jnp.maximum(i - npt, 0), 0)
    y_p, y_s = pl.pallas_call(
        functools.partial(_ple_kernel, n_prompt_tiles=npt, final=fin),
        grid=(m // tp,),
        in_specs=[pl.BlockSpec((tp, d), lambda i: (i, 0)), pl.BlockSpec((tp, ple_dim), prow),
                  pl.BlockSpec((tp, ple_dim), srow), pl.BlockSpec((1, d), c2), pl.BlockSpec((d, d), c2),
                  pl.BlockSpec((1, d), c2), pl.BlockSpec((ple_dim, d), c2), pl.BlockSpec((1, d), c2)],
        out_specs=[pl.BlockSpec((tp, d), prow), pl.BlockSpec((tp, d), srow)],
        out_shape=[jax.ShapeDtypeStruct((mp, d), f32), jax.ShapeDtypeStruct((ms, d), f32)],
        compiler_params=_cparams(("arbitrary",)),
        name="ple",
    )(x3, pe_p, pe_s, lp['ple_norm'].reshape(1, d), _to_bf16(lp['ple_wg']), lp['ple_bg'].reshape(1, d),
      lp['ple_proj'], fn)

    states = (hl_p.reshape(n_prompt, d_rnn), cn_p, so_p, hl_s.reshape(n_sample, d_rnn), cn_s, so_s)
    return y_p, y_s, states


def kernel(x_prompt, x_sample, p_prompt, p_sample, state_lru, state_conv, state_ret, ffn1_norm, ffn1_wg, ffn1_wu, ffn1_wd, mix_norm, w_in, b_in, conv_w, conv_b, lru_wa, lru_ba, lru_wx, lru_bx, lru_lambda, ret_norm, proj_a, proj_b, w_out, ffn2_norm, ffn2_wg, ffn2_wu, ffn2_wd, ple_norm, ple_wg, ple_bg, ple_proj, final_norm):
    params = dict(ffn1_norm=ffn1_norm, ffn1_wg=ffn1_wg, ffn1_wu=ffn1_wu, ffn1_wd=ffn1_wd, mix_norm=mix_norm,
                  w_in=w_in, b_in=b_in, conv_w=conv_w, conv_b=conv_b, lru_wa=lru_wa, lru_ba=lru_ba, lru_wx=lru_wx,
                  lru_bx=lru_bx, lru_lambda=lru_lambda, ret_norm=ret_norm, proj_a=proj_a, proj_b=proj_b,
                  w_out=w_out, ffn2_norm=ffn2_norm, ffn2_wg=ffn2_wg, ffn2_wu=ffn2_wu, ffn2_wd=ffn2_wd,
                  ple_norm=ple_norm, ple_wg=ple_wg, ple_bg=ple_bg, ple_proj=ple_proj)
    depth = w_in.shape[0]
    n_prompt, seq, d = x_prompt.shape
    n_sample, dec_seq, _ = x_sample.shape
    mp, ms = n_prompt * seq, n_sample * dec_seq
    dk = d // N_RET_HEADS

    y_p = x_prompt.astype(f32).reshape(mp, d)
    y_s = x_sample.astype(f32).reshape(ms, d)
    cos_t, sin_t = _rope_table(seq, dec_seq, ms, dk // 2)
    cos_t, sin_t = (jnp.concatenate([jnp.tile(t[:seq], (n_prompt, 1)), t[seq:]], axis=0) for t in (cos_t, sin_t))
    outs = [[] for _ in range(6)]
    for i in range(depth):
        lp = {k: v[i].astype(f32) for k, v in params.items()}
        y_p, y_s, st = _layer([(y_p, 0), (y_s, mp)], p_prompt[i].astype(f32).reshape(mp, -1),
                              p_sample[i].astype(f32).reshape(ms, -1), n_prompt, seq, n_sample, dec_seq,
                              state_lru[i].astype(f32), state_conv[i].astype(f32), state_ret[i].astype(f32),
                              cos_t, sin_t, lp, final_norm.astype(f32) if i == depth - 1 else None)
        for o, s in zip(outs, st):
            o.append(s)
    y_prompt = y_p.reshape(n_prompt, seq, d).astype(x_prompt.dtype)
    y_sample = y_s.reshape(n_sample, dec_seq, d).astype(x_sample.dtype)
    lru_p, conv_p, ret_p, lru_s, conv_s, ret_s = (jnp.stack(o) for o in outs)
    return (y_prompt, y_sample, lru_p.astype(state_lru.dtype), conv_p.astype(state_conv.dtype),
            ret_p.astype(state_ret.dtype), lru_s.astype(state_lru.dtype), conv_s.astype(state_conv.dtype),
            ret_s.astype(state_ret.dtype))
```

```python
import functools

import jax
import jax.numpy as jnp
from jax import lax
from jax.experimental import pallas as pl
from jax.experimental.pallas import tpu as pltpu

f32 = jnp.float32
bf16 = jnp.bfloat16

N_LRU_BLOCKS = 8
CONV_W = 4
LRU_C = 8.0
N_RET_HEADS = 8
ROPE_BASE = 10000.0
EPS = 1e-6
PAST_LEN = 16384

RET_CHUNK = 256
SAMPLE_PAIR = 2
ROW_TILE = 1024
IN_ROW_TILE = 1536
GATE_COL_TILE = 256
S_RING = 3
VMEM_LIMIT = 58 * 1024 * 1024


def _cparams(sem):
    return pltpu.CompilerParams(dimension_semantics=sem, vmem_limit_bytes=VMEM_LIMIT)


def _sigmoid(x):
    return 1.0 / (1.0 + jnp.exp(-x))


def _rms(x, g):
    return x * lax.rsqrt(jnp.mean(x * x, axis=-1, keepdims=True) + EPS) * g


def _dot(a, b):
    return jnp.dot(a, b, preferred_element_type=f32)


def _dot_nt(a, b):
    return lax.dot_general(a, b, (((1,), (1,)), ((), ())), preferred_element_type=f32)


def _dot_tn(a, b):
    return lax.dot_general(a, b, (((0,), (0,)), ((), ())), preferred_element_type=f32)


_ANY = pl.BlockSpec(memory_space=pl.ANY)


def _cast_kernel(x_ref, o_ref):
    o_ref[...] = x_ref[...].astype(o_ref.dtype)


def _to_bf16(w):
    w2 = w.reshape(-1, w.shape[-1])
    return pl.pallas_call(_cast_kernel, out_shape=jax.ShapeDtypeStruct(w2.shape, bf16), name="to_bf16")(w2).reshape(w.shape)


def _rope_table_kernel(inv_ref, cos_ref, sin_ref, *, seq, dec_seq):
    rows = cos_ref.shape[0]
    r = lax.broadcasted_iota(jnp.int32, (rows, inv_ref.shape[1]), 0)
    pos = jnp.where(r < seq, r, PAST_LEN + lax.rem(r - seq, dec_seq))
    ang = pos.astype(f32) * inv_ref[...]
    cos_ref[...] = jnp.cos(ang)
    sin_ref[...] = jnp.sin(ang)


def _rope_table(seq, dec_seq, n_sample_rows, half):
    inv = (ROPE_BASE ** (-jnp.arange(half, dtype=f32) / half)).reshape(1, half)
    rows = seq + n_sample_rows
    return pl.pallas_call(
        functools.partial(_rope_table_kernel, seq=seq, dec_seq=dec_seq),
        out_shape=(jax.ShapeDtypeStruct((rows, half), f32), jax.ShapeDtypeStruct((rows, half), f32)),
        name="rope_table",
    )(inv)


def _ffn_kernel(*refs, group_rows, n_tiles, n_col_tiles, emit_norm):
    n_groups = len(group_rows)
    x_hbm = refs[:n_groups]
    g_ref, wg_hbm, wu_hbm, wd_hbm = refs[n_groups:n_groups + 4]
    rest = list(refs[n_groups + 4:])
    g2_ref = rest.pop(0) if emit_norm else None
    xo_ref = rest.pop(0)
    u2_ref = rest.pop(0) if emit_norm else None
    u_scr, wg_buf, wu_buf, wd_buf, w_sem = rest
    i = pl.program_id(0)
    f = pl.program_id(1)
    tm = xo_ref.shape[0]
    tf = wg_buf.shape[2]
    step = i * n_col_tiles + f
    n_steps = n_tiles * n_col_tiles

    def weight_copies(s):
        slot = lax.rem(s, W_RING)
        c0 = pl.multiple_of(lax.rem(s, n_col_tiles) * tf, tf)
        return (pltpu.make_async_copy(wg_hbm.at[:, pl.ds(c0, tf)], wg_buf.at[slot], w_sem.at[0, slot]),
                pltpu.make_async_copy(wu_hbm.at[:, pl.ds(c0, tf)], wu_buf.at[slot], w_sem.at[1, slot]),
                pltpu.make_async_copy(wd_hbm.at[pl.ds(c0, tf), :], wd_buf.at[slot], w_sem.at[2, slot]))

    @pl.when(step == 0)
    def _():
        for s0 in range(W_RING - 1):
            for cp in weight_copies(jnp.int32(s0)):
                cp.start()

    @pl.when(step + (W_RING - 1) < n_steps)
    def _():
        for cp in weight_copies(step + (W_RING - 1)):
            cp.start()

    @pl.when(f == 0)
    def _():
        for t in range(n_tiles):
            @pl.when(i == t)
            def _(t=t):
                for gi, (r0, nr) in enumerate(group_rows):
                    lo, hi = max(t * tm, r0), min((t + 1) * tm, r0 + nr)
                    if lo < hi:
                        pltpu.sync_copy(x_hbm[gi].at[lo - r0:hi - r0, :], xo_ref.at[lo - t * tm:hi - t * tm, :])

        u_scr[...] = _rms(xo_ref[...], g_ref[...]).astype(bf16)

    for cp in weight_copies(step):
        cp.wait()
    slot = lax.rem(step, W_RING)
    u = u_scr[...]
    hs = []
    for c0 in range(0, tf, FFN_SUB_TILE):
        cols = slice(c0, c0 + FFN_SUB_TILE)
        hg = _dot(u, wg_buf[slot, :, cols].astype(bf16))
        hu = _dot(u, wu_buf[slot, :, cols].astype(bf16))
        hs.append((0.5 * hg * _sigmoid(hg) * hu).astype(bf16))
    h = hs[0] if len(hs) == 1 else jnp.concatenate(hs, axis=1)
    xo_ref[...] += _dot(h, wd_buf[slot].astype(bf16))

    if emit_norm:
        @pl.when(f == pl.num_programs(1) - 1)
        def _():
            u2_ref[...] = _rms(xo_ref[...], g2_ref[...]).astype(bf16)


FFN_SUB_TILE = 256
W_RING = 3
FFN_VMEM_BUDGET = 60 * 1024 * 1024


def _ffn_tiles(m, d, ff, with_norm):
    sub = FFN_SUB_TILE
    for tm in range(m, 1023, -256):
        if m % tm:
            continue
        for tf in (4 * sub, 2 * sub, sub):
            if ff % tf == 0:
                rows = tm * d * (4 + 2 + (2 if with_norm else 0))
                weights = 3 * d * tf * 4 * W_RING
                temps = 2 * tm * sub * 4 + tm * tf * 2 + (2 * d * sub + tf * d) * 2
                if rows + weights + temps <= FFN_VMEM_BUDGET:
                    return tm, tf
    raise ValueError("no FFN tiling fits VMEM")


def _ffn(parts, m_total, g, wg, wu, wd, g2=None, *, tm, tf):
    d = wg.shape[0]
    ff = wg.shape[1]
    emit_norm = g2 is not None
    row = lambda i, f: (i, 0)
    out_mode = dict(pipeline_mode=pl.Buffered(1))
    in_specs = [_ANY] * len(parts) + [pl.BlockSpec((1, d), lambda i, f: (0, 0)), _ANY, _ANY, _ANY]
    args = [xp for xp, _ in parts] + [g.reshape(1, d), wg, wu, wd]
    out_shape = [jax.ShapeDtypeStruct((m_total, d), f32)]
    out_specs = [pl.BlockSpec((tm, d), row, **out_mode)]
    if emit_norm:
        in_specs.append(pl.BlockSpec((1, d), lambda i, f: (0, 0)))
        args.append(g2.reshape(1, d))
        out_shape.append(jax.ShapeDtypeStruct((m_total, d), bf16))
        out_specs.append(pl.BlockSpec((tm, d), row, **out_mode))
    outs = pl.pallas_call(
        functools.partial(_ffn_kernel, group_rows=tuple((r0, xp.shape[0]) for xp, r0 in parts),
                          n_tiles=m_total // tm, n_col_tiles=ff // tf, emit_norm=emit_norm),
        grid=(m_total // tm, ff // tf),
        in_specs=in_specs,
        out_specs=out_specs,
        out_shape=out_shape,
        scratch_shapes=[pltpu.VMEM((tm, d), bf16), pltpu.VMEM((W_RING, d, tf), f32), pltpu.VMEM((W_RING, d, tf), f32),
                        pltpu.VMEM((W_RING, tf, d), f32), pltpu.SemaphoreType.DMA((3, W_RING))],
        compiler_params=_cparams(("arbitrary", "arbitrary")),
        name="ffn",
    )(*args)
    return outs if emit_norm else outs[0]


MM_ROW_SPLIT = 2


def _mm_kernel(a_ref, b_ref, *rest, epilogue, col_axis):
    *extra, o_ref, bw_scr = rest
    j = pl.program_id(col_axis)
    slot = j if col_axis == 1 else 0

    @pl.when(pl.program_id(1 - col_axis) == 0)
    def _():
        bw_scr[slot] = b_ref[...].astype(bf16)

    rows = a_ref.shape[0] // MM_ROW_SPLIT
    for r in range(MM_ROW_SPLIT):
        rs = slice(r * rows, (r + 1) * rows)
        o_ref[rs, :] = epilogue(_dot(a_ref[rs, :], bw_scr[slot]), *extra, rows=rs, col_tile=j).astype(o_ref.dtype)


def _mm(a, b, *, col0, n_cols, out_dtype, epilogue, extras=(), tm=ROW_TILE, tn=1024, resident=True, name="mm"):
    m, k = a.shape
    off = col0 // tn
    nj = n_cols // tn
    if resident:
        grid, order = (m // tm, nj), (lambda f: (lambda i, j: f(j, i)))
        b_map = lambda j, i: (0, jnp.where(i == 0, j, nj - 1) + off)
    else:
        grid, order = (nj, m // tm), (lambda f: f)
        b_map = lambda j, i: (0, j + off)
    in_specs = [pl.BlockSpec((tm, k), order(lambda j, i: (i, 0))), pl.BlockSpec((k, tn), order(b_map))]
    args = [a, b]
    for arr, blk, imap in extras:
        in_specs.append(pl.BlockSpec(blk, order(imap)))
        args.append(arr)
    return pl.pallas_call(
        functools.partial(_mm_kernel, epilogue=epilogue, col_axis=1 if resident else 0),
        grid=grid,
        in_specs=in_specs,
        out_specs=pl.BlockSpec((tm, tn), order(lambda j, i: (i, j))),
        out_shape=jax.ShapeDtypeStruct((m, n_cols), out_dtype),
        scratch_shapes=[pltpu.VMEM((nj if resident else 1, k, tn), bf16)],
        compiler_params=_cparams(("arbitrary", "arbitrary")),
        name=name,
    )(*args)


def _gelu_tanh(x):
    return 0.5 * x * (1.0 + jnp.tanh(0.7978845608028654 * (x + 0.044715 * (x * x * x))))


def _rope_epilogue(acc, bias_ref, cos_ref, sin_ref, *, rows, col_tile, k_tile0, k_scale, head_dim):
    z = acc + bias_ref[...]
    cos = cos_ref[rows, :]
    sin = sin_ref[rows, :]
    half = head_dim // 2
    parts = []
    for h0 in range(0, z.shape[1], head_dim):
        x1 = z[:, h0:h0 + half]
        x2 = z[:, h0 + half:h0 + head_dim]
        parts += [x1 * cos - x2 * sin, x2 * cos + x1 * sin]
    scale = jnp.where(col_tile >= k_tile0, k_scale, 1.0).astype(f32)
    return jnp.concatenate(parts, axis=1) * scale


def _lru_gates(xc, n, cs, wa_ref, wx_ref, ba_ref, bx_ref, lam_ref):
    xcb = xc.astype(bf16)
    r = _sigmoid(_dot(xcb, wa_ref[n].astype(bf16)) + ba_ref[:, cs])
    gi = _sigmoid(_dot(xcb, wx_ref[n].astype(bf16)) + bx_ref[:, cs])
    log_a = -LRU_C * r * jax.nn.softplus(-lam_ref[:, cs])
    a = jnp.exp(log_a)
    a2 = a * a
    one_minus = jnp.where(log_a < -0.25, 1.0 - a2, -jnp.tanh(log_a) * (a2 + 1.0))
    return a, one_minus * lax.rsqrt(jnp.maximum(one_minus, 1e-36)), gi


def _lru_prompt_kernel(*refs, n_seq):
    oa_ref = refs[9]

    @pl.when(pl.program_id(0) >= n_seq)
    def _():
        oa_ref[...] = jnp.zeros_like(oa_ref)

    @pl.when(pl.program_id(0) < n_seq)
    def _():
        _lru_prompt_body(*refs)


def _lru_prompt_body(xa_ref, gg_ref, cw_ref, cb_ref, wa_ref, wx_ref, ba_ref, bx_ref, lam_ref,
                     oa_ref, hl_ref, cn_ref, xe_scr, a_scr, b_scr, h_scr, hc_scr):
    tt, c = xa_ref.shape
    blk = c // N_LRU_BLOCKS
    hist = 8
    t = pl.program_id(1)

    @pl.when(t == 0)
    def _():
        xe_scr[0:hist, :] = jnp.zeros((hist, c), f32)
        hc_scr[...] = jnp.zeros_like(hc_scr)

    xe_scr[hist:hist + tt, :] = xa_ref[...]
    pos = lax.broadcasted_iota(jnp.int32, (tt, 1), 0) + t * tt
    for n in range(N_LRU_BLOCKS):
        cs = slice(n * blk, (n + 1) * blk)
        ext = xe_scr[:, cs]
        xc = cb_ref[:, cs] + cw_ref[CONV_W - 1:CONV_W, cs] * ext[hist:, :]
        for s in range(1, CONV_W):
            xc = xc + cw_ref[CONV_W - 1 - s:CONV_W - s, cs] * pltpu.roll(ext, s, axis=0)[hist:, :]
        a, mult, gi = _lru_gates(xc, n, cs, wa_ref, wx_ref, ba_ref, bx_ref, lam_ref)
        mult = jnp.where(pos == 0, 1.0, mult)
        a_scr[:, cs] = a
        b_scr[:, cs] = mult * (gi * xc)

    def step(i, h):
        h = a_scr[pl.ds(i, 1), :] * h + b_scr[pl.ds(i, 1), :]
        h_scr[pl.ds(i, 1), :] = h
        return h

    h = lax.fori_loop(0, tt, step, hc_scr[...], unroll=8)
    hc_scr[...] = h
    hl_ref[0] = h
    for n in range(N_LRU_BLOCKS):
        cs = slice(n * blk, (n + 1) * blk)
        oa_ref[:, cs] = (h_scr[:, cs] * gg_ref[:, cs].astype(f32)).astype(bf16)
    cn_ref[0] = xe_scr[pl.ds(hist + tt - (CONV_W - 1), CONV_W - 1), :]
    xe_scr[0:hist, :] = xe_scr[pl.ds(tt, hist), :]


def _lru_sample_kernel(xa_ref, gg_ref, scp_ref, h0_ref, cw_ref, cb_ref, wa_ref, wx_ref, ba_ref, bx_ref, lam_ref,
                       oa_prev_ref, oa_ref, hl_ref, cn_ref):
    del oa_prev_ref
    rows, c = xa_ref.shape
    bt, t_len = scp_ref.shape[0], scp_ref.shape[1]
    blk = c // N_LRU_BLOCKS
    tpos = lax.broadcasted_iota(jnp.int32, (bt, t_len, blk), 1)
    for n in range(N_LRU_BLOCKS):
        cs = slice(n * blk, (n + 1) * blk)
        x3 = xa_ref[:, cs].reshape(bt, t_len, blk)
        ext = jnp.concatenate([scp_ref[:, :, cs], x3], axis=1)
        xc3 = cb_ref[:, cs] + cw_ref[CONV_W - 1:CONV_W, cs] * x3
        for s in range(1, CONV_W):
            xc3 = xc3 + cw_ref[CONV_W - 1 - s:CONV_W - s, cs] * pltpu.roll(ext, s, axis=1)[:, t_len:, :]
        xc = xc3.reshape(rows, blk)
        a, mult, gi = _lru_gates(xc, n, cs, wa_ref, wx_ref, ba_ref, bx_ref, lam_ref)
        a3 = a.reshape(bt, t_len, blk)
        b3 = (mult * (gi * xc)).reshape(bt, t_len, blk)
        d = 1
        while d < t_len:
            keep = tpos >= d
            b3 = jnp.where(keep, a3 * pltpu.roll(b3, d, axis=1) + b3, b3)
            a3 = jnp.where(keep, a3 * pltpu.roll(a3, d, axis=1), a3)
            d *= 2
        h3 = b3 + a3 * h0_ref[:, :, cs]
        oa_ref[:, cs] = (h3.reshape(rows, blk) * gg_ref[:, cs].astype(f32)).astype(bf16)
        hl_ref[:, :, cs] = h3[:, t_len - 1:t_len, :]
        cn_ref[:, :, cs] = pltpu.roll(x3, CONV_W - 1, axis=1)[:, 0:CONV_W - 1, :]


def _ret_norm_gate(o, rn, sg):
    return (o * lax.rsqrt(jnp.mean(o * o, axis=-1, keepdims=True) + EPS) * rn * sg.astype(f32)).astype(bf16)


def _ret_prompt_kernel(*refs, n_seq):
    ob_ref = refs[9]

    @pl.when(pl.program_id(0) >= n_seq)
    def _():
        ob_ref[...] = jnp.zeros_like(ob_ref)

    @pl.when(pl.program_id(0) < n_seq)
    def _():
        _ret_prompt_body(*refs)


def _ret_prompt_body(q_ref, k_ref, v_ref, sg_ref, rn_ref, dm_ref, cd_ref, sd_ref, chd_ref, ob_ref, so_ref):
    c = pl.program_id(1)
    dk, dv = so_ref.shape[2], so_ref.shape[3]

    @pl.when(c == 0)
    def _():
        so_ref[...] = jnp.zeros_like(so_ref)

    for h in range(N_RET_HEADS):
        ks = slice(h * dk, (h + 1) * dk)
        vs = slice(h * dv, (h + 1) * dv)
        q = q_ref[:, ks]
        k = k_ref[:, ks]
        v = v_ref[:, vs]
        s_old = so_ref[0, h]
        scores = _dot_nt(q, k) * dm_ref[h]
        inner = _dot(scores.astype(bf16), v)
        cross = _dot(q, s_old.astype(bf16)) * cd_ref[h]
        kd = (k.astype(f32) * sd_ref[h]).astype(bf16)
        so_ref[0, h] = s_old * chd_ref[h] + _dot_tn(kd, v)
        ob_ref[:, vs] = _ret_norm_gate(inner + cross, rn_ref[h], sg_ref[:, vs])


def _ret_sample_body(q_ref, k_ref, v_ref, sg_ref, s_ref, rn_ref, dm_ref, cd_ref, sd_ref, chd_ref, ob_ref, so_ref,
                     *, t_len, head0):
    rows = q_ref.shape[0]
    n_heads, dk, dv = so_ref.shape[1], so_ref.shape[2], so_ref.shape[3]
    row = lax.broadcasted_iota(jnp.int32, (rows, 1), 0)
    for hl in range(n_heads):
        h = head0 + hl
        ks = slice(hl * dk, (hl + 1) * dk)
        vs = slice(hl * dv, (hl + 1) * dv)
        q = q_ref[:, ks]
        k = k_ref[:, ks]
        v = v_ref[:, vs]
        scores = _dot_nt(q, k) * dm_ref[h]
        o = _dot(scores.astype(bf16), v)
        kd = k.astype(f32) * sd_ref[h]
        for bb in range(rows // t_len):
            mine = (row >= bb * t_len) & (row < (bb + 1) * t_len)
            s_old = s_ref[bb, hl]
            cross = _dot(q, s_old.astype(bf16)) * cd_ref[h]
            o = o + jnp.where(mine, cross, 0.0)
            so_ref[bb, hl] = s_old * chd_ref[h] + _dot_tn(jnp.where(mine, kd, 0.0).astype(bf16), v)
        ob_ref[:, vs] = _ret_norm_gate(o, rn_ref[h], sg_ref[:, vs])


def _gates_ret_kernel(a_ref, b_ref, bias_ref, q_ref, k_ref, v_ref, sg_ref, s_hbm, rn_ref, dm_ref, cd_ref, sd_ref,
                      chd_ref, ob_prev_ref, g_ref, ob_ref, so_ref, bw_scr, s_buf, s_sem, *, t_len, n_units):
    del ob_prev_ref
    i, j = pl.program_id(0), pl.program_id(1)
    n_seq, n_heads = s_buf.shape[1], s_buf.shape[2]
    n_parts = N_RET_HEADS // n_heads
    unit = i * pl.num_programs(1) + j

    def state_copy(u):
        slot = lax.rem(u, S_RING)
        src = s_hbm.at[pl.ds((u // n_parts) * n_seq, n_seq), pl.ds(lax.rem(u, n_parts) * n_heads, n_heads)]
        return pltpu.make_async_copy(src, s_buf.at[slot], s_sem.at[slot])

    @pl.when(unit == 0)
    def _():
        for u0 in range(S_RING - 1):
            state_copy(jnp.int32(u0)).start()

    @pl.when(unit + (S_RING - 1) < n_units)
    def _():
        state_copy(unit + (S_RING - 1)).start()

    @pl.when(i == 0)
    def _():
        bw_scr[j] = b_ref[...].astype(bf16)

    rows = a_ref.shape[0] // MM_ROW_SPLIT
    for r in range(MM_ROW_SPLIT):
        rs = slice(r * rows, (r + 1) * rows)
        g_ref[rs, :] = _sigmoid(_dot(a_ref[rs, :], bw_scr[j]) + bias_ref[...]).astype(g_ref.dtype)

    state_copy(unit).wait()
    _ret_sample_body(q_ref, k_ref, v_ref, sg_ref, s_buf.at[lax.rem(unit, S_RING)], rn_ref, dm_ref, cd_ref, sd_ref,
                     chd_ref, ob_ref, so_ref, t_len=t_len, head0=lax.rem(unit, n_parts) * n_heads)


def _decay_tables(chunk, n_seq):
    log_g = jnp.log1p(-jnp.exp2(-5.0 - jnp.arange(N_RET_HEADS, dtype=f32)))
    r = jnp.arange(chunk * n_seq)
    idx = (r % chunk).astype(f32)
    same = (r[:, None] // chunk) == (r[None, :] // chunk)
    diff = idx[:, None] - idx[None, :]
    dmask = jnp.where(same & (diff >= 0), jnp.exp(jnp.maximum(diff, 0.0)[None] * log_g[:, None, None]), 0.0)
    cross_decay = jnp.exp((idx[None] + 1.0) * log_g[:, None])[..., None]
    state_decay = jnp.exp((chunk - 1.0 - idx[None]) * log_g[:, None])[..., None]
    chunk_decay = jnp.exp(chunk * log_g)[:, None, None]
    return dmask, cross_decay, state_decay, chunk_decay


def _ple_kernel(x_ref, pep_ref, pes_ref, gn_ref, wg_ref, bg_ref, pp_ref, fn_ref, yp_ref, ys_ref, *, n_prompt_tiles, final):
    i = pl.program_id(0)
    x = x_ref[...]
    u = _rms(x, gn_ref[...]).astype(bf16)
    gate = _sigmoid(_dot(u, wg_ref[...]) + bg_ref[...])
    pe = jnp.where(i < n_prompt_tiles, pep_ref[...], pes_ref[...]).astype(bf16)
    x = x + gate * _dot(pe, pp_ref[...].astype(bf16))
    y = _rms(x, fn_ref[...]) if final else x

    @pl.when(i < n_prompt_tiles)
    def _():
        yp_ref[...] = y

    @pl.when(i >= n_prompt_tiles)
    def _():
        ys_ref[...] = y


def _layer(x_parts, pe_p, pe_s, n_prompt, seq, n_sample, dec_seq, h0_s, conv_s, ret_s, cos_t, sin_t, lp, final_norm):
    d = lp['mix_norm'].shape[0]
    mp, ms = n_prompt * seq, n_sample * dec_seq
    m = mp + ms
    d_rnn = lp['conv_w'].shape[1]
    hh = N_RET_HEADS
    dk = d // hh
    dv = lp['ret_norm'].shape[1]
    qk_dim, v_dim = hh * dk, hh * dv
    tm, tn = ROW_TILE, 1024

    ff = lp['ffn1_wg'].shape[1]
    tm1, tf1 = _ffn_tiles(m, d, ff, with_norm=True)
    x1, u = _ffn(x_parts, m, lp['ffn1_norm'], lp['ffn1_wg'], lp['ffn1_wu'], lp['ffn1_wd'], lp['mix_norm'],
                 tm=tm1, tf=tf1)

    w_in = lp['w_in']
    b_in = lp['b_in'].reshape(1, -1)

    def bias_extra(col0):
        return (b_in, (1, tn), lambda j, i, o=col0 // tn: (0, j + o))

    def with_bias(fn):
        return lambda acc, b_ref, rows, col_tile: fn(acc + b_ref[...])

    c_xa, c_ga, c_q, c_v = 0, d_rnn, 2 * d_rnn, 2 * d_rnn + 2 * qk_dim
    c_gr, c_gate = c_v + v_dim, c_v + 2 * v_dim
    tmi = IN_ROW_TILE if m % IN_ROW_TILE == 0 else tm
    proj = functools.partial(_mm, u, w_in, tm=tmi, tn=tn)
    xa = proj(col0=c_xa, n_cols=d_rnn, out_dtype=f32, epilogue=with_bias(lambda z: z),
              extras=[bias_extra(c_xa)], name="inproj_xa")
    gg = proj(col0=c_ga, n_cols=d_rnn, out_dtype=bf16, epilogue=with_bias(_gelu_tanh),
              extras=[bias_extra(c_ga)], name="inproj_ga")
    half = dk // 2
    rope_map = lambda j, i: (i, 0)
    qk = proj(col0=c_q, n_cols=2 * qk_dim, out_dtype=bf16,
              epilogue=functools.partial(_rope_epilogue, k_tile0=qk_dim // tn, k_scale=dk ** -0.5, head_dim=dk),
              extras=[bias_extra(c_q), (cos_t, (tmi, half), rope_map), (sin_t, (tmi, half), rope_map)],
              name="inproj_qk")
    v = proj(col0=c_v, n_cols=v_dim, out_dtype=bf16, epilogue=with_bias(lambda z: z),
             extras=[bias_extra(c_v)], name="inproj_v")
    sg = proj(col0=c_gr, n_cols=v_dim, out_dtype=bf16, epilogue=with_bias(lambda z: z * _sigmoid(z)),
              extras=[bias_extra(c_gr)], name="inproj_gr")

    row2 = lambda a: a.reshape(1, -1)
    lru_w = [lp['conv_w'], row2(lp['conv_b']), lp['lru_wa'], lp['lru_wx'], row2(lp['lru_ba']), row2(lp['lru_bx']),
             row2(lp['lru_lambda'])]
    nb = N_LRU_BLOCKS
    blk = d_rnn // nb
    z2 = lambda *_: (0, 0)
    z3 = lambda *_: (0, 0, 0)
    lru_w_specs = [pl.BlockSpec((CONV_W, d_rnn), z2), pl.BlockSpec((1, d_rnn), z2),
                   pl.BlockSpec((nb, blk, blk), z3), pl.BlockSpec((nb, blk, blk), z3),
                   pl.BlockSpec((1, d_rnn), z2), pl.BlockSpec((1, d_rnn), z2), pl.BlockSpec((1, d_rnn), z2)]

    tt = 512
    ntt = seq // tt
    assert ms % tt == 0 and ms // tt <= ntt
    seq_rows = lambda b, t: (jnp.minimum(b * ntt + t, mp // tt - 1), 0)
    all_rows = lambda b, t: (jnp.minimum(b * ntt + t, m // tt - 1), 0)
    per_seq = lambda b, t: (jnp.minimum(b, n_prompt - 1), 0, 0)
    oa, hl_p, cn_p = pl.pallas_call(
        functools.partial(_lru_prompt_kernel, n_seq=n_prompt),
        grid=(n_prompt + 1, ntt),
        in_specs=[pl.BlockSpec((tt, d_rnn), seq_rows), pl.BlockSpec((tt, d_rnn), seq_rows)] + lru_w_specs,
        out_specs=[pl.BlockSpec((tt, d_rnn), all_rows), pl.BlockSpec((1, 1, d_rnn), per_seq),
                   pl.BlockSpec((1, CONV_W - 1, d_rnn), per_seq)],
        out_shape=[jax.ShapeDtypeStruct((m, d_rnn), bf16), jax.ShapeDtypeStruct((n_prompt, 1, d_rnn), f32),
                   jax.ShapeDtypeStruct((n_prompt, CONV_W - 1, d_rnn), f32)],
        scratch_shapes=[pltpu.VMEM((tt + 8, d_rnn), f32), pltpu.VMEM((tt, d_rnn), f32), pltpu.VMEM((tt, d_rnn), f32),
                        pltpu.VMEM((tt, d_rnn), f32), pltpu.VMEM((1, d_rnn), f32)],
        compiler_params=_cparams(("arbitrary", "arbitrary")),
        name="lru_prompt",
    )(xa, gg, *lru_w)

    bt = 32
    rows_s = bt * dec_seq
    s_off = mp // rows_s
    scp = jnp.pad(conv_s, ((0, 0), (dec_seq - (CONV_W - 1), 0), (0, 0)))
    samp_rows = lambda i: (s_off + i, 0)
    samp3 = lambda i: (i, 0, 0)
    oa, hl_s, cn_s = pl.pallas_call(
        _lru_sample_kernel,
        grid=(n_sample // bt,),
        in_specs=[pl.BlockSpec((rows_s, d_rnn), samp_rows), pl.BlockSpec((rows_s, d_rnn), samp_rows),
                  pl.BlockSpec((bt, dec_seq, d_rnn), samp3), pl.BlockSpec((bt, 1, d_rnn), samp3)] + lru_w_specs + [_ANY],
        out_specs=[pl.BlockSpec((rows_s, d_rnn), samp_rows), pl.BlockSpec((bt, 1, d_rnn), samp3),
                   pl.BlockSpec((bt, CONV_W - 1, d_rnn), samp3)],
        out_shape=[jax.ShapeDtypeStruct((m, d_rnn), bf16), jax.ShapeDtypeStruct((n_sample, 1, d_rnn), f32),
                   jax.ShapeDtypeStruct((n_sample, CONV_W - 1, d_rnn), f32)],
        input_output_aliases={11: 0},
        compiler_params=_cparams(("parallel",)),
        name="lru_sample",
    )(xa, gg, scp, h0_s.reshape(n_sample, 1, d_rnn), *lru_w, oa)

    rn = lp['ret_norm'].reshape(hh, 1, dv)
    ct = RET_CHUNK
    nct = seq // ct
    tabs = _decay_tables(ct, 1)
    full3 = lambda shape: pl.BlockSpec(shape, lambda *_: (0, 0, 0))
    tab_specs = lambda r: [full3((hh, 1, dv)), full3((hh, r, r)), full3((hh, r, 1)), full3((hh, r, 1)), full3((hh, 1, 1))]
    assert ms % ct == 0 and ms // ct <= nct
    chunk_row = lambda b, c: jnp.minimum(b * nct + c, mp // ct - 1)
    ob, so_p = pl.pallas_call(
        functools.partial(_ret_prompt_kernel, n_seq=n_prompt),
        grid=(n_prompt + 1, nct),
        in_specs=[pl.BlockSpec((ct, qk_dim), lambda b, c: (chunk_row(b, c), 0)),
                  pl.BlockSpec((ct, qk_dim), lambda b, c: (chunk_row(b, c), 1)),
                  pl.BlockSpec((ct, v_dim), lambda b, c: (chunk_row(b, c), 0)),
                  pl.BlockSpec((ct, v_dim), lambda b, c: (chunk_row(b, c), 0))] + tab_specs(ct),
        out_specs=[pl.BlockSpec((ct, v_dim), lambda b, c: (jnp.minimum(b * nct + c, m // ct - 1), 0)),
                   pl.BlockSpec((1, hh, dk, dv), lambda b, c: (jnp.minimum(b, n_prompt - 1), 0, 0, 0))],
        out_shape=[jax.ShapeDtypeStruct((m, v_dim), bf16), jax.ShapeDtypeStruct((n_prompt, hh, dk, dv), f32)],
        compiler_params=_cparams(("arbitrary", "arbitrary")),
        name="ret_prompt",
    )(qk, qk, v, sg, rn, *tabs)

    pr = SAMPLE_PAIR
    rows_r = pr * dec_seq
    r_off = mp // rows_r
    tabs = _decay_tables(dec_seq, pr)
    n_parts = 2
    hp = hh // n_parts
    n_units = (n_sample // pr) * n_parts
    tng = GATE_COL_TILE
    njg = 2 * d // tng
    nig = n_units // njg
    tmg = m // nig
    assert nig * njg == n_units >= S_RING and nig * tmg == m and tmg % 16 == 0 and njg % n_parts == 0
    unit = lambda i, j: i * njg + j
    seqs = lambda i, j: unit(i, j) // n_parts
    part = lambda i, j: unit(i, j) % n_parts
    qk_parts = qk_dim // (hp * dk)
    gates, ob, so_s = pl.pallas_call(
        functools.partial(_gates_ret_kernel, t_len=dec_seq, n_units=n_units),
        grid=(nig, njg),
        in_specs=[pl.BlockSpec((tmg, d), lambda i, j: (i, 0)),
                  pl.BlockSpec((d, tng), lambda i, j: (0, jnp.where(i == 0, j, njg - 1) + c_gate // tng)),
                  pl.BlockSpec((1, tng), lambda i, j: (0, j + c_gate // tng)),
                  pl.BlockSpec((rows_r, hp * dk), lambda i, j: (r_off + seqs(i, j), part(i, j))),
                  pl.BlockSpec((rows_r, hp * dk), lambda i, j: (r_off + seqs(i, j), qk_parts + part(i, j))),
                  pl.BlockSpec((rows_r, hp * dv), lambda i, j: (r_off + seqs(i, j), part(i, j))),
                  pl.BlockSpec((rows_r, hp * dv), lambda i, j: (r_off + seqs(i, j), part(i, j))),
                  _ANY] + tab_specs(rows_r) + [_ANY],
        out_specs=[pl.BlockSpec((tmg, tng), lambda i, j: (i, j)),
                   pl.BlockSpec((rows_r, hp * dv), lambda i, j: (r_off + seqs(i, j), part(i, j))),
                   pl.BlockSpec((pr, hp, dk, dv), lambda i, j: (seqs(i, j), part(i, j), 0, 0))],
        out_shape=[jax.ShapeDtypeStruct((m, 2 * d), bf16), jax.ShapeDtypeStruct((m, v_dim), bf16),
                   jax.ShapeDtypeStruct((n_sample, hh, dk, dv), f32)],
        scratch_shapes=[pltpu.VMEM((njg, d, tng), bf16), pltpu.VMEM((S_RING, pr, hp, dk, dv), f32),
                        pltpu.SemaphoreType.DMA((S_RING,))],
        input_output_aliases={13: 1},
        compiler_params=_cparams(("arbitrary", "arbitrary")),
        name="gates_ret_sample",
    )(u, w_in, b_in, qk, qk, v, sg, ret_s, rn, *tabs, ob)

    tile = lambda j, i: (i, j)
    pa = _mm(oa, lp['proj_a'], col0=0, n_cols=d, out_dtype=f32, epilogue=lambda acc, rows, col_tile: acc,
             name="proj_a")
    tnb = 512
    merged = _mm(ob, lp['proj_b'], col0=0, n_cols=d, out_dtype=bf16, tn=tnb, resident=False,
                 epilogue=lambda acc, pa_ref, ga_ref, gb_ref, rows, col_tile: (
                     ga_ref[rows, :].astype(f32) * pa_ref[rows, :] + gb_ref[rows, :].astype(f32) * acc),
                 extras=[(pa, (tm, tnb), tile), (gates, (tm, tnb), tile),
                         (gates, (tm, tnb), lambda j, i: (i, j + d // tnb))], name="proj_b_merge")
    x2 = _mm(merged, lp['w_out'], col0=0, n_cols=d, out_dtype=f32,
             epilogue=lambda acc, x_ref, rows, col_tile: x_ref[rows, :] + acc,
             extras=[(x1, (tm, tn), tile)], name="w_out")

    tm2, tf2 = _ffn_tiles(m, d, ff, with_norm=False)
    x3 = _ffn([(x2, 0)], m, lp['ffn2_norm'], lp['ffn2_wg'], lp['ffn2_wu'], lp['ffn2_wd'], tm=tm2, tf=tf2)

    tp = 512
    ple_dim = pe_p.shape[1]
    fin = final_norm is not None
    fn = (final_norm if fin else lp['ple_norm']).reshape(1, d)
    npt = mp // tp
    c2 = lambda i: (0, 0)
    prow = lambda i: (jnp.minimum(i, npt - 1), 0)
    srow = lambda i: (jnp.maximum(i - npt, 0), 0)
    y_p, y_s = pl.pallas_call(
        functools.partial(_ple_kernel, n_prompt_tiles=npt, final=fin),
        grid=(m // tp,),
        in_specs=[pl.BlockSpec((tp, d), lambda i: (i, 0)), pl.BlockSpec((tp, ple_dim), prow),
                  pl.BlockSpec((tp, ple_dim), srow), pl.BlockSpec((1, d), c2), pl.BlockSpec((d, d), c2),
                  pl.BlockSpec((1, d), c2), pl.BlockSpec((ple_dim, d), c2), pl.BlockSpec((1, d), c2)],
        out_specs=[pl.BlockSpec((tp, d), prow), pl.BlockSpec((tp, d), srow)],
        out_shape=[jax.ShapeDtypeStruct((mp, d), f32), jax.ShapeDtypeStruct((ms, d), f32)],
        compiler_params=_cparams(("arbitrary",)),
        name="ple",
    )(x3, pe_p, pe_s, lp['ple_norm'].reshape(1, d), _to_bf16(lp['ple_wg']), lp['ple_bg'].reshape(1, d),
      lp['ple_proj'], fn)

    states = (hl_p.reshape(n_prompt, d_rnn), cn_p, so_p, hl_s.reshape(n_sample, d_rnn), cn_s, so_s)
    return y_p, y_s, states


def kernel(x_prompt, x_sample, p_prompt, p_sample, state_lru, state_conv, state_ret, ffn1_norm, ffn1_wg, ffn1_wu, ffn1_wd, mix_norm, w_in, b_in, conv_w, conv_b, lru_wa, lru_ba, lru_wx, lru_bx, lru_lambda, ret_norm, proj_a, proj_b, w_out, ffn2_norm, ffn2_wg, ffn2_wu, ffn2_wd, ple_norm, ple_wg, ple_bg, ple_proj, final_norm):
    params = dict(ffn1_norm=ffn1_norm, ffn1_wg=ffn1_wg, ffn1_wu=ffn1_wu, ffn1_wd=ffn1_wd, mix_norm=mix_norm,
                  w_in=w_in, b_in=b_in, conv_w=conv_w, conv_b=conv_b, lru_wa=lru_wa, lru_ba=lru_ba, lru_wx=lru_wx,
                  lru_bx=lru_bx, lru_lambda=lru_lambda, ret_norm=ret_norm, proj_a=proj_a, proj_b=proj_b,
                  w_out=w_out, ffn2_norm=ffn2_norm, ffn2_wg=ffn2_wg, ffn2_wu=ffn2_wu, ffn2_wd=ffn2_wd,
                  ple_norm=ple_norm, ple_wg=ple_wg, ple_bg=ple_bg, ple_proj=ple_proj)
    depth = w_in.shape[0]
    n_prompt, seq, d = x_prompt.shape
    n_sample, dec_seq, _ = x_sample.shape
    mp, ms = n_prompt * seq, n_sample * dec_seq
    dk = d // N_RET_HEADS

    y_p = x_prompt.astype(f32).reshape(mp, d)
    y_s = x_sample.astype(f32).reshape(ms, d)
    cos_t, sin_t = _rope_table(seq, dec_seq, ms, dk // 2)
    cos_t, sin_t = (jnp.concatenate([jnp.tile(t[:seq], (n_prompt, 1)), t[seq:]], axis=0) for t in (cos_t, sin_t))
    outs = [[] for _ in range(6)]
    for i in range(depth):
        lp = {k: v[i].astype(f32) for k, v in params.items()}
        y_p, y_s, st = _layer([(y_p, 0), (y_s, mp)], p_prompt[i].astype(f32).reshape(mp, -1),
                              p_sample[i].astype(f32).reshape(ms, -1), n_prompt, seq, n_sample, dec_seq,
                              state_lru[i].astype(f32), state_conv[i].astype(f32), state_ret[i].astype(f32),
                              cos_t, sin_t, lp, final_norm.astype(f32) if i == depth - 1 else None)
        for o, s in zip(outs, st):
            o.append(s)
    y_prompt = y_p.reshape(n_prompt, seq, d).astype(x_prompt.dtype)
    y_sample = y_s.reshape(n_sample, dec_seq, d).astype(x_sample.dtype)
    lru_p, conv_p, ret_p, lru_s, conv_s, ret_s = (jnp.stack(o) for o in outs)
    return (y_prompt, y_sample, lru_p.astype(state_lru.dtype), conv_p.astype(state_conv.dtype),
            ret_p.astype(state_ret.dtype), lru_s.astype(state_lru.dtype), conv_s.astype(state_conv.dtype),
            ret_s.astype(state_ret.dtype))
```

```python
import functools

import jax
import jax.numpy as jnp
from jax import lax
from jax.experimental import pallas as pl
from jax.experimental.pallas import tpu as pltpu

f32 = jnp.float32
bf16 = jnp.bfloat16

N_LRU_BLOCKS = 8
CONV_W = 4
LRU_C = 8.0
N_RET_HEADS = 8
ROPE_BASE = 10000.0
EPS = 1e-6
PAST_LEN = 16384

RET_CHUNK = 256
SAMPLE_PAIR = 2
ROW_TILE = 1024
IN_ROW_TILE = 1536
GATE_COL_TILE = 256
S_RING = 3
VMEM_LIMIT = 58 * 1024 * 1024


def _cparams(sem):
    return pltpu.CompilerParams(dimension_semantics=sem, vmem_limit_bytes=VMEM_LIMIT)


def _sigmoid(x):
    return 1.0 / (1.0 + jnp.exp(-x))


def _rms(x, g):
    return x * lax.rsqrt(jnp.mean(x * x, axis=-1, keepdims=True) + EPS) * g


def _dot(a, b):
    return jnp.dot(a, b, preferred_element_type=f32)


def _dot_nt(a, b):
    return lax.dot_general(a, b, (((1,), (1,)), ((), ())), preferred_element_type=f32)


def _dot_tn(a, b):
    return lax.dot_general(a, b, (((0,), (0,)), ((), ())), preferred_element_type=f32)


_ANY = pl.BlockSpec(memory_space=pl.ANY)


def _cast_kernel(x_ref, o_ref):
    o_ref[...] = x_ref[...].astype(o_ref.dtype)


def _to_bf16(w):
    w2 = w.reshape(-1, w.shape[-1])
    return pl.pallas_call(_cast_kernel, out_shape=jax.ShapeDtypeStruct(w2.shape, bf16), name="to_bf16")(w2).reshape(w.shape)


def _rope_table_kernel(inv_ref, cos_ref, sin_ref, *, seq, dec_seq):
    rows = cos_ref.shape[0]
    r = lax.broadcasted_iota(jnp.int32, (rows, inv_ref.shape[1]), 0)
    pos = jnp.where(r < seq, r, PAST_LEN + lax.rem(r - seq, dec_seq))
    ang = pos.astype(f32) * inv_ref[...]
    cos_ref[...] = jnp.cos(ang)
    sin_ref[...] = jnp.sin(ang)


def _rope_table(seq, dec_seq, n_sample_rows, half):
    inv = (ROPE_BASE ** (-jnp.arange(half, dtype=f32) / half)).reshape(1, half)
    rows = seq + n_sample_rows
    return pl.pallas_call(
        functools.partial(_rope_table_kernel, seq=seq, dec_seq=dec_seq),
        out_shape=(jax.ShapeDtypeStruct((rows, half), f32), jax.ShapeDtypeStruct((rows, half), f32)),
        name="rope_table",
    )(inv)


def _ffn_kernel(*refs, group_rows, n_tiles, emit_norm):
    n_groups = len(group_rows)
    x_hbm = refs[:n_groups]
    g_ref, wg_ref, wu_ref, wd_ref = refs[n_groups:n_groups + 4]
    rest = list(refs[n_groups + 4:])
    g2_ref = rest.pop(0) if emit_norm else None
    xo_ref = rest.pop(0)
    u2_ref = rest.pop(0) if emit_norm else None
    (u_scr,) = rest
    i = pl.program_id(0)
    f = pl.program_id(1)
    tm = xo_ref.shape[0]

    @pl.when(f == 0)
    def _():
        for t in range(n_tiles):
            @pl.when(i == t)
            def _(t=t):
                for gi, (r0, nr) in enumerate(group_rows):
                    lo, hi = max(t * tm, r0), min((t + 1) * tm, r0 + nr)
                    if lo < hi:
                        pltpu.sync_copy(x_hbm[gi].at[lo - r0:hi - r0, :], xo_ref.at[lo - t * tm:hi - t * tm, :])

        u_scr[...] = _rms(xo_ref[...], g_ref[...]).astype(bf16)

    u = u_scr[...]
    hs = []
    for c0 in range(0, wg_ref.shape[1], FFN_SUB_TILE):
        cols = slice(c0, c0 + FFN_SUB_TILE)
        hg = _dot(u, wg_ref[:, cols].astype(bf16))
        hu = _dot(u, wu_ref[:, cols].astype(bf16))
        hs.append((0.5 * hg * _sigmoid(hg) * hu).astype(bf16))
    h = hs[0] if len(hs) == 1 else jnp.concatenate(hs, axis=1)
    xo_ref[...] += _dot(h, wd_ref[...].astype(bf16))

    if emit_norm:
        @pl.when(f == pl.num_programs(1) - 1)
        def _():
            u2_ref[...] = _rms(xo_ref[...], g2_ref[...]).astype(bf16)


FFN_SUB_TILE = 256
FFN_VMEM_BUDGET = 60 * 1024 * 1024


def _ffn_tiles(m, d, ff, with_norm):
    sub = FFN_SUB_TILE
    for tm in range(m, 1023, -256):
        if m % tm:
            continue
        for tf in (4 * sub, 2 * sub, sub):
            if ff % tf == 0:
                rows = tm * d * (4 + 2 + (2 if with_norm else 0))
                weights = 3 * d * tf * 4 * 2
                temps = 2 * tm * sub * 4 + tm * tf * 2 + (2 * d * sub + tf * d) * 2
                if rows + weights + temps <= FFN_VMEM_BUDGET:
                    return tm, tf
    raise ValueError("no FFN tiling fits VMEM")


def _ffn(parts, m_total, g, wg, wu, wd, g2=None, *, tm, tf):
    d = wg.shape[0]
    ff = wg.shape[1]
    emit_norm = g2 is not None
    row = lambda i, f: (i, 0)
    out_mode = dict(pipeline_mode=pl.Buffered(1))
    in_specs = [_ANY] * len(parts) + [
        pl.BlockSpec((1, d), lambda i, f: (0, 0)),
        pl.BlockSpec((d, tf), lambda i, f: (0, f)),
        pl.BlockSpec((d, tf), lambda i, f: (0, f)),
        pl.BlockSpec((tf, d), lambda i, f: (f, 0)),
    ]
    args = [xp for xp, _ in parts] + [g.reshape(1, d), wg, wu, wd]
    out_shape = [jax.ShapeDtypeStruct((m_total, d), f32)]
    out_specs = [pl.BlockSpec((tm, d), row, **out_mode)]
    if emit_norm:
        in_specs.append(pl.BlockSpec((1, d), lambda i, f: (0, 0)))
        args.append(g2.reshape(1, d))
        out_shape.append(jax.ShapeDtypeStruct((m_total, d), bf16))
        out_specs.append(pl.BlockSpec((tm, d), row, **out_mode))
    outs = pl.pallas_call(
        functools.partial(_ffn_kernel, group_rows=tuple((r0, xp.shape[0]) for xp, r0 in parts),
                          n_tiles=m_total // tm, emit_norm=emit_norm),
        grid=(m_total // tm, ff // tf),
        in_specs=in_specs,
        out_specs=out_specs,
        out_shape=out_shape,
        scratch_shapes=[pltpu.VMEM((tm, d), bf16)],
        compiler_params=_cparams(("arbitrary", "arbitrary")),
        name="ffn",
    )(*args)
    return outs if emit_norm else outs[0]


MM_ROW_SPLIT = 2


def _mm_kernel(a_ref, b_ref, *rest, epilogue, col_axis):
    *extra, o_ref, bw_scr = rest
    j = pl.program_id(col_axis)
    slot = j if col_axis == 1 else 0

    @pl.when(pl.program_id(1 - col_axis) == 0)
    def _():
        bw_scr[slot] = b_ref[...].astype(bf16)

    rows = a_ref.shape[0] // MM_ROW_SPLIT
    for r in range(MM_ROW_SPLIT):
        rs = slice(r * rows, (r + 1) * rows)
        o_ref[rs, :] = epilogue(_dot(a_ref[rs, :], bw_scr[slot]), *extra, rows=rs, col_tile=j).astype(o_ref.dtype)


def _mm(a, b, *, col0, n_cols, out_dtype, epilogue, extras=(), tm=ROW_TILE, tn=1024, resident=True, name="mm"):
    m, k = a.shape
    off = col0 // tn
    nj = n_cols // tn
    if resident:
        grid, order = (m // tm, nj), (lambda f: (lambda i, j: f(j, i)))
        b_map = lambda j, i: (0, jnp.where(i == 0, j, nj - 1) + off)
    else:
        grid, order = (nj, m // tm), (lambda f: f)
        b_map = lambda j, i: (0, j + off)
    in_specs = [pl.BlockSpec((tm, k), order(lambda j, i: (i, 0))), pl.BlockSpec((k, tn), order(b_map))]
    args = [a, b]
    for arr, blk, imap in extras:
        in_specs.append(pl.BlockSpec(blk, order(imap)))
        args.append(arr)
    return pl.pallas_call(
        functools.partial(_mm_kernel, epilogue=epilogue, col_axis=1 if resident else 0),
        grid=grid,
        in_specs=in_specs,
        out_specs=pl.BlockSpec((tm, tn), order(lambda j, i: (i, j))),
        out_shape=jax.ShapeDtypeStruct((m, n_cols), out_dtype),
        scratch_shapes=[pltpu.VMEM((nj if resident else 1, k, tn), bf16)],
        compiler_params=_cparams(("arbitrary", "arbitrary")),
        name=name,
    )(*args)


def _gelu_tanh(x):
    return 0.5 * x * (1.0 + jnp.tanh(0.7978845608028654 * (x + 0.044715 * (x * x * x))))


def _rope_epilogue(acc, bias_ref, cos_ref, sin_ref, *, rows, col_tile, k_tile0, k_scale, head_dim):
    z = acc + bias_ref[...]
    cos = cos_ref[rows, :]
    sin = sin_ref[rows, :]
    half = head_dim // 2
    parts = []
    for h0 in range(0, z.shape[1], head_dim):
        x1 = z[:, h0:h0 + half]
        x2 = z[:, h0 + half:h0 + head_dim]
        parts += [x1 * cos - x2 * sin, x2 * cos + x1 * sin]
    scale = jnp.where(col_tile >= k_tile0, k_scale, 1.0).astype(f32)
    return jnp.concatenate(parts, axis=1) * scale


def _lru_gates(xc, n, cs, wa_ref, wx_ref, ba_ref, bx_ref, lam_ref):
    xcb = xc.astype(bf16)
    r = _sigmoid(_dot(xcb, wa_ref[n].astype(bf16)) + ba_ref[:, cs])
    gi = _sigmoid(_dot(xcb, wx_ref[n].astype(bf16)) + bx_ref[:, cs])
    log_a = -LRU_C * r * jax.nn.softplus(-lam_ref[:, cs])
    a = jnp.exp(log_a)
    a2 = a * a
    one_minus = jnp.where(log_a < -0.25, 1.0 - a2, -jnp.tanh(log_a) * (a2 + 1.0))
    return a, one_minus * lax.rsqrt(jnp.maximum(one_minus, 1e-36)), gi


def _lru_prompt_kernel(*refs, n_seq):
    oa_ref = refs[9]

    @pl.when(pl.program_id(0) >= n_seq)
    def _():
        oa_ref[...] = jnp.zeros_like(oa_ref)

    @pl.when(pl.program_id(0) < n_seq)
    def _():
        _lru_prompt_body(*refs)


def _lru_prompt_body(xa_ref, gg_ref, cw_ref, cb_ref, wa_ref, wx_ref, ba_ref, bx_ref, lam_ref,
                     oa_ref, hl_ref, cn_ref, tail_scr, a_scr, b_scr, hc_scr):
    tt, c = xa_ref.shape
    blk = c // N_LRU_BLOCKS
    n_steps = tt // 8
    nw = CONV_W - 1
    t = pl.program_id(1)

    def interleave(x):
        return jnp.swapaxes(x.reshape(8, n_steps, blk), 0, 1).reshape(tt, blk)

    @pl.when(t == 0)
    def _():
        tail_scr[...] = jnp.zeros_like(tail_scr)
        hc_scr[...] = jnp.zeros_like(hc_scr)

    sub = lax.broadcasted_iota(jnp.int32, (8, blk), 0)
    seq_start = lax.broadcasted_iota(jnp.int32, (tt, 1), 0) + t * tt == 0
    for n in range(N_LRU_BLOCKS):
        cs = slice(n * blk, (n + 1) * blk)
        xp = interleave(xa_ref[:, cs])
        lead = []
        for jv in range(nw):
            own = xp[(n_steps - nw + jv) * 8:(n_steps - nw + jv + 1) * 8, :]
            prev = tail_scr[jv * 8:(jv + 1) * 8, cs]
            lead.append(pltpu.roll(jnp.where(sub == 7, prev, own), 1, axis=0))
        xc = cb_ref[:, cs] + cw_ref[nw:nw + 1, cs] * xp
        for s in range(1, CONV_W):
            shifted = jnp.concatenate(lead[nw - s:] + [xp[:tt - 8 * s, :]], axis=0)
            xc = xc + cw_ref[nw - s:nw - s + 1, cs] * shifted
        tail_scr[:, cs] = xp[tt - 8 * nw:, :]
        a, mult, gi = _lru_gates(xc, n, cs, wa_ref, wx_ref, ba_ref, bx_ref, lam_ref)
        mult = jnp.where(seq_start, 1.0, mult)
        a_scr[:, cs] = a
        b_scr[:, cs] = mult * (gi * xc)

    def step(k, carry):
        h, p = carry
        r = pl.ds(pl.multiple_of(k * 8, 8), 8)
        a = a_scr[r, :]
        h = a * h + b_scr[r, :]
        p = a * p
        b_scr[r, :] = h
        a_scr[r, :] = p
        return h, p

    h_end, p_end = lax.fori_loop(0, n_steps, step, (jnp.zeros((8, c), f32), jnp.ones((8, c), f32)), unroll=4)

    sub_c = lax.broadcasted_iota(jnp.int32, (8, c), 0)
    d = 1
    while d < 8:
        keep = sub_c >= d
        h_end = jnp.where(keep, p_end * pltpu.roll(h_end, d, axis=0) + h_end, h_end)
        p_end = jnp.where(keep, p_end * pltpu.roll(p_end, d, axis=0), p_end)
        d *= 2
    h_in = hc_scr[...]
    ends = h_end + p_end * h_in
    starts = jnp.where(sub_c == 0, h_in, pltpu.roll(ends, 1, axis=0))
    hc_scr[...] = ends[7:8, :]
    hl_ref[0] = ends[7:8, :]

    for n in range(N_LRU_BLOCKS):
        cs = slice(n * blk, (n + 1) * blk)
        h3 = b_scr[:, cs].reshape(n_steps, 8, blk) + a_scr[:, cs].reshape(n_steps, 8, blk) * starts[None, :, cs]
        h = jnp.swapaxes(h3, 0, 1).reshape(tt, blk)
        oa_ref[:, cs] = (h * gg_ref[:, cs].astype(f32)).astype(bf16)
    cn_ref[0] = xa_ref[tt - nw:tt, :]


def _lru_sample_kernel(xa_ref, gg_ref, scp_ref, h0_ref, cw_ref, cb_ref, wa_ref, wx_ref, ba_ref, bx_ref, lam_ref,
                       oa_prev_ref, oa_ref, hl_ref, cn_ref):
    del oa_prev_ref
    rows, c = xa_ref.shape
    bt, t_len = scp_ref.shape[0], scp_ref.shape[1]
    blk = c // N_LRU_BLOCKS
    tpos = lax.broadcasted_iota(jnp.int32, (bt, t_len, blk), 1)
    for n in range(N_LRU_BLOCKS):
        cs = slice(n * blk, (n + 1) * blk)
        x3 = xa_ref[:, cs].reshape(bt, t_len, blk)
        ext = jnp.concatenate([scp_ref[:, :, cs], x3], axis=1)
        xc3 = cb_ref[:, cs] + cw_ref[CONV_W - 1:CONV_W, cs] * x3
        for s in range(1, CONV_W):
            xc3 = xc3 + cw_ref[CONV_W - 1 - s:CONV_W - s, cs] * pltpu.roll(ext, s, axis=1)[:, t_len:, :]
        xc = xc3.reshape(rows, blk)
        a, mult, gi = _lru_gates(xc, n, cs, wa_ref, wx_ref, ba_ref, bx_ref, lam_ref)
        a3 = a.reshape(bt, t_len, blk)
        b3 = (mult * (gi * xc)).reshape(bt, t_len, blk)
        d = 1
        while d < t_len:
            keep = tpos >= d
            b3 = jnp.where(keep, a3 * pltpu.roll(b3, d, axis=1) + b3, b3)
            a3 = jnp.where(keep, a3 * pltpu.roll(a3, d, axis=1), a3)
            d *= 2
        h3 = b3 + a3 * h0_ref[:, :, cs]
        oa_ref[:, cs] = (h3.reshape(rows, blk) * gg_ref[:, cs].astype(f32)).astype(bf16)
        hl_ref[:, :, cs] = h3[:, t_len - 1:t_len, :]
        cn_ref[:, :, cs] = pltpu.roll(x3, CONV_W - 1, axis=1)[:, 0:CONV_W - 1, :]


def _ret_norm_gate(o, rn, sg):
    return (o * lax.rsqrt(jnp.mean(o * o, axis=-1, keepdims=True) + EPS) * rn * sg.astype(f32)).astype(bf16)


def _ret_prompt_kernel(*refs, n_seq):
    ob_ref = refs[9]

    @pl.when(pl.program_id(0) >= n_seq)
    def _():
        ob_ref[...] = jnp.zeros_like(ob_ref)

    @pl.when(pl.program_id(0) < n_seq)
    def _():
        _ret_prompt_body(*refs)


def _ret_prompt_body(q_ref, k_ref, v_ref, sg_ref, rn_ref, dm_ref, cd_ref, sd_ref, chd_ref, ob_ref, so_ref):
    c = pl.program_id(1)
    dk, dv = so_ref.shape[2], so_ref.shape[3]

    @pl.when(c == 0)
    def _():
        so_ref[...] = jnp.zeros_like(so_ref)

    for h in range(N_RET_HEADS):
        ks = slice(h * dk, (h + 1) * dk)
        vs = slice(h * dv, (h + 1) * dv)
        q = q_ref[:, ks]
        k = k_ref[:, ks]
        v = v_ref[:, vs]
        s_old = so_ref[0, h]
        scores = _dot_nt(q, k) * dm_ref[h]
        inner = _dot(scores.astype(bf16), v)
        cross = _dot(q, s_old.astype(bf16)) * cd_ref[h]
        kd = (k.astype(f32) * sd_ref[h]).astype(bf16)
        so_ref[0, h] = s_old * chd_ref[h] + _dot_tn(kd, v)
        ob_ref[:, vs] = _ret_norm_gate(inner + cross, rn_ref[h], sg_ref[:, vs])


def _ret_sample_body(q_ref, k_ref, v_ref, sg_ref, s_ref, rn_ref, dm_ref, cd_ref, sd_ref, chd_ref, ob_ref, so_ref,
                     *, t_len, head0):
    rows = q_ref.shape[0]
    n_heads, dk, dv = so_ref.shape[1], so_ref.shape[2], so_ref.shape[3]
    row = lax.broadcasted_iota(jnp.int32, (rows, 1), 0)
    for hl in range(n_heads):
        h = head0 + hl
        ks = slice(hl * dk, (hl + 1) * dk)
        vs = slice(hl * dv, (hl + 1) * dv)
        q = q_ref[:, ks]
        k = k_ref[:, ks]
        v = v_ref[:, vs]
        scores = _dot_nt(q, k) * dm_ref[h]
        o = _dot(scores.astype(bf16), v)
        kd = k.astype(f32) * sd_ref[h]
        for bb in range(rows // t_len):
            mine = (row >= bb * t_len) & (row < (bb + 1) * t_len)
            s_old = s_ref[bb, hl]
            cross = _dot(q, s_old.astype(bf16)) * cd_ref[h]
            o = o + jnp.where(mine, cross, 0.0)
            so_ref[bb, hl] = s_old * chd_ref[h] + _dot_tn(jnp.where(mine, kd, 0.0).astype(bf16), v)
        ob_ref[:, vs] = _ret_norm_gate(o, rn_ref[h], sg_ref[:, vs])


def _gates_ret_kernel(a_ref, b_ref, bias_ref, q_ref, k_ref, v_ref, sg_ref, s_hbm, rn_ref, dm_ref, cd_ref, sd_ref,
                      chd_ref, ob_prev_ref, g_ref, ob_ref, so_ref, bw_scr, s_buf, s_sem, *, t_len, n_units):
    del ob_prev_ref
    i, j = pl.program_id(0), pl.program_id(1)
    n_seq, n_heads = s_buf.shape[1], s_buf.shape[2]
    n_parts = N_RET_HEADS // n_heads
    unit = i * pl.num_programs(1) + j

    def state_copy(u):
        slot = lax.rem(u, S_RING)
        src = s_hbm.at[pl.ds((u // n_parts) * n_seq, n_seq), pl.ds(lax.rem(u, n_parts) * n_heads, n_heads)]
        return pltpu.make_async_copy(src, s_buf.at[slot], s_sem.at[slot])

    @pl.when(unit == 0)
    def _():
        for u0 in range(S_RING - 1):
            state_copy(jnp.int32(u0)).start()

    @pl.when(unit + (S_RING - 1) < n_units)
    def _():
        state_copy(unit + (S_RING - 1)).start()

    @pl.when(i == 0)
    def _():
        bw_scr[j] = b_ref[...].astype(bf16)

    rows = a_ref.shape[0] // MM_ROW_SPLIT
    for r in range(MM_ROW_SPLIT):
        rs = slice(r * rows, (r + 1) * rows)
        g_ref[rs, :] = _sigmoid(_dot(a_ref[rs, :], bw_scr[j]) + bias_ref[...]).astype(g_ref.dtype)

    state_copy(unit).wait()
    _ret_sample_body(q_ref, k_ref, v_ref, sg_ref, s_buf.at[lax.rem(unit, S_RING)], rn_ref, dm_ref, cd_ref, sd_ref,
                     chd_ref, ob_ref, so_ref, t_len=t_len, head0=lax.rem(unit, n_parts) * n_heads)


def _decay_tables(chunk, n_seq):
    log_g = jnp.log1p(-jnp.exp2(-5.0 - jnp.arange(N_RET_HEADS, dtype=f32)))
    r = jnp.arange(chunk * n_seq)
    idx = (r % chunk).astype(f32)
    same = (r[:, None] // chunk) == (r[None, :] // chunk)
    diff = idx[:, None] - idx[None, :]
    dmask = jnp.where(same & (diff >= 0), jnp.exp(jnp.maximum(diff, 0.0)[None] * log_g[:, None, None]), 0.0)
    cross_decay = jnp.exp((idx[None] + 1.0) * log_g[:, None])[..., None]
    state_decay = jnp.exp((chunk - 1.0 - idx[None]) * log_g[:, None])[..., None]
    chunk_decay = jnp.exp(chunk * log_g)[:, None, None]
    return dmask, cross_decay, state_decay, chunk_decay


def _ple_kernel(x_ref, pep_ref, pes_ref, gn_ref, wg_ref, bg_ref, pp_ref, fn_ref, yp_ref, ys_ref, *, n_prompt_tiles, final):
    i = pl.program_id(0)
    x = x_ref[...]
    u = _rms(x, gn_ref[...]).astype(bf16)
    gate = _sigmoid(_dot(u, wg_ref[...]) + bg_ref[...])
    pe = jnp.where(i < n_prompt_tiles, pep_ref[...], pes_ref[...]).astype(bf16)
    x = x + gate * _dot(pe, pp_ref[...].astype(bf16))
    y = _rms(x, fn_ref[...]) if final else x

    @pl.when(i < n_prompt_tiles)
    def _():
        yp_ref[...] = y

    @pl.when(i >= n_prompt_tiles)
    def _():
        ys_ref[...] = y


def _layer(x_parts, pe_p, pe_s, n_prompt, seq, n_sample, dec_seq, h0_s, conv_s, ret_s, cos_t, sin_t, lp, final_norm):
    d = lp['mix_norm'].shape[0]
    mp, ms = n_prompt * seq, n_sample * dec_seq
    m = mp + ms
    d_rnn = lp['conv_w'].shape[1]
    hh = N_RET_HEADS
    dk = d // hh
    dv = lp['ret_norm'].shape[1]
    qk_dim, v_dim = hh * dk, hh * dv
    tm, tn = ROW_TILE, 1024

    ff = lp['ffn1_wg'].shape[1]
    tm1, tf1 = _ffn_tiles(m, d, ff, with_norm=True)
    x1, u = _ffn(x_parts, m, lp['ffn1_norm'], lp['ffn1_wg'], lp['ffn1_wu'], lp['ffn1_wd'], lp['mix_norm'],
                 tm=tm1, tf=tf1)

    w_in = lp['w_in']
    b_in = lp['b_in'].reshape(1, -1)

    def bias_extra(col0):
        return (b_in, (1, tn), lambda j, i, o=col0 // tn: (0, j + o))

    def with_bias(fn):
        return lambda acc, b_ref, rows, col_tile: fn(acc + b_ref[...])

    c_xa, c_ga, c_q, c_v = 0, d_rnn, 2 * d_rnn, 2 * d_rnn + 2 * qk_dim
    c_gr, c_gate = c_v + v_dim, c_v + 2 * v_dim
    tmi = IN_ROW_TILE if m % IN_ROW_TILE == 0 else tm
    proj = functools.partial(_mm, u, w_in, tm=tmi, tn=tn)
    xa = proj(col0=c_xa, n_cols=d_rnn, out_dtype=f32, epilogue=with_bias(lambda z: z),
              extras=[bias_extra(c_xa)], name="inproj_xa")
    gg = proj(col0=c_ga, n_cols=d_rnn, out_dtype=bf16, epilogue=with_bias(_gelu_tanh),
              extras=[bias_extra(c_ga)], name="inproj_ga")
    half = dk // 2
    rope_map = lambda j, i: (i, 0)
    qk = proj(col0=c_q, n_cols=2 * qk_dim, out_dtype=bf16,
              epilogue=functools.partial(_rope_epilogue, k_tile0=qk_dim // tn, k_scale=dk ** -0.5, head_dim=dk),
              extras=[bias_extra(c_q), (cos_t, (tmi, half), rope_map), (sin_t, (tmi, half), rope_map)],
              name="inproj_qk")
    v = proj(col0=c_v, n_cols=v_dim, out_dtype=bf16, epilogue=with_bias(lambda z: z),
             extras=[bias_extra(c_v)], name="inproj_v")
    sg = proj(col0=c_gr, n_cols=v_dim, out_dtype=bf16, epilogue=with_bias(lambda z: z * _sigmoid(z)),
              extras=[bias_extra(c_gr)], name="inproj_gr")

    row2 = lambda a: a.reshape(1, -1)
    lru_w = [lp['conv_w'], row2(lp['conv_b']), lp['lru_wa'], lp['lru_wx'], row2(lp['lru_ba']), row2(lp['lru_bx']),
             row2(lp['lru_lambda'])]
    nb = N_LRU_BLOCKS
    blk = d_rnn // nb
    z2 = lambda *_: (0, 0)
    z3 = lambda *_: (0, 0, 0)
    lru_w_specs = [pl.BlockSpec((CONV_W, d_rnn), z2), pl.BlockSpec((1, d_rnn), z2),
                   pl.BlockSpec((nb, blk, blk), z3), pl.BlockSpec((nb, blk, blk), z3),
                   pl.BlockSpec((1, d_rnn), z2), pl.BlockSpec((1, d_rnn), z2), pl.BlockSpec((1, d_rnn), z2)]

    tt = 512
    ntt = seq // tt
    assert ms % tt == 0 and ms // tt <= ntt
    seq_rows = lambda b, t: (jnp.minimum(b * ntt + t, mp // tt - 1), 0)
    all_rows = lambda b, t: (jnp.minimum(b * ntt + t, m // tt - 1), 0)
    per_seq = lambda b, t: (jnp.minimum(b, n_prompt - 1), 0, 0)
    oa, hl_p, cn_p = pl.pallas_call(
        functools.partial(_lru_prompt_kernel, n_seq=n_prompt),
        grid=(n_prompt + 1, ntt),
        in_specs=[pl.BlockSpec((tt, d_rnn), seq_rows), pl.BlockSpec((tt, d_rnn), seq_rows)] + lru_w_specs,
        out_specs=[pl.BlockSpec((tt, d_rnn), all_rows), pl.BlockSpec((1, 1, d_rnn), per_seq),
                   pl.BlockSpec((1, CONV_W - 1, d_rnn), per_seq)],
        out_shape=[jax.ShapeDtypeStruct((m, d_rnn), bf16), jax.ShapeDtypeStruct((n_prompt, 1, d_rnn), f32),
                   jax.ShapeDtypeStruct((n_prompt, CONV_W - 1, d_rnn), f32)],
        scratch_shapes=[pltpu.VMEM((8 * (CONV_W - 1), d_rnn), f32), pltpu.VMEM((tt, d_rnn), f32),
                        pltpu.VMEM((tt, d_rnn), f32), pltpu.VMEM((1, d_rnn), f32)],
        compiler_params=_cparams(("arbitrary", "arbitrary")),
        name="lru_prompt",
    )(xa, gg, *lru_w)

    bt = 32
    rows_s = bt * dec_seq
    s_off = mp // rows_s
    scp = jnp.pad(conv_s, ((0, 0), (dec_seq - (CONV_W - 1), 0), (0, 0)))
    samp_rows = lambda i: (s_off + i, 0)
    samp3 = lambda i: (i, 0, 0)
    oa, hl_s, cn_s = pl.pallas_call(
        _lru_sample_kernel,
        grid=(n_sample // bt,),
        in_specs=[pl.BlockSpec((rows_s, d_rnn), samp_rows), pl.BlockSpec((rows_s, d_rnn), samp_rows),
                  pl.BlockSpec((bt, dec_seq, d_rnn), samp3), pl.BlockSpec((bt, 1, d_rnn), samp3)] + lru_w_specs + [_ANY],
        out_specs=[pl.BlockSpec((rows_s, d_rnn), samp_rows), pl.BlockSpec((bt, 1, d_rnn), samp3),
                   pl.BlockSpec((bt, CONV_W - 1, d_rnn), samp3)],
        out_shape=[jax.ShapeDtypeStruct((m, d_rnn), bf16), jax.ShapeDtypeStruct((n_sample, 1, d_rnn), f32),
                   jax.ShapeDtypeStruct((n_sample, CONV_W - 1, d_rnn), f32)],
        input_output_aliases={11: 0},
        compiler_params=_cparams(("parallel",)),
        name="lru_sample",
    )(xa, gg, scp, h0_s.reshape(n_sample, 1, d_rnn), *lru_w, oa)

    rn = lp['ret_norm'].reshape(hh, 1, dv)
    ct = RET_CHUNK
    nct = seq // ct
    tabs = _decay_tables(ct, 1)
    full3 = lambda shape: pl.BlockSpec(shape, lambda *_: (0, 0, 0))
    tab_specs = lambda r: [full3((hh, 1, dv)), full3((hh, r, r)), full3((hh, r, 1)), full3((hh, r, 1)), full3((hh, 1, 1))]
    assert ms % ct == 0 and ms // ct <= nct
    chunk_row = lambda b, c: jnp.minimum(b * nct + c, mp // ct - 1)
    ob, so_p = pl.pallas_call(
        functools.partial(_ret_prompt_kernel, n_seq=n_prompt),
        grid=(n_prompt + 1, nct),
        in_specs=[pl.BlockSpec((ct, qk_dim), lambda b, c: (chunk_row(b, c), 0)),
                  pl.BlockSpec((ct, qk_dim), lambda b, c: (chunk_row(b, c), 1)),
                  pl.BlockSpec((ct, v_dim), lambda b, c: (chunk_row(b, c), 0)),
                  pl.BlockSpec((ct, v_dim), lambda b, c: (chunk_row(b, c), 0))] + tab_specs(ct),
        out_specs=[pl.BlockSpec((ct, v_dim), lambda b, c: (jnp.minimum(b * nct + c, m // ct - 1), 0)),
                   pl.BlockSpec((1, hh, dk, dv), lambda b, c: (jnp.minimum(b, n_prompt - 1), 0, 0, 0))],
        out_shape=[jax.ShapeDtypeStruct((m, v_dim), bf16), jax.ShapeDtypeStruct((n_prompt, hh, dk, dv), f32)],
        compiler_params=_cparams(("arbitrary", "arbitrary")),
        name="ret_prompt",
    )(qk, qk, v, sg, rn, *tabs)

    pr = SAMPLE_PAIR
    rows_r = pr * dec_seq
    r_off = mp // rows_r
    tabs = _decay_tables(dec_seq, pr)
    n_parts = 2
    hp = hh // n_parts
    n_units = (n_sample // pr) * n_parts
    tng = GATE_COL_TILE
    njg = 2 * d // tng
    nig = n_units // njg
    tmg = m // nig
    assert nig * njg == n_units >= S_RING and nig * tmg == m and tmg % 16 == 0 and njg % n_parts == 0
    unit = lambda i, j: i * njg + j
    seqs = lambda i, j: unit(i, j) // n_parts
    part = lambda i, j: unit(i, j) % n_parts
    qk_parts = qk_dim // (hp * dk)
    gates, ob, so_s = pl.pallas_call(
        functools.partial(_gates_ret_kernel, t_len=dec_seq, n_units=n_units),
        grid=(nig, njg),
        in_specs=[pl.BlockSpec((tmg, d), lambda i, j: (i, 0)),
                  pl.BlockSpec((d, tng), lambda i, j: (0, jnp.where(i == 0, j, njg - 1) + c_gate // tng)),
                  pl.BlockSpec((1, tng), lambda i, j: (0, j + c_gate // tng)),
                  pl.BlockSpec((rows_r, hp * dk), lambda i, j: (r_off + seqs(i, j), part(i, j))),
                  pl.BlockSpec((rows_r, hp * dk), lambda i, j: (r_off + seqs(i, j), qk_parts + part(i, j))),
                  pl.BlockSpec((rows_r, hp * dv), lambda i, j: (r_off + seqs(i, j), part(i, j))),
                  pl.BlockSpec((rows_r, hp * dv), lambda i, j: (r_off + seqs(i, j), part(i, j))),
                  _ANY] + tab_specs(rows_r) + [_ANY],
        out_specs=[pl.BlockSpec((tmg, tng), lambda i, j: (i, j)),
                   pl.BlockSpec((rows_r, hp * dv), lambda i, j: (r_off + seqs(i, j), part(i, j))),
                   pl.BlockSpec((pr, hp, dk, dv), lambda i, j: (seqs(i, j), part(i, j), 0, 0))],
        out_shape=[jax.ShapeDtypeStruct((m, 2 * d), bf16), jax.ShapeDtypeStruct((m, v_dim), bf16),
                   jax.ShapeDtypeStruct((n_sample, hh, dk, dv), f32)],
        scratch_shapes=[pltpu.VMEM((njg, d, tng), bf16), pltpu.VMEM((S_RING, pr, hp, dk, dv), f32),
                        pltpu.SemaphoreType.DMA((S_RING,))],
        input_output_aliases={13: 1},
        compiler_params=_cparams(("arbitrary", "arbitrary")),
        name="gates_ret_sample",
    )(u, w_in, b_in, qk, qk, v, sg, ret_s, rn, *tabs, ob)

    tile = lambda j, i: (i, j)
    pa = _mm(oa, lp['proj_a'], col0=0, n_cols=d, out_dtype=f32, epilogue=lambda acc, rows, col_tile: acc,
             name="proj_a")
    tnb = 512
    merged = _mm(ob, lp['proj_b'], col0=0, n_cols=d, out_dtype=bf16, tn=tnb, resident=False,
                 epilogue=lambda acc, pa_ref, ga_ref, gb_ref, rows, col_tile: (
                     ga_ref[rows, :].astype(f32) * pa_ref[rows, :] + gb_ref[rows, :].astype(f32) * acc),
                 extras=[(pa, (tm, tnb), tile), (gates, (tm, tnb), tile),
                         (gates, (tm, tnb), lambda j, i: (i, j + d // tnb))], name="proj_b_merge")
    x2 = _mm(merged, lp['w_out'], col0=0, n_cols=d, out_dtype=f32,
             epilogue=lambda acc, x_ref, rows, col_tile: x_ref[rows, :] + acc,
             extras=[(x1, (tm, tn), tile)], name="w_out")

    tm2, tf2 = _ffn_tiles(m, d, ff, with_norm=False)
    x3 = _ffn([(x2, 0)], m, lp['ffn2_norm'], lp['ffn2_wg'], lp['ffn2_wu'], lp['ffn2_wd'], tm=tm2, tf=tf2)

    tp = 512
    ple_dim = pe_p.shape[1]
    fin = final_norm is not None
    fn = (final_norm if fin else lp['ple_norm']).reshape(1, d)
    npt = mp // tp
    c2 = lambda i: (0, 0)
    prow = lambda i: (jnp.minimum(i, npt - 1), 0)
    srow = lambda i: (jnp.maximum(i - npt, 0), 0)
    y_p, y_s = pl.pallas_call(
        functools.partial(_ple_kernel, n_prompt_tiles=npt, final=fin),
        grid=(m // tp,),
        in_specs=[pl.BlockSpec((tp, d), lambda i: (i, 0)), pl.BlockSpec((tp, ple_dim), prow),
                  pl.BlockSpec((tp, ple_dim), srow), pl.BlockSpec((1, d), c2), pl.BlockSpec((d, d), c2),
                  pl.BlockSpec((1, d), c2), pl.BlockSpec((ple_dim, d), c2), pl.BlockSpec((1, d), c2)],
        out_specs=[pl.BlockSpec((tp, d), prow), pl.BlockSpec((tp, d), srow)],
        out_shape=[jax.ShapeDtypeStruct((mp, d), f32), jax.ShapeDtypeStruct((ms, d), f32)],
        compiler_params=_cparams(("arbitrary",)),
        name="ple",
    )(x3, pe_p, pe_s, lp['ple_norm'].reshape(1, d), _to_bf16(lp['ple_wg']), lp['ple_bg'].reshape(1, d),
      lp['ple_proj'], fn)

    states = (hl_p.reshape(n_prompt, d_rnn), cn_p, so_p, hl_s.reshape(n_sample, d_rnn), cn_s, so_s)
    return y_p, y_s, states


def kernel(x_prompt, x_sample, p_prompt, p_sample, state_lru, state_conv, state_ret, ffn1_norm, ffn1_wg, ffn1_wu, ffn1_wd, mix_norm, w_in, b_in, conv_w, conv_b, lru_wa, lru_ba, lru_wx, lru_bx, lru_lambda, ret_norm, proj_a, proj_b, w_out, ffn2_norm, ffn2_wg, ffn2_wu, ffn2_wd, ple_norm, ple_wg, ple_bg, ple_proj, final_norm):
    params = dict(ffn1_norm=ffn1_norm, ffn1_wg=ffn1_wg, ffn1_wu=ffn1_wu, ffn1_wd=ffn1_wd, mix_norm=mix_norm,
                  w_in=w_in, b_in=b_in, conv_w=conv_w, conv_b=conv_b, lru_wa=lru_wa, lru_ba=lru_ba, lru_wx=lru_wx,
                  lru_bx=lru_bx, lru_lambda=lru_lambda, ret_norm=ret_norm, proj_a=proj_a, proj_b=proj_b,
                  w_out=w_out, ffn2_norm=ffn2_norm, ffn2_wg=ffn2_wg, ffn2_wu=ffn2_wu, ffn2_wd=ffn2_wd,
                  ple_norm=ple_norm, ple_wg=ple_wg, ple_bg=ple_bg, ple_proj=ple_proj)
    depth = w_in.shape[0]
    n_prompt, seq, d = x_prompt.shape
    n_sample, dec_seq, _ = x_sample.shape
    mp, ms = n_prompt * seq, n_sample * dec_seq
    dk = d // N_RET_HEADS

    y_p = x_prompt.astype(f32).reshape(mp, d)
    y_s = x_sample.astype(f32).reshape(ms, d)
    cos_t, sin_t = _rope_table(seq, dec_seq, ms, dk // 2)
    cos_t, sin_t = (jnp.concatenate([jnp.tile(t[:seq], (n_prompt, 1)), t[seq:]], axis=0) for t in (cos_t, sin_t))
    outs = [[] for _ in range(6)]
    for i in range(depth):
        lp = {k: v[i].astype(f32) for k, v in params.items()}
        y_p, y_s, st = _layer([(y_p, 0), (y_s, mp)], p_prompt[i].astype(f32).reshape(mp, -1),
                              p_sample[i].astype(f32).reshape(ms, -1), n_prompt, seq, n_sample, dec_seq,
                              state_lru[i].astype(f32), state_conv[i].astype(f32), state_ret[i].astype(f32),
                              cos_t, sin_t, lp, final_norm.astype(f32) if i == depth - 1 else None)
        for o, s in zip(outs, st):
            o.append(s)
    y_prompt = y_p.reshape(n_prompt, seq, d).astype(x_prompt.dtype)
    y_sample = y_s.reshape(n_sample, dec_seq, d).astype(x_sample.dtype)
    lru_p, conv_p, ret_p, lru_s, conv_s, ret_s = (jnp.stack(o) for o in outs)
    return (y_prompt, y_sample, lru_p.astype(state_lru.dtype), conv_p.astype(state_conv.dtype),
            ret_p.astype(state_ret.dtype), lru_s.astype(state_lru.dtype), conv_s.astype(state_conv.dtype),
            ret_s.astype(state_ret.dtype))
```

```python
import functools

import jax
import jax.numpy as jnp
from jax import lax
from jax.experimental import pallas as pl
from jax.experimental.pallas import tpu as pltpu

f32 = jnp.float32
bf16 = jnp.bfloat16

N_LRU_BLOCKS = 8
CONV_W = 4
LRU_C = 8.0
N_RET_HEADS = 8
ROPE_BASE = 10000.0
EPS = 1e-6
PAST_LEN = 16384

RET_CHUNK = 256
SAMPLE_PAIR = 2
ROW_TILE = 1024
IN_ROW_TILE = 1536
GATE_COL_TILE = 256
S_RING = 3
VMEM_LIMIT = 58 * 1024 * 1024


def _cparams(sem):
    return pltpu.CompilerParams(dimension_semantics=sem, vmem_limit_bytes=VMEM_LIMIT)


LOG2E = 1.4426950408889634


def _exp_neg(x):
    return jnp.exp2(x * -LOG2E)


def _sigmoid(x):
    return 1.0 / (1.0 + _exp_neg(x))


def _rms(x, g):
    return x * lax.rsqrt(jnp.mean(x * x, axis=-1, keepdims=True) + EPS) * g


def _dot(a, b):
    return jnp.dot(a, b, preferred_element_type=f32)


def _dot_nt(a, b):
    return lax.dot_general(a, b, (((1,), (1,)), ((), ())), preferred_element_type=f32)


def _dot_tn(a, b):
    return lax.dot_general(a, b, (((0,), (0,)), ((), ())), preferred_element_type=f32)


_ANY = pl.BlockSpec(memory_space=pl.ANY)


def _cast_kernel(x_ref, o_ref):
    o_ref[...] = x_ref[...].astype(o_ref.dtype)


def _to_bf16(w):
    w2 = w.reshape(-1, w.shape[-1])
    return pl.pallas_call(_cast_kernel, out_shape=jax.ShapeDtypeStruct(w2.shape, bf16), name="to_bf16")(w2).reshape(w.shape)


def _rope_table_kernel(inv_ref, cos_ref, sin_ref, *, seq, dec_seq):
    rows = cos_ref.shape[0]
    r = lax.broadcasted_iota(jnp.int32, (rows, inv_ref.shape[1]), 0)
    pos = jnp.where(r < seq, r, PAST_LEN + lax.rem(r - seq, dec_seq))
    ang = pos.astype(f32) * inv_ref[...]
    cos_ref[...] = jnp.cos(ang)
    sin_ref[...] = jnp.sin(ang)


def _rope_table(seq, dec_seq, n_sample_rows, half):
    inv = (ROPE_BASE ** (-jnp.arange(half, dtype=f32) / half)).reshape(1, half)
    rows = seq + n_sample_rows
    return pl.pallas_call(
        functools.partial(_rope_table_kernel, seq=seq, dec_seq=dec_seq),
        out_shape=(jax.ShapeDtypeStruct((rows, half), f32), jax.ShapeDtypeStruct((rows, half), f32)),
        name="rope_table",
    )(inv)


def _ffn_kernel(*refs, group_rows, n_tiles, emit_norm):
    n_groups = len(group_rows)
    x_hbm = refs[:n_groups]
    g_ref, wg_ref, wu_ref, wd_ref = refs[n_groups:n_groups + 4]
    rest = list(refs[n_groups + 4:])
    g2_ref = rest.pop(0) if emit_norm else None
    xo_ref = rest.pop(0)
    u2_ref = rest.pop(0) if emit_norm else None
    (u_scr,) = rest
    i = pl.program_id(0)
    f = pl.program_id(1)
    tm = xo_ref.shape[0]

    @pl.when(f == 0)
    def _():
        for t in range(n_tiles):
            @pl.when(i == t)
            def _(t=t):
                for gi, (r0, nr) in enumerate(group_rows):
                    lo, hi = max(t * tm, r0), min((t + 1) * tm, r0 + nr)
                    if lo < hi:
                        pltpu.sync_copy(x_hbm[gi].at[lo - r0:hi - r0, :], xo_ref.at[lo - t * tm:hi - t * tm, :])

        u_scr[...] = _rms(xo_ref[...], g_ref[...]).astype(bf16)

    u = u_scr[...]
    hs = []
    for c0 in range(0, wg_ref.shape[1], FFN_SUB_TILE):
        cols = slice(c0, c0 + FFN_SUB_TILE)
        hg = _dot(u, wg_ref[:, cols].astype(bf16))
        hu = _dot(u, wu_ref[:, cols].astype(bf16))
        hs.append((0.5 * hg * _sigmoid(hg) * hu).astype(bf16))
    h = hs[0] if len(hs) == 1 else jnp.concatenate(hs, axis=1)
    xo_ref[...] += _dot(h, wd_ref[...].astype(bf16))

    if emit_norm:
        @pl.when(f == pl.num_programs(1) - 1)
        def _():
            u2_ref[...] = _rms(xo_ref[...], g2_ref[...]).astype(bf16)


FFN_SUB_TILE = 256
FFN_VMEM_BUDGET = 60 * 1024 * 1024


def _ffn_tiles(m, d, ff, with_norm):
    sub = FFN_SUB_TILE
    for tm in range(m, 1023, -256):
        if m % tm:
            continue
        for tf in (4 * sub, 2 * sub, sub):
            if ff % tf == 0:
                rows = tm * d * (4 + 2 + (2 if with_norm else 0))
                weights = 3 * d * tf * 4 * 2
                temps = 2 * tm * sub * 4 + tm * tf * 2 + (2 * d * sub + tf * d) * 2
                if rows + weights + temps <= FFN_VMEM_BUDGET:
                    return tm, tf
    raise ValueError("no FFN tiling fits VMEM")


def _ffn(parts, m_total, g, wg, wu, wd, g2=None, *, tm, tf):
    d = wg.shape[0]
    ff = wg.shape[1]
    emit_norm = g2 is not None
    row = lambda i, f: (i, 0)
    out_mode = dict(pipeline_mode=pl.Buffered(1))
    in_specs = [_ANY] * len(parts) + [
        pl.BlockSpec((1, d), lambda i, f: (0, 0)),
        pl.BlockSpec((d, tf), lambda i, f: (0, f)),
        pl.BlockSpec((d, tf), lambda i, f: (0, f)),
        pl.BlockSpec((tf, d), lambda i, f: (f, 0)),
    ]
    args = [xp for xp, _ in parts] + [g.reshape(1, d), wg, wu, wd]
    out_shape = [jax.ShapeDtypeStruct((m_total, d), f32)]
    out_specs = [pl.BlockSpec((tm, d), row, **out_mode)]
    if emit_norm:
        in_specs.append(pl.BlockSpec((1, d), lambda i, f: (0, 0)))
        args.append(g2.reshape(1, d))
        out_shape.append(jax.ShapeDtypeStruct((m_total, d), bf16))
        out_specs.append(pl.BlockSpec((tm, d), row, **out_mode))
    outs = pl.pallas_call(
        functools.partial(_ffn_kernel, group_rows=tuple((r0, xp.shape[0]) for xp, r0 in parts),
                          n_tiles=m_total // tm, emit_norm=emit_norm),
        grid=(m_total // tm, ff // tf),
        in_specs=in_specs,
        out_specs=out_specs,
        out_shape=out_shape,
        scratch_shapes=[pltpu.VMEM((tm, d), bf16)],
        compiler_params=_cparams(("arbitrary", "arbitrary")),
        name="ffn",
    )(*args)
    return outs if emit_norm else outs[0]


MM_ROW_SPLIT = 2


def _mm_kernel(a_ref, b_ref, *rest, epilogue, col_axis):
    *extra, o_ref, bw_scr = rest
    j = pl.program_id(col_axis)
    slot = j if col_axis == 1 else 0

    @pl.when(pl.program_id(1 - col_axis) == 0)
    def _():
        bw_scr[slot] = b_ref[...].astype(bf16)

    rows = a_ref.shape[0] // MM_ROW_SPLIT
    for r in range(MM_ROW_SPLIT):
        rs = slice(r * rows, (r + 1) * rows)
        o_ref[rs, :] = epilogue(_dot(a_ref[rs, :], bw_scr[slot]), *extra, rows=rs, col_tile=j).astype(o_ref.dtype)


def _mm(a, b, *, col0, n_cols, out_dtype, epilogue, extras=(), tm=ROW_TILE, tn=1024, resident=True, name="mm"):
    m, k = a.shape
    off = col0 // tn
    nj = n_cols // tn
    if resident:
        grid, order = (m // tm, nj), (lambda f: (lambda i, j: f(j, i)))
        b_map = lambda j, i: (0, jnp.where(i == 0, j, nj - 1) + off)
    else:
        grid, order = (nj, m // tm), (lambda f: f)
        b_map = lambda j, i: (0, j + off)
    in_specs = [pl.BlockSpec((tm, k), order(lambda j, i: (i, 0))), pl.BlockSpec((k, tn), order(b_map))]
    args = [a, b]
    for arr, blk, imap in extras:
        in_specs.append(pl.BlockSpec(blk, order(imap)))
        args.append(arr)
    return pl.pallas_call(
        functools.partial(_mm_kernel, epilogue=epilogue, col_axis=1 if resident else 0),
        grid=grid,
        in_specs=in_specs,
        out_specs=pl.BlockSpec((tm, tn), order(lambda j, i: (i, j))),
        out_shape=jax.ShapeDtypeStruct((m, n_cols), out_dtype),
        scratch_shapes=[pltpu.VMEM((nj if resident else 1, k, tn), bf16)],
        compiler_params=_cparams(("arbitrary", "arbitrary")),
        name=name,
    )(*args)


def _gelu_tanh(x):
    half_x = 0.5 * x
    return half_x + half_x * jnp.tanh(x * (0.7978845608028654 + 0.7978845608028654 * 0.044715 * (x * x)))


def _rope_epilogue(acc, bias_ref, cos_ref, sin_ref, *, rows, col_tile, k_tile0, k_scale, head_dim):
    z = acc + bias_ref[...]
    cos = cos_ref[rows, :]
    sin = sin_ref[rows, :]
    half = head_dim // 2
    parts = []
    for h0 in range(0, z.shape[1], head_dim):
        x1 = z[:, h0:h0 + half]
        x2 = z[:, h0 + half:h0 + head_dim]
        parts += [x1 * cos - x2 * sin, x2 * cos + x1 * sin]
    scale = jnp.where(col_tile >= k_tile0, k_scale, 1.0).astype(f32)
    return jnp.concatenate(parts, axis=1) * scale


def _lru_gates(xc, n, cs, wa_ref, wx_ref, ba_ref, bx_ref, lam_ref):
    xcb = xc.astype(bf16)
    r = _sigmoid(_dot(xcb, wa_ref[n].astype(bf16)) + ba_ref[:, cs])
    gi = _sigmoid(_dot(xcb, wx_ref[n].astype(bf16)) + bx_ref[:, cs])
    q = r * (LRU_C * jax.nn.softplus(-lam_ref[:, cs]))
    a = _exp_neg(q)
    one_minus = jnp.tanh(q) * (a * a + 1.0)
    return a, one_minus * lax.rsqrt(jnp.maximum(one_minus, 1e-36)), gi


def _lru_prompt_kernel(*refs, n_seq):
    oa_ref = refs[9]

    @pl.when(pl.program_id(0) >= n_seq)
    def _():
        oa_ref[...] = jnp.zeros_like(oa_ref)

    @pl.when(pl.program_id(0) < n_seq)
    def _():
        _lru_prompt_body(*refs)


def _lru_prompt_body(xa_ref, gg_ref, cw_ref, cb_ref, wa_ref, wx_ref, ba_ref, bx_ref, lam_ref,
                     oa_ref, hl_ref, cn_ref, tail_scr, a_scr, b_scr, hc_scr):
    tt, c = xa_ref.shape
    blk = c // N_LRU_BLOCKS
    n_steps = tt // 8
    nw = CONV_W - 1
    t = pl.program_id(1)

    def interleave(x):
        return jnp.swapaxes(x.reshape(8, n_steps, blk), 0, 1).reshape(tt, blk)

    @pl.when(t == 0)
    def _():
        tail_scr[...] = jnp.zeros_like(tail_scr)
        hc_scr[...] = jnp.zeros_like(hc_scr)

    sub = lax.broadcasted_iota(jnp.int32, (8, blk), 0)
    seq_start = lax.broadcasted_iota(jnp.int32, (tt, 1), 0) + t * tt == 0
    for n in range(N_LRU_BLOCKS):
        cs = slice(n * blk, (n + 1) * blk)
        xp = interleave(xa_ref[:, cs])
        lead = []
        for jv in range(nw):
            own = xp[(n_steps - nw + jv) * 8:(n_steps - nw + jv + 1) * 8, :]
            prev = tail_scr[jv * 8:(jv + 1) * 8, cs]
            lead.append(pltpu.roll(jnp.where(sub == 7, prev, own), 1, axis=0))
        xc = cb_ref[:, cs] + cw_ref[nw:nw + 1, cs] * xp
        for s in range(1, CONV_W):
            shifted = jnp.concatenate(lead[nw - s:] + [xp[:tt - 8 * s, :]], axis=0)
            xc = xc + cw_ref[nw - s:nw - s + 1, cs] * shifted
        tail_scr[:, cs] = xp[tt - 8 * nw:, :]
        a, mult, gi = _lru_gates(xc, n, cs, wa_ref, wx_ref, ba_ref, bx_ref, lam_ref)
        mult = jnp.where(seq_start, 1.0, mult)
        a_scr[:, cs] = a
        b_scr[:, cs] = mult * (gi * xc)

    def step(k, carry):
        h, p = carry
        r = pl.ds(pl.multiple_of(k * 8, 8), 8)
        a = a_scr[r, :]
        h = a * h + b_scr[r, :]
        p = a * p
        b_scr[r, :] = h
        a_scr[r, :] = p
        return h, p

    h_end, p_end = lax.fori_loop(0, n_steps, step, (jnp.zeros((8, c), f32), jnp.ones((8, c), f32)), unroll=4)

    sub_c = lax.broadcasted_iota(jnp.int32, (8, c), 0)
    d = 1
    while d < 8:
        keep = sub_c >= d
        h_end = jnp.where(keep, p_end * pltpu.roll(h_end, d, axis=0) + h_end, h_end)
        p_end = jnp.where(keep, p_end * pltpu.roll(p_end, d, axis=0), p_end)
        d *= 2
    h_in = hc_scr[...]
    ends = h_end + p_end * h_in
    starts = jnp.where(sub_c == 0, h_in, pltpu.roll(ends, 1, axis=0))
    hc_scr[...] = ends[7:8, :]
    hl_ref[0] = ends[7:8, :]

    for n in range(N_LRU_BLOCKS):
        cs = slice(n * blk, (n + 1) * blk)
        h3 = b_scr[:, cs].reshape(n_steps, 8, blk) + a_scr[:, cs].reshape(n_steps, 8, blk) * starts[None, :, cs]
        h = jnp.swapaxes(h3, 0, 1).reshape(tt, blk)
        oa_ref[:, cs] = (h * gg_ref[:, cs].astype(f32)).astype(bf16)
    cn_ref[0] = xa_ref[tt - nw:tt, :]


def _lru_sample_kernel(xa_ref, gg_ref, scp_ref, h0_ref, cw_ref, cb_ref, wa_ref, wx_ref, ba_ref, bx_ref, lam_ref,
                       oa_prev_ref, oa_ref, hl_ref, cn_ref):
    del oa_prev_ref
    rows, c = xa_ref.shape
    bt, t_len = scp_ref.shape[0], scp_ref.shape[1]
    blk = c // N_LRU_BLOCKS
    tpos = lax.broadcasted_iota(jnp.int32, (bt, t_len, blk), 1)
    for n in range(N_LRU_BLOCKS):
        cs = slice(n * blk, (n + 1) * blk)
        x3 = xa_ref[:, cs].reshape(bt, t_len, blk)
        ext = jnp.concatenate([scp_ref[:, :, cs], x3], axis=1)
        xc3 = cb_ref[:, cs] + cw_ref[CONV_W - 1:CONV_W, cs] * x3
        for s in range(1, CONV_W):
            xc3 = xc3 + cw_ref[CONV_W - 1 - s:CONV_W - s, cs] * pltpu.roll(ext, s, axis=1)[:, t_len:, :]
        xc = xc3.reshape(rows, blk)
        a, mult, gi = _lru_gates(xc, n, cs, wa_ref, wx_ref, ba_ref, bx_ref, lam_ref)
        a3 = a.reshape(bt, t_len, blk)
        b3 = (mult * (gi * xc)).reshape(bt, t_len, blk)
        d = 1
        while d < t_len:
            keep = tpos >= d
            b3 = jnp.where(keep, a3 * pltpu.roll(b3, d, axis=1) + b3, b3)
            a3 = jnp.where(keep, a3 * pltpu.roll(a3, d, axis=1), a3)
            d *= 2
        h3 = b3 + a3 * h0_ref[:, :, cs]
        oa_ref[:, cs] = (h3.reshape(rows, blk) * gg_ref[:, cs].astype(f32)).astype(bf16)
        hl_ref[:, :, cs] = h3[:, t_len - 1:t_len, :]
        cn_ref[:, :, cs] = pltpu.roll(x3, CONV_W - 1, axis=1)[:, 0:CONV_W - 1, :]


def _ret_norm_gate(o, rn, sg):
    return (o * lax.rsqrt(jnp.mean(o * o, axis=-1, keepdims=True) + EPS) * rn * sg.astype(f32)).astype(bf16)


def _ret_prompt_kernel(*refs, n_seq):
    ob_ref = refs[9]

    @pl.when(pl.program_id(0) >= n_seq)
    def _():
        ob_ref[...] = jnp.zeros_like(ob_ref)

    @pl.when(pl.program_id(0) < n_seq)
    def _():
        _ret_prompt_body(*refs)


def _ret_prompt_body(q_ref, k_ref, v_ref, sg_ref, rn_ref, dm_ref, cd_ref, sd_ref, chd_ref, ob_ref, so_ref):
    c = pl.program_id(1)
    dk, dv = so_ref.shape[2], so_ref.shape[3]

    @pl.when(c == 0)
    def _():
        so_ref[...] = jnp.zeros_like(so_ref)

    for h in range(N_RET_HEADS):
        ks = slice(h * dk, (h + 1) * dk)
        vs = slice(h * dv, (h + 1) * dv)
        q = q_ref[:, ks]
        k = k_ref[:, ks]
        v = v_ref[:, vs]
        s_old = so_ref[0, h]
        scores = _dot_nt(q, k) * dm_ref[h]
        inner = _dot(scores.astype(bf16), v)
        cross = _dot(q, s_old.astype(bf16)) * cd_ref[h]
        kd = (k.astype(f32) * sd_ref[h]).astype(bf16)
        so_ref[0, h] = s_old * chd_ref[h] + _dot_tn(kd, v)
        ob_ref[:, vs] = _ret_norm_gate(inner + cross, rn_ref[h], sg_ref[:, vs])


def _ret_sample_body(q_ref, k_ref, v_ref, sg_ref, s_ref, rn_ref, dm_ref, cd_ref, sd_ref, chd_ref, ob_ref, so_ref,
                     *, t_len, head0):
    rows = q_ref.shape[0]
    n_heads, dk, dv = so_ref.shape[1], so_ref.shape[2], so_ref.shape[3]
    row = lax.broadcasted_iota(jnp.int32, (rows, 1), 0)
    for hl in range(n_heads):
        h = head0 + hl
        ks = slice(hl * dk, (hl + 1) * dk)
        vs = slice(hl * dv, (hl + 1) * dv)
        q = q_ref[:, ks]
        k = k_ref[:, ks]
        v = v_ref[:, vs]
        scores = _dot_nt(q, k) * dm_ref[h]
        o = _dot(scores.astype(bf16), v)
        kd = k.astype(f32) * sd_ref[h]
        for bb in range(rows // t_len):
            mine = (row >= bb * t_len) & (row < (bb + 1) * t_len)
            s_old = s_ref[bb, hl]
            cross = _dot(q, s_old.astype(bf16)) * cd_ref[h]
            o = o + jnp.where(mine, cross, 0.0)
            so_ref[bb, hl] = s_old * chd_ref[h] + _dot_tn(jnp.where(mine, kd, 0.0).astype(bf16), v)
        ob_ref[:, vs] = _ret_norm_gate(o, rn_ref[h], sg_ref[:, vs])


def _gates_ret_kernel(a_ref, b_ref, bias_ref, q_ref, k_ref, v_ref, sg_ref, s_hbm, rn_ref, dm_ref, cd_ref, sd_ref,
                      chd_ref, ob_prev_ref, g_ref, ob_ref, so_ref, bw_scr, s_buf, s_sem, *, t_len, n_units):
    del ob_prev_ref
    i, j = pl.program_id(0), pl.program_id(1)
    n_seq, n_heads = s_buf.shape[1], s_buf.shape[2]
    n_parts = N_RET_HEADS // n_heads
    unit = i * pl.num_programs(1) + j

    def state_copy(u):
        slot = lax.rem(u, S_RING)
        src = s_hbm.at[pl.ds((u // n_parts) * n_seq, n_seq), pl.ds(lax.rem(u, n_parts) * n_heads, n_heads)]
        return pltpu.make_async_copy(src, s_buf.at[slot], s_sem.at[slot])

    @pl.when(unit == 0)
    def _():
        for u0 in range(S_RING - 1):
            state_copy(jnp.int32(u0)).start()

    @pl.when(unit + (S_RING - 1) < n_units)
    def _():
        state_copy(unit + (S_RING - 1)).start()

    @pl.when(i == 0)
    def _():
        bw_scr[j] = b_ref[...].astype(bf16)

    rows = a_ref.shape[0] // MM_ROW_SPLIT
    for r in range(MM_ROW_SPLIT):
        rs = slice(r * rows, (r + 1) * rows)
        g_ref[rs, :] = _sigmoid(_dot(a_ref[rs, :], bw_scr[j]) + bias_ref[...]).astype(g_ref.dtype)

    state_copy(unit).wait()
    _ret_sample_body(q_ref, k_ref, v_ref, sg_ref, s_buf.at[lax.rem(unit, S_RING)], rn_ref, dm_ref, cd_ref, sd_ref,
                     chd_ref, ob_ref, so_ref, t_len=t_len, head0=lax.rem(unit, n_parts) * n_heads)


def _decay_tables(chunk, n_seq):
    log_g = jnp.log1p(-jnp.exp2(-5.0 - jnp.arange(N_RET_HEADS, dtype=f32)))
    r = jnp.arange(chunk * n_seq)
    idx = (r % chunk).astype(f32)
    same = (r[:, None] // chunk) == (r[None, :] // chunk)
    diff = idx[:, None] - idx[None, :]
    dmask = jnp.where(same & (diff >= 0), jnp.exp(jnp.maximum(diff, 0.0)[None] * log_g[:, None, None]), 0.0)
    cross_decay = jnp.exp((idx[None] + 1.0) * log_g[:, None])[..., None]
    state_decay = jnp.exp((chunk - 1.0 - idx[None]) * log_g[:, None])[..., None]
    chunk_decay = jnp.exp(chunk * log_g)[:, None, None]
    return dmask, cross_decay, state_decay, chunk_decay


def _ple_kernel(x_ref, pep_ref, pes_ref, gn_ref, wg_ref, bg_ref, pp_ref, fn_ref, yp_ref, ys_ref, *, n_prompt_tiles, final):
    i = pl.program_id(0)
    x = x_ref[...]
    u = _rms(x, gn_ref[...]).astype(bf16)
    gate = _sigmoid(_dot(u, wg_ref[...]) + bg_ref[...])
    pe = jnp.where(i < n_prompt_tiles, pep_ref[...], pes_ref[...]).astype(bf16)
    x = x + gate * _dot(pe, pp_ref[...].astype(bf16))
    y = _rms(x, fn_ref[...]) if final else x

    @pl.when(i < n_prompt_tiles)
    def _():
        yp_ref[...] = y

    @pl.when(i >= n_prompt_tiles)
    def _():
        ys_ref[...] = y


def _layer(x_parts, pe_p, pe_s, n_prompt, seq, n_sample, dec_seq, h0_s, conv_s, ret_s, cos_t, sin_t, lp, final_norm):
    d = lp['mix_norm'].shape[0]
    mp, ms = n_prompt * seq, n_sample * dec_seq
    m = mp + ms
    d_rnn = lp['conv_w'].shape[1]
    hh = N_RET_HEADS
    dk = d // hh
    dv = lp['ret_norm'].shape[1]
    qk_dim, v_dim = hh * dk, hh * dv
    tm, tn = ROW_TILE, 1024

    ff = lp['ffn1_wg'].shape[1]
    tm1, tf1 = _ffn_tiles(m, d, ff, with_norm=True)
    x1, u = _ffn(x_parts, m, lp['ffn1_norm'], lp['ffn1_wg'], lp['ffn1_wu'], lp['ffn1_wd'], lp['mix_norm'],
                 tm=tm1, tf=tf1)

    w_in = lp['w_in']
    b_in = lp['b_in'].reshape(1, -1)

    def bias_extra(col0):
        return (b_in, (1, tn), lambda j, i, o=col0 // tn: (0, j + o))

    def with_bias(fn):
        return lambda acc, b_ref, rows, col_tile: fn(acc + b_ref[...])

    c_xa, c_ga, c_q, c_v = 0, d_rnn, 2 * d_rnn, 2 * d_rnn + 2 * qk_dim
    c_gr, c_gate = c_v + v_dim, c_v + 2 * v_dim
    tmi = IN_ROW_TILE if m % IN_ROW_TILE == 0 else tm
    proj = functools.partial(_mm, u, w_in, tm=tmi, tn=tn)
    xa = proj(col0=c_xa, n_cols=d_rnn, out_dtype=f32, epilogue=with_bias(lambda z: z),
              extras=[bias_extra(c_xa)], name="inproj_xa")
    gg = proj(col0=c_ga, n_cols=d_rnn, out_dtype=bf16, epilogue=with_bias(_gelu_tanh),
              extras=[bias_extra(c_ga)], name="inproj_ga")
    half = dk // 2
    rope_map = lambda j, i: (i, 0)
    qk = proj(col0=c_q, n_cols=2 * qk_dim, out_dtype=bf16,
              epilogue=functools.partial(_rope_epilogue, k_tile0=qk_dim // tn, k_scale=dk ** -0.5, head_dim=dk),
              extras=[bias_extra(c_q), (cos_t, (tmi, half), rope_map), (sin_t, (tmi, half), rope_map)],
              name="inproj_qk")
    v = proj(col0=c_v, n_cols=v_dim, out_dtype=bf16, epilogue=with_bias(lambda z: z),
             extras=[bias_extra(c_v)], name="inproj_v")
    sg = proj(col0=c_gr, n_cols=v_dim, out_dtype=bf16, epilogue=with_bias(lambda z: z * _sigmoid(z)),
              extras=[bias_extra(c_gr)], name="inproj_gr")

    row2 = lambda a: a.reshape(1, -1)
    lru_w = [lp['conv_w'], row2(lp['conv_b']), lp['lru_wa'], lp['lru_wx'], row2(lp['lru_ba']), row2(lp['lru_bx']),
             row2(lp['lru_lambda'])]
    nb = N_LRU_BLOCKS
    blk = d_rnn // nb
    z2 = lambda *_: (0, 0)
    z3 = lambda *_: (0, 0, 0)
    lru_w_specs = [pl.BlockSpec((CONV_W, d_rnn), z2), pl.BlockSpec((1, d_rnn), z2),
                   pl.BlockSpec((nb, blk, blk), z3), pl.BlockSpec((nb, blk, blk), z3),
                   pl.BlockSpec((1, d_rnn), z2), pl.BlockSpec((1, d_rnn), z2), pl.BlockSpec((1, d_rnn), z2)]

    tt = 512
    ntt = seq // tt
    assert ms % tt == 0 and ms // tt <= ntt
    seq_rows = lambda b, t: (jnp.minimum(b * ntt + t, mp // tt - 1), 0)
    all_rows = lambda b, t: (jnp.minimum(b * ntt + t, m // tt - 1), 0)
    per_seq = lambda b, t: (jnp.minimum(b, n_prompt - 1), 0, 0)
    oa, hl_p, cn_p = pl.pallas_call(
        functools.partial(_lru_prompt_kernel, n_seq=n_prompt),
        grid=(n_prompt + 1, ntt),
        in_specs=[pl.BlockSpec((tt, d_rnn), seq_rows), pl.BlockSpec((tt, d_rnn), seq_rows)] + lru_w_specs,
        out_specs=[pl.BlockSpec((tt, d_rnn), all_rows), pl.BlockSpec((1, 1, d_rnn), per_seq),
                   pl.BlockSpec((1, CONV_W - 1, d_rnn), per_seq)],
        out_shape=[jax.ShapeDtypeStruct((m, d_rnn), bf16), jax.ShapeDtypeStruct((n_prompt, 1, d_rnn), f32),
                   jax.ShapeDtypeStruct((n_prompt, CONV_W - 1, d_rnn), f32)],
        scratch_shapes=[pltpu.VMEM((8 * (CONV_W - 1), d_rnn), f32), pltpu.VMEM((tt, d_rnn), f32),
                        pltpu.VMEM((tt, d_rnn), f32), pltpu.VMEM((1, d_rnn), f32)],
        compiler_params=_cparams(("arbitrary", "arbitrary")),
        name="lru_prompt",
    )(xa, gg, *lru_w)

    bt = 32
    rows_s = bt * dec_seq
    s_off = mp // rows_s
    scp = jnp.pad(conv_s, ((0, 0), (dec_seq - (CONV_W - 1), 0), (0, 0)))
    samp_rows = lambda i: (s_off + i, 0)
    samp3 = lambda i: (i, 0, 0)
    oa, hl_s, cn_s = pl.pallas_call(
        _lru_sample_kernel,
        grid=(n_sample // bt,),
        in_specs=[pl.BlockSpec((rows_s, d_rnn), samp_rows), pl.BlockSpec((rows_s, d_rnn), samp_rows),
                  pl.BlockSpec((bt, dec_seq, d_rnn), samp3), pl.BlockSpec((bt, 1, d_rnn), samp3)] + lru_w_specs + [_ANY],
        out_specs=[pl.BlockSpec((rows_s, d_rnn), samp_rows), pl.BlockSpec((bt, 1, d_rnn), samp3),
                   pl.BlockSpec((bt, CONV_W - 1, d_rnn), samp3)],
        out_shape=[jax.ShapeDtypeStruct((m, d_rnn), bf16), jax.ShapeDtypeStruct((n_sample, 1, d_rnn), f32),
                   jax.ShapeDtypeStruct((n_sample, CONV_W - 1, d_rnn), f32)],
        input_output_aliases={11: 0},
        compiler_params=_cparams(("parallel",)),
        name="lru_sample",
    )(xa, gg, scp, h0_s.reshape(n_sample, 1, d_rnn), *lru_w, oa)

    rn = lp['ret_norm'].reshape(hh, 1, dv)
    ct = RET_CHUNK
    nct = seq // ct
    tabs = _decay_tables(ct, 1)
    full3 = lambda shape: pl.BlockSpec(shape, lambda *_: (0, 0, 0))
    tab_specs = lambda r: [full3((hh, 1, dv)), full3((hh, r, r)), full3((hh, r, 1)), full3((hh, r, 1)), full3((hh, 1, 1))]
    assert ms % ct == 0 and ms // ct <= nct
    chunk_row = lambda b, c: jnp.minimum(b * nct + c, mp // ct - 1)
    ob, so_p = pl.pallas_call(
        functools.partial(_ret_prompt_kernel, n_seq=n_prompt),
        grid=(n_prompt + 1, nct),
        in_specs=[pl.BlockSpec((ct, qk_dim), lambda b, c: (chunk_row(b, c), 0)),
                  pl.BlockSpec((ct, qk_dim), lambda b, c: (chunk_row(b, c), 1)),
                  pl.BlockSpec((ct, v_dim), lambda b, c: (chunk_row(b, c), 0)),
                  pl.BlockSpec((ct, v_dim), lambda b, c: (chunk_row(b, c), 0))] + tab_specs(ct),
        out_specs=[pl.BlockSpec((ct, v_dim), lambda b, c: (jnp.minimum(b * nct + c, m // ct - 1), 0)),
                   pl.BlockSpec((1, hh, dk, dv), lambda b, c: (jnp.minimum(b, n_prompt - 1), 0, 0, 0))],
        out_shape=[jax.ShapeDtypeStruct((m, v_dim), bf16), jax.ShapeDtypeStruct((n_prompt, hh, dk, dv), f32)],
        compiler_params=_cparams(("arbitrary", "arbitrary")),
        name="ret_prompt",
    )(qk, qk, v, sg, rn, *tabs)

    pr = SAMPLE_PAIR
    rows_r = pr * dec_seq
    r_off = mp // rows_r
    tabs = _decay_tables(dec_seq, pr)
    n_parts = 2
    hp = hh // n_parts
    n_units = (n_sample // pr) * n_parts
    tng = GATE_COL_TILE
    njg = 2 * d // tng
    nig = n_units // njg
    tmg = m // nig
    assert nig * njg == n_units >= S_RING and nig * tmg == m and tmg % 16 == 0 and njg % n_parts == 0
    unit = lambda i, j: i * njg + j
    seqs = lambda i, j: unit(i, j) // n_parts
    part = lambda i, j: unit(i, j) % n_parts
    qk_parts = qk_dim // (hp * dk)
    gates, ob, so_s = pl.pallas_call(
        functools.partial(_gates_ret_kernel, t_len=dec_seq, n_units=n_units),
        grid=(nig, njg),
        in_specs=[pl.BlockSpec((tmg, d), lambda i, j: (i, 0)),
                  pl.BlockSpec((d, tng), lambda i, j: (0, jnp.where(i == 0, j, njg - 1) + c_gate // tng)),
                  pl.BlockSpec((1, tng), lambda i, j: (0, j + c_gate // tng)),
                  pl.BlockSpec((rows_r, hp * dk), lambda i, j: (r_off + seqs(i, j), part(i, j))),
                  pl.BlockSpec((rows_r, hp * dk), lambda i, j: (r_off + seqs(i, j), qk_parts + part(i, j))),
                  pl.BlockSpec((rows_r, hp * dv), lambda i, j: (r_off + seqs(i, j), part(i, j))),
                  pl.BlockSpec((rows_r, hp * dv), lambda i, j: (r_off + seqs(i, j), part(i, j))),
                  _ANY] + tab_specs(rows_r) + [_ANY],
        out_specs=[pl.BlockSpec((tmg, tng), lambda i, j: (i, j)),
                   pl.BlockSpec((rows_r, hp * dv), lambda i, j: (r_off + seqs(i, j), part(i, j))),
                   pl.BlockSpec((pr, hp, dk, dv), lambda i, j: (seqs(i, j), part(i, j), 0, 0))],
        out_shape=[jax.ShapeDtypeStruct((m, 2 * d), bf16), jax.ShapeDtypeStruct((m, v_dim), bf16),
                   jax.ShapeDtypeStruct((n_sample, hh, dk, dv), f32)],
        scratch_shapes=[pltpu.VMEM((njg, d, tng), bf16), pltpu.VMEM((S_RING, pr, hp, dk, dv), f32),
                        pltpu.SemaphoreType.DMA((S_RING,))],
        input_output_aliases={13: 1},
        compiler_params=_cparams(("arbitrary", "arbitrary")),
        name="gates_ret_sample",
    )(u, w_in, b_in, qk, qk, v, sg, ret_s, rn, *tabs, ob)

    tile = lambda j, i: (i, j)
    pa = _mm(oa, lp['proj_a'], col0=0, n_cols=d, out_dtype=f32, epilogue=lambda acc, rows, col_tile: acc,
             name="proj_a")
    tnb = 512
    merged = _mm(ob, lp['proj_b'], col0=0, n_cols=d, out_dtype=bf16, tn=tnb, resident=False,
                 epilogue=lambda acc, pa_ref, ga_ref, gb_ref, rows, col_tile: (
                     ga_ref[rows, :].astype(f32) * pa_ref[rows, :] + gb_ref[rows, :].astype(f32) * acc),
                 extras=[(pa, (tm, tnb), tile), (gates, (tm, tnb), tile),
                         (gates, (tm, tnb), lambda j, i: (i, j + d // tnb))], name="proj_b_merge")
    x2 = _mm(merged, lp['w_out'], col0=0, n_cols=d, out_dtype=f32,
             epilogue=lambda acc, x_ref, rows, col_tile: x_ref[rows, :] + acc,
             extras=[(x1, (tm, tn), tile)], name="w_out")

    tm2, tf2 = _ffn_tiles(m, d, ff, with_norm=False)
    x3 = _ffn([(x2, 0)], m, lp['ffn2_norm'], lp['ffn2_wg'], lp['ffn2_wu'], lp['ffn2_wd'], tm=tm2, tf=tf2)

    tp = 512
    ple_dim = pe_p.shape[1]
    fin = final_norm is not None
    fn = (final_norm if fin else lp['ple_norm']).reshape(1, d)
    npt = mp // tp
    c2 = lambda i: (0, 0)
    prow = lambda i: (jnp.minimum(i, npt - 1), 0)
    srow = lambda i: (jnp.maximum(i - npt, 0), 0)
    y_p, y_s = pl.pallas_call(
        functools.partial(_ple_kernel, n_prompt_tiles=npt, final=fin),
        grid=(m // tp,),
        in_specs=[pl.BlockSpec((tp, d), lambda i: (i, 0)), pl.BlockSpec((tp, ple_dim), prow),
                  pl.BlockSpec((tp, ple_dim), srow), pl.BlockSpec((1, d), c2), pl.BlockSpec((d, d), c2),
                  pl.BlockSpec((1, d), c2), pl.BlockSpec((ple_dim, d), c2), pl.BlockSpec((1, d), c2)],
        out_specs=[pl.BlockSpec((tp, d), prow), pl.BlockSpec((tp, d), srow)],
        out_shape=[jax.ShapeDtypeStruct((mp, d), f32), jax.ShapeDtypeStruct((ms, d), f32)],
        compiler_params=_cparams(("arbitrary",)),
        name="ple",
    )(x3, pe_p, pe_s, lp['ple_norm'].reshape(1, d), _to_bf16(lp['ple_wg']), lp['ple_bg'].reshape(1, d),
      lp['ple_proj'], fn)

    states = (hl_p.reshape(n_prompt, d_rnn), cn_p, so_p, hl_s.reshape(n_sample, d_rnn), cn_s, so_s)
    return y_p, y_s, states


def kernel(x_prompt, x_sample, p_prompt, p_sample, state_lru, state_conv, state_ret, ffn1_norm, ffn1_wg, ffn1_wu, ffn1_wd, mix_norm, w_in, b_in, conv_w, conv_b, lru_wa, lru_ba, lru_wx, lru_bx, lru_lambda, ret_norm, proj_a, proj_b, w_out, ffn2_norm, ffn2_wg, ffn2_wu, ffn2_wd, ple_norm, ple_wg, ple_bg, ple_proj, final_norm):
    params = dict(ffn1_norm=ffn1_norm, ffn1_wg=ffn1_wg, ffn1_wu=ffn1_wu, ffn1_wd=ffn1_wd, mix_norm=mix_norm,
                  w_in=w_in, b_in=b_in, conv_w=conv_w, conv_b=conv_b, lru_wa=lru_wa, lru_ba=lru_ba, lru_wx=lru_wx,
                  lru_bx=lru_bx, lru_lambda=lru_lambda, ret_norm=ret_norm, proj_a=proj_a, proj_b=proj_b,
                  w_out=w_out, ffn2_norm=ffn2_norm, ffn2_wg=ffn2_wg, ffn2_wu=ffn2_wu, ffn2_wd=ffn2_wd,
                  ple_norm=ple_norm, ple_wg=ple_wg, ple_bg=ple_bg, ple_proj=ple_proj)
    depth = w_in.shape[0]
    n_prompt, seq, d = x_prompt.shape
    n_sample, dec_seq, _ = x_sample.shape
    mp, ms = n_prompt * seq, n_sample * dec_seq
    dk = d // N_RET_HEADS

    y_p = x_prompt.astype(f32).reshape(mp, d)
    y_s = x_sample.astype(f32).reshape(ms, d)
    cos_t, sin_t = _rope_table(seq, dec_seq, ms, dk // 2)
    cos_t, sin_t = (jnp.concatenate([jnp.tile(t[:seq], (n_prompt, 1)), t[seq:]], axis=0) for t in (cos_t, sin_t))
    outs = [[] for _ in range(6)]
    for i in range(depth):
        lp = {k: v[i].astype(f32) for k, v in params.items()}
        y_p, y_s, st = _layer([(y_p, 0), (y_s, mp)], p_prompt[i].astype(f32).reshape(mp, -1),
                              p_sample[i].astype(f32).reshape(ms, -1), n_prompt, seq, n_sample, dec_seq,
                              state_lru[i].astype(f32), state_conv[i].astype(f32), state_ret[i].astype(f32),
                              cos_t, sin_t, lp, final_norm.astype(f32) if i == depth - 1 else None)
        for o, s in zip(outs, st):
            o.append(s)
    y_prompt = y_p.reshape(n_prompt, seq, d).astype(x_prompt.dtype)
    y_sample = y_s.reshape(n_sample, dec_seq, d).astype(x_sample.dtype)
    lru_p, conv_p, ret_p, lru_s, conv_s, ret_s = (jnp.stack(o) for o in outs)
    return (y_prompt, y_sample, lru_p.astype(state_lru.dtype), conv_p.astype(state_conv.dtype),
            ret_p.astype(state_ret.dtype), lru_s.astype(state_lru.dtype), conv_s.astype(state_conv.dtype),
            ret_s.astype(state_ret.dtype))
```

```python
import functools

import jax
import jax.numpy as jnp
from jax import lax
from jax.experimental import pallas as pl
from jax.experimental.pallas import tpu as pltpu

f32 = jnp.float32
bf16 = jnp.bfloat16

N_LRU_BLOCKS = 8
CONV_W = 4
LRU_C = 8.0
N_RET_HEADS = 8
ROPE_BASE = 10000.0
EPS = 1e-6
PAST_LEN = 16384

RET_CHUNK = 256
SAMPLE_PAIR = 2
ROW_TILE = 1024
IN_ROW_TILE = 1536
GATE_COL_TILE = 256
S_RING = 3
VMEM_LIMIT = 58 * 1024 * 1024


def _cparams(sem):
    return pltpu.CompilerParams(dimension_semantics=sem, vmem_limit_bytes=VMEM_LIMIT)


LOG2E = 1.4426950408889634


def _exp_neg(x):
    return jnp.exp2(x * -LOG2E)


def _sigmoid(x):
    return 1.0 / (1.0 + _exp_neg(x))


def _rms(x, g):
    return x * lax.rsqrt(jnp.mean(x * x, axis=-1, keepdims=True) + EPS) * g


def _dot(a, b):
    return jnp.dot(a, b, preferred_element_type=f32)


def _dot_nt(a, b):
    return lax.dot_general(a, b, (((1,), (1,)), ((), ())), preferred_element_type=f32)


def _dot_tn(a, b):
    return lax.dot_general(a, b, (((0,), (0,)), ((), ())), preferred_element_type=f32)


_ANY = pl.BlockSpec(memory_space=pl.ANY)


def _cast_kernel(x_ref, o_ref):
    o_ref[...] = x_ref[...].astype(o_ref.dtype)


def _to_bf16(w):
    w2 = w.reshape(-1, w.shape[-1])
    return pl.pallas_call(_cast_kernel, out_shape=jax.ShapeDtypeStruct(w2.shape, bf16), name="to_bf16")(w2).reshape(w.shape)


def _rope_table_kernel(inv_ref, cos_ref, sin_ref, *, seq, dec_seq):
    rows = cos_ref.shape[0]
    r = lax.broadcasted_iota(jnp.int32, (rows, inv_ref.shape[1]), 0)
    pos = jnp.where(r < seq, r, PAST_LEN + lax.rem(r - seq, dec_seq))
    ang = pos.astype(f32) * inv_ref[...]
    cos_ref[...] = jnp.cos(ang)
    sin_ref[...] = jnp.sin(ang)


def _rope_table(seq, dec_seq, n_sample_rows, half):
    inv = (ROPE_BASE ** (-jnp.arange(half, dtype=f32) / half)).reshape(1, half)
    rows = seq + n_sample_rows
    return pl.pallas_call(
        functools.partial(_rope_table_kernel, seq=seq, dec_seq=dec_seq),
        out_shape=(jax.ShapeDtypeStruct((rows, half), f32), jax.ShapeDtypeStruct((rows, half), f32)),
        name="rope_table",
    )(inv)


def _ffn_kernel(*refs, group_rows, n_tiles, emit_norm):
    n_groups = len(group_rows)
    x_hbm = refs[:n_groups]
    g_ref, wg_ref, wu_ref, wd_ref = refs[n_groups:n_groups + 4]
    rest = list(refs[n_groups + 4:])
    g2_ref = rest.pop(0) if emit_norm else None
    xo_ref = rest.pop(0)
    u2_ref = rest.pop(0) if emit_norm else None
    (u_scr,) = rest
    i = pl.program_id(0)
    f = pl.program_id(1)
    tm = xo_ref.shape[0]

    @pl.when(f == 0)
    def _():
        for t in range(n_tiles):
            @pl.when(i == t)
            def _(t=t):
                for gi, (r0, nr) in enumerate(group_rows):
                    lo, hi = max(t * tm, r0), min((t + 1) * tm, r0 + nr)
                    if lo < hi:
                        pltpu.sync_copy(x_hbm[gi].at[lo - r0:hi - r0, :], xo_ref.at[lo - t * tm:hi - t * tm, :])

        u_scr[...] = _rms(xo_ref[...], g_ref[...]).astype(bf16)

    u = u_scr[...]
    hs = []
    for c0 in range(0, wg_ref.shape[1], FFN_SUB_TILE):
        cols = slice(c0, c0 + FFN_SUB_TILE)
        hg = _dot(u, wg_ref[:, cols].astype(bf16))
        hu = _dot(u, wu_ref[:, cols].astype(bf16))
        hs.append((0.5 * hg * _sigmoid(hg) * hu).astype(bf16))
    h = hs[0] if len(hs) == 1 else jnp.concatenate(hs, axis=1)
    xo_ref[...] += _dot(h, wd_ref[...].astype(bf16))

    if emit_norm:
        @pl.when(f == pl.num_programs(1) - 1)
        def _():
            u2_ref[...] = _rms(xo_ref[...], g2_ref[...]).astype(bf16)


FFN_SUB_TILE = 256
FFN_VMEM_BUDGET = 60 * 1024 * 1024


def _ffn_tiles(m, d, ff, with_norm):
    sub = FFN_SUB_TILE
    for tm in range(m, 1023, -256):
        if m % tm:
            continue
        for tf in (4 * sub, 2 * sub, sub):
            if ff % tf == 0:
                rows = tm * d * (4 + 2 + (2 if with_norm else 0))
                weights = 3 * d * tf * 4 * 2
                temps = 2 * tm * sub * 4 + tm * tf * 2 + (2 * d * sub + tf * d) * 2
                if rows + weights + temps <= FFN_VMEM_BUDGET:
                    return tm, tf
    raise ValueError("no FFN tiling fits VMEM")


def _ffn(parts, m_total, g, wg, wu, wd, g2=None, *, tm, tf):
    d = wg.shape[0]
    ff = wg.shape[1]
    emit_norm = g2 is not None
    row = lambda i, f: (i, 0)
    out_mode = dict(pipeline_mode=pl.Buffered(1))
    in_specs = [_ANY] * len(parts) + [
        pl.BlockSpec((1, d), lambda i, f: (0, 0)),
        pl.BlockSpec((d, tf), lambda i, f: (0, f)),
        pl.BlockSpec((d, tf), lambda i, f: (0, f)),
        pl.BlockSpec((tf, d), lambda i, f: (f, 0)),
    ]
    args = [xp for xp, _ in parts] + [g.reshape(1, d), wg, wu, wd]
    out_shape = [jax.ShapeDtypeStruct((m_total, d), f32)]
    out_specs = [pl.BlockSpec((tm, d), row, **out_mode)]
    if emit_norm:
        in_specs.append(pl.BlockSpec((1, d), lambda i, f: (0, 0)))
        args.append(g2.reshape(1, d))
        out_shape.append(jax.ShapeDtypeStruct((m_total, d), bf16))
        out_specs.append(pl.BlockSpec((tm, d), row, **out_mode))
    outs = pl.pallas_call(
        functools.partial(_ffn_kernel, group_rows=tuple((r0, xp.shape[0]) for xp, r0 in parts),
                          n_tiles=m_total // tm, emit_norm=emit_norm),
        grid=(m_total // tm, ff // tf),
        in_specs=in_specs,
        out_specs=out_specs,
        out_shape=out_shape,
        scratch_shapes=[pltpu.VMEM((tm, d), bf16)],
        compiler_params=_cparams(("arbitrary", "arbitrary")),
        name="ffn",
    )(*args)
    return outs if emit_norm else outs[0]


MM_ROW_SPLIT = 2


A_RING = 3


def _mm_kernel(a_ref, b_ref, *rest, epilogue, col_axis, row_split, a_ring):
    if a_ring:
        *extra, o_ref, bw_scr, a_buf, a_sem = rest
    else:
        *extra, o_ref, bw_scr = rest
    j = pl.program_id(col_axis)
    slot = j if col_axis == 1 else 0

    @pl.when(pl.program_id(1 - col_axis) == 0)
    def _():
        bw_scr[slot] = b_ref[...].astype(bf16)

    if a_ring:
        n_rows, tm = pl.num_programs(1), a_buf.shape[1]
        step = pl.program_id(0) * n_rows + pl.program_id(1)
        n_steps = pl.num_programs(0) * n_rows

        def a_copy(s):
            src = a_ref.at[pl.ds(pl.multiple_of(lax.rem(s, n_rows) * tm, tm), tm), :]
            return pltpu.make_async_copy(src, a_buf.at[lax.rem(s, A_RING)], a_sem.at[lax.rem(s, A_RING)])

        @pl.when(step == 0)
        def _():
            for s0 in range(A_RING - 1):
                a_copy(jnp.int32(s0)).start()

        @pl.when(step + (A_RING - 1) < n_steps)
        def _():
            a_copy(step + (A_RING - 1)).start()

        a_copy(step).wait()
        a_tile = a_buf.at[lax.rem(step, A_RING)]
    else:
        a_tile = a_ref

    rows = a_tile.shape[0] // row_split
    for r in range(row_split):
        rs = slice(r * rows, (r + 1) * rows)
        o_ref[rs, :] = epilogue(_dot(a_tile[rs, :], bw_scr[slot]), *extra, rows=rs, col_tile=j).astype(o_ref.dtype)


def _mm(a, b, *, col0, n_cols, out_dtype, epilogue, extras=(), tm=ROW_TILE, tn=1024, resident=True,
        row_split=MM_ROW_SPLIT, name="mm"):
    m, k = a.shape
    off = col0 // tn
    nj = n_cols // tn
    scratch = [pltpu.VMEM((nj if resident else 1, k, tn), bf16)]
    if resident:
        grid, order = (m // tm, nj), (lambda f: (lambda i, j: f(j, i)))
        b_map = lambda j, i: (0, jnp.where(i == 0, j, nj - 1) + off)
        a_spec = pl.BlockSpec((tm, k), order(lambda j, i: (i, 0)))
    else:
        grid, order = (nj, m // tm), (lambda f: f)
        b_map = lambda j, i: (0, j + off)
        a_spec = _ANY
        scratch += [pltpu.VMEM((A_RING, tm, k), a.dtype), pltpu.SemaphoreType.DMA((A_RING,))]
        assert nj * (m // tm) >= A_RING
    in_specs = [a_spec, pl.BlockSpec((k, tn), order(b_map))]
    args = [a, b]
    for arr, blk, imap in extras:
        in_specs.append(pl.BlockSpec(blk, order(imap)))
        args.append(arr)
    return pl.pallas_call(
        functools.partial(_mm_kernel, epilogue=epilogue, col_axis=1 if resident else 0, row_split=row_split,
                          a_ring=not resident),
        grid=grid,
        in_specs=in_specs,
        out_specs=pl.BlockSpec((tm, tn), order(lambda j, i: (i, j))),
        out_shape=jax.ShapeDtypeStruct((m, n_cols), out_dtype),
        scratch_shapes=scratch,
        compiler_params=_cparams(("arbitrary", "arbitrary")),
        name=name,
    )(*args)


def _gelu_tanh(x):
    half_x = 0.5 * x
    return half_x + half_x * jnp.tanh(x * (0.7978845608028654 + 0.7978845608028654 * 0.044715 * (x * x)))


def _rope_epilogue(acc, bias_ref, cos_ref, sin_ref, *, rows, col_tile, k_tile0, k_scale, head_dim):
    z = acc + bias_ref[...]
    cos = cos_ref[rows, :]
    sin = sin_ref[rows, :]
    half = head_dim // 2
    parts = []
    for h0 in range(0, z.shape[1], head_dim):
        x1 = z[:, h0:h0 + half]
        x2 = z[:, h0 + half:h0 + head_dim]
        parts += [x1 * cos - x2 * sin, x2 * cos + x1 * sin]
    scale = jnp.where(col_tile >= k_tile0, k_scale, 1.0).astype(f32)
    return jnp.concatenate(parts, axis=1) * scale


def _lru_gates(xc, n, cs, wa_ref, wx_ref, ba_ref, bx_ref, lam_ref):
    xcb = xc.astype(bf16)
    r = _sigmoid(_dot(xcb, wa_ref[n].astype(bf16)) + ba_ref[:, cs])
    gi = _sigmoid(_dot(xcb, wx_ref[n].astype(bf16)) + bx_ref[:, cs])
    q = r * (LRU_C * jax.nn.softplus(-lam_ref[:, cs]))
    a = _exp_neg(q)
    one_minus = jnp.tanh(q) * (a * a + 1.0)
    return a, one_minus * lax.rsqrt(jnp.maximum(one_minus, 1e-36)), gi


def _lru_prompt_kernel(*refs, n_seq):
    oa_ref = refs[9]

    @pl.when(pl.program_id(0) >= n_seq)
    def _():
        oa_ref[...] = jnp.zeros_like(oa_ref)

    @pl.when(pl.program_id(0) < n_seq)
    def _():
        _lru_prompt_body(*refs)


def _lru_prompt_body(xa_ref, gg_ref, cw_ref, cb_ref, wa_ref, wx_ref, ba_ref, bx_ref, lam_ref,
                     oa_ref, hl_ref, cn_ref, tail_scr, a_scr, b_scr, hc_scr):
    tt, c = xa_ref.shape
    blk = c // N_LRU_BLOCKS
    n_steps = tt // 8
    nw = CONV_W - 1
    t = pl.program_id(1)

    def interleave(x):
        return jnp.swapaxes(x.reshape(8, n_steps, blk), 0, 1).reshape(tt, blk)

    @pl.when(t == 0)
    def _():
        tail_scr[...] = jnp.zeros_like(tail_scr)
        hc_scr[...] = jnp.zeros_like(hc_scr)

    sub = lax.broadcasted_iota(jnp.int32, (8, blk), 0)
    seq_start = lax.broadcasted_iota(jnp.int32, (tt, 1), 0) + t * tt == 0
    for n in range(N_LRU_BLOCKS):
        cs = slice(n * blk, (n + 1) * blk)
        xp = interleave(xa_ref[:, cs])
        lead = []
        for jv in range(nw):
            own = xp[(n_steps - nw + jv) * 8:(n_steps - nw + jv + 1) * 8, :]
            prev = tail_scr[jv * 8:(jv + 1) * 8, cs]
            lead.append(pltpu.roll(jnp.where(sub == 7, prev, own), 1, axis=0))
        xc = cb_ref[:, cs] + cw_ref[nw:nw + 1, cs] * xp
        for s in range(1, CONV_W):
            shifted = jnp.concatenate(lead[nw - s:] + [xp[:tt - 8 * s, :]], axis=0)
            xc = xc + cw_ref[nw - s:nw - s + 1, cs] * shifted
        tail_scr[:, cs] = xp[tt - 8 * nw:, :]
        a, mult, gi = _lru_gates(xc, n, cs, wa_ref, wx_ref, ba_ref, bx_ref, lam_ref)
        mult = jnp.where(seq_start, 1.0, mult)
        a_scr[:, cs] = a
        b_scr[:, cs] = mult * (gi * xc)

    def step(k, carry):
        h, p = carry
        r = pl.ds(pl.multiple_of(k * 8, 8), 8)
        a = a_scr[r, :]
        h = a * h + b_scr[r, :]
        p = a * p
        b_scr[r, :] = h
        a_scr[r, :] = p
        return h, p

    h_end, p_end = lax.fori_loop(0, n_steps, step, (jnp.zeros((8, c), f32), jnp.ones((8, c), f32)), unroll=4)

    sub_c = lax.broadcasted_iota(jnp.int32, (8, c), 0)
    d = 1
    while d < 8:
        keep = sub_c >= d
        h_end = jnp.where(keep, p_end * pltpu.roll(h_end, d, axis=0) + h_end, h_end)
        p_end = jnp.where(keep, p_end * pltpu.roll(p_end, d, axis=0), p_end)
        d *= 2
    h_in = hc_scr[...]
    ends = h_end + p_end * h_in
    starts = jnp.where(sub_c == 0, h_in, pltpu.roll(ends, 1, axis=0))
    hc_scr[...] = ends[7:8, :]
    hl_ref[0] = ends[7:8, :]

    for n in range(N_LRU_BLOCKS):
        cs = slice(n * blk, (n + 1) * blk)
        h3 = b_scr[:, cs].reshape(n_steps, 8, blk) + a_scr[:, cs].reshape(n_steps, 8, blk) * starts[None, :, cs]
        h = jnp.swapaxes(h3, 0, 1).reshape(tt, blk)
        oa_ref[:, cs] = h.astype(bf16) * gg_ref[:, cs]
    cn_ref[0] = xa_ref[tt - nw:tt, :]


def _lru_sample_kernel(xa_ref, gg_ref, scp_ref, h0_ref, cw_ref, cb_ref, wa_ref, wx_ref, ba_ref, bx_ref, lam_ref,
                       oa_prev_ref, oa_ref, hl_ref, cn_ref):
    del oa_prev_ref
    rows, c = xa_ref.shape
    bt, t_len = scp_ref.shape[0], scp_ref.shape[1]
    blk = c // N_LRU_BLOCKS
    tpos = lax.broadcasted_iota(jnp.int32, (bt, t_len, blk), 1)
    for n in range(N_LRU_BLOCKS):
        cs = slice(n * blk, (n + 1) * blk)
        x3 = xa_ref[:, cs].reshape(bt, t_len, blk)
        ext = jnp.concatenate([scp_ref[:, :, cs], x3], axis=1)
        xc3 = cb_ref[:, cs] + cw_ref[CONV_W - 1:CONV_W, cs] * x3
        for s in range(1, CONV_W):
            xc3 = xc3 + cw_ref[CONV_W - 1 - s:CONV_W - s, cs] * pltpu.roll(ext, s, axis=1)[:, t_len:, :]
        xc = xc3.reshape(rows, blk)
        a, mult, gi = _lru_gates(xc, n, cs, wa_ref, wx_ref, ba_ref, bx_ref, lam_ref)
        a3 = a.reshape(bt, t_len, blk)
        b3 = (mult * (gi * xc)).reshape(bt, t_len, blk)
        d = 1
        while d < t_len:
            keep = tpos >= d
            b3 = jnp.where(keep, a3 * pltpu.roll(b3, d, axis=1) + b3, b3)
            a3 = jnp.where(keep, a3 * pltpu.roll(a3, d, axis=1), a3)
            d *= 2
        h3 = b3 + a3 * h0_ref[:, :, cs]
        oa_ref[:, cs] = h3.reshape(rows, blk).astype(bf16) * gg_ref[:, cs]
        hl_ref[:, :, cs] = h3[:, t_len - 1:t_len, :]
        cn_ref[:, :, cs] = pltpu.roll(x3, CONV_W - 1, axis=1)[:, 0:CONV_W - 1, :]


def _ret_norm_gate(o, rn, sg):
    return (o * lax.rsqrt(jnp.mean(o * o, axis=-1, keepdims=True) + EPS) * rn).astype(bf16) * sg


def _ret_prompt_kernel(*refs, n_seq):
    ob_ref = refs[9]

    @pl.when(pl.program_id(0) >= n_seq)
    def _():
        ob_ref[...] = jnp.zeros_like(ob_ref)

    @pl.when(pl.program_id(0) < n_seq)
    def _():
        _ret_prompt_body(*refs)


def _ret_prompt_body(q_ref, k_ref, v_ref, sg_ref, rn_ref, dm_ref, cd_ref, sd_ref, chd_ref, ob_ref, so_ref):
    c = pl.program_id(1)
    dk, dv = so_ref.shape[2], so_ref.shape[3]

    @pl.when(c == 0)
    def _():
        so_ref[...] = jnp.zeros_like(so_ref)

    for h in range(N_RET_HEADS):
        ks = slice(h * dk, (h + 1) * dk)
        vs = slice(h * dv, (h + 1) * dv)
        q = q_ref[:, ks]
        k = k_ref[:, ks]
        v = v_ref[:, vs]
        s_old = so_ref[0, h]
        scores = _dot_nt(q, k) * dm_ref[h]
        inner = _dot(scores.astype(bf16), v)
        cross = _dot(q, s_old.astype(bf16)) * cd_ref[h]
        kd = (k.astype(f32) * sd_ref[h]).astype(bf16)
        so_ref[0, h] = s_old * chd_ref[h] + _dot_tn(kd, v)
        ob_ref[:, vs] = _ret_norm_gate(inner + cross, rn_ref[h], sg_ref[:, vs])


def _ret_sample_body(q_ref, k_ref, v_ref, sg_ref, s_ref, rn_ref, dm_ref, cd_ref, sd_ref, chd_ref, ob_ref, so_ref,
                     *, t_len, head0):
    rows = q_ref.shape[0]
    n_heads, dk, dv = so_ref.shape[1], so_ref.shape[2], so_ref.shape[3]
    row = lax.broadcasted_iota(jnp.int32, (rows, 1), 0)
    for hl in range(n_heads):
        h = head0 + hl
        ks = slice(hl * dk, (hl + 1) * dk)
        vs = slice(hl * dv, (hl + 1) * dv)
        q = q_ref[:, ks]
        k = k_ref[:, ks]
        v = v_ref[:, vs]
        scores = _dot_nt(q, k) * dm_ref[h]
        o = _dot(scores.astype(bf16), v)
        kd = k.astype(f32) * sd_ref[h]
        for bb in range(rows // t_len):
            mine = (row >= bb * t_len) & (row < (bb + 1) * t_len)
            s_old = s_ref[bb, hl]
            cross = _dot(q, s_old.astype(bf16)) * cd_ref[h]
            o = o + jnp.where(mine, cross, 0.0)
            so_ref[bb, hl] = s_old * chd_ref[h] + _dot_tn(jnp.where(mine, kd, 0.0).astype(bf16), v)
        ob_ref[:, vs] = _ret_norm_gate(o, rn_ref[h], sg_ref[:, vs])


def _gates_ret_kernel(a_ref, b_ref, bias_ref, q_ref, k_ref, v_ref, sg_ref, s_hbm, rn_ref, dm_ref, cd_ref, sd_ref,
                      chd_ref, ob_prev_ref, g_ref, ob_ref, so_ref, bw_scr, s_buf, s_sem, *, t_len, n_units):
    del ob_prev_ref
    i, j = pl.program_id(0), pl.program_id(1)
    n_seq, n_heads = s_buf.shape[1], s_buf.shape[2]
    n_parts = N_RET_HEADS // n_heads
    unit = i * pl.num_programs(1) + j

    def state_copy(u):
        slot = lax.rem(u, S_RING)
        src = s_hbm.at[pl.ds((u // n_parts) * n_seq, n_seq), pl.ds(lax.rem(u, n_parts) * n_heads, n_heads)]
        return pltpu.make_async_copy(src, s_buf.at[slot], s_sem.at[slot])

    @pl.when(unit == 0)
    def _():
        for u0 in range(S_RING - 1):
            state_copy(jnp.int32(u0)).start()

    @pl.when(unit + (S_RING - 1) < n_units)
    def _():
        state_copy(unit + (S_RING - 1)).start()

    @pl.when(i == 0)
    def _():
        bw_scr[j] = b_ref[...].astype(bf16)

    rows = a_ref.shape[0] // MM_ROW_SPLIT
    for r in range(MM_ROW_SPLIT):
        rs = slice(r * rows, (r + 1) * rows)
        g_ref[rs, :] = _sigmoid(_dot(a_ref[rs, :], bw_scr[j]) + bias_ref[...]).astype(g_ref.dtype)

    state_copy(unit).wait()
    _ret_sample_body(q_ref, k_ref, v_ref, sg_ref, s_buf.at[lax.rem(unit, S_RING)], rn_ref, dm_ref, cd_ref, sd_ref,
                     chd_ref, ob_ref, so_ref, t_len=t_len, head0=lax.rem(unit, n_parts) * n_heads)


def _decay_tables(chunk, n_seq):
    log_g = jnp.log1p(-jnp.exp2(-5.0 - jnp.arange(N_RET_HEADS, dtype=f32)))
    r = jnp.arange(chunk * n_seq)
    idx = (r % chunk).astype(f32)
    same = (r[:, None] // chunk) == (r[None, :] // chunk)
    diff = idx[:, None] - idx[None, :]
    dmask = jnp.where(same & (diff >= 0), jnp.exp(jnp.maximum(diff, 0.0)[None] * log_g[:, None, None]), 0.0)
    cross_decay = jnp.exp((idx[None] + 1.0) * log_g[:, None])[..., None]
    state_decay = jnp.exp((chunk - 1.0 - idx[None]) * log_g[:, None])[..., None]
    chunk_decay = jnp.exp(chunk * log_g)[:, None, None]
    return dmask, cross_decay, state_decay, chunk_decay


def _ple_kernel(x_ref, pep_ref, pes_ref, gn_ref, wg_ref, bg_ref, pp_ref, fn_ref, yp_ref, ys_ref, *, n_prompt_tiles, final):
    i = pl.program_id(0)
    x = x_ref[...]
    u = _rms(x, gn_ref[...]).astype(bf16)
    gate = _sigmoid(_dot(u, wg_ref[...]) + bg_ref[...])
    pe = jnp.where(i < n_prompt_tiles, pep_ref[...], pes_ref[...]).astype(bf16)
    x = x + gate * _dot(pe, pp_ref[...].astype(bf16))
    y = _rms(x, fn_ref[...]) if final else x

    @pl.when(i < n_prompt_tiles)
    def _():
        yp_ref[...] = y

    @pl.when(i >= n_prompt_tiles)
    def _():
        ys_ref[...] = y


def _layer(x_parts, pe_p, pe_s, n_prompt, seq, n_sample, dec_seq, h0_s, conv_s, ret_s, cos_t, sin_t, lp, final_norm):
    d = lp['mix_norm'].shape[0]
    mp, ms = n_prompt * seq, n_sample * dec_seq
    m = mp + ms
    d_rnn = lp['conv_w'].shape[1]
    hh = N_RET_HEADS
    dk = d // hh
    dv = lp['ret_norm'].shape[1]
    qk_dim, v_dim = hh * dk, hh * dv
    tm, tn = ROW_TILE, 1024

    ff = lp['ffn1_wg'].shape[1]
    tm1, tf1 = _ffn_tiles(m, d, ff, with_norm=True)
    x1, u = _ffn(x_parts, m, lp['ffn1_norm'], lp['ffn1_wg'], lp['ffn1_wu'], lp['ffn1_wd'], lp['mix_norm'],
                 tm=tm1, tf=tf1)

    w_in = lp['w_in']
    b_in = lp['b_in'].reshape(1, -1)

    def bias_extra(col0):
        return (b_in, (1, tn), lambda j, i, o=col0 // tn: (0, j + o))

    def with_bias(fn):
        return lambda acc, b_ref, rows, col_tile: fn(acc + b_ref[...])

    c_xa, c_ga, c_q, c_v = 0, d_rnn, 2 * d_rnn, 2 * d_rnn + 2 * qk_dim
    c_gr, c_gate = c_v + v_dim, c_v + 2 * v_dim
    tmi = IN_ROW_TILE if m % IN_ROW_TILE == 0 else tm
    proj = functools.partial(_mm, u, w_in, tm=tmi, tn=tn)
    xa = proj(col0=c_xa, n_cols=d_rnn, out_dtype=f32, epilogue=with_bias(lambda z: z),
              extras=[bias_extra(c_xa)], name="inproj_xa")
    heavy = 2 * MM_ROW_SPLIT
    gg = proj(col0=c_ga, n_cols=d_rnn, out_dtype=bf16, epilogue=with_bias(_gelu_tanh),
              extras=[bias_extra(c_ga)], row_split=heavy, name="inproj_ga")
    half = dk // 2
    rope_map = lambda j, i: (i, 0)
    qk = proj(col0=c_q, n_cols=2 * qk_dim, out_dtype=bf16,
              epilogue=functools.partial(_rope_epilogue, k_tile0=qk_dim // tn, k_scale=dk ** -0.5, head_dim=dk),
              extras=[bias_extra(c_q), (cos_t, (tmi, half), rope_map), (sin_t, (tmi, half), rope_map)],
              row_split=heavy, name="inproj_qk")
    v = proj(col0=c_v, n_cols=v_dim, out_dtype=bf16, epilogue=with_bias(lambda z: z),
             extras=[bias_extra(c_v)], name="inproj_v")
    sg = proj(col0=c_gr, n_cols=v_dim, out_dtype=bf16, epilogue=with_bias(lambda z: z * _sigmoid(z)),
              extras=[bias_extra(c_gr)], row_split=heavy, name="inproj_gr")

    row2 = lambda a: a.reshape(1, -1)
    lru_w = [lp['conv_w'], row2(lp['conv_b']), lp['lru_wa'], lp['lru_wx'], row2(lp['lru_ba']), row2(lp['lru_bx']),
             row2(lp['lru_lambda'])]
    nb = N_LRU_BLOCKS
    blk = d_rnn // nb
    z2 = lambda *_: (0, 0)
    z3 = lambda *_: (0, 0, 0)
    lru_w_specs = [pl.BlockSpec((CONV_W, d_rnn), z2), pl.BlockSpec((1, d_rnn), z2),
                   pl.BlockSpec((nb, blk, blk), z3), pl.BlockSpec((nb, blk, blk), z3),
                   pl.BlockSpec((1, d_rnn), z2), pl.BlockSpec((1, d_rnn), z2), pl.BlockSpec((1, d_rnn), z2)]

    tt = 512
    ntt = seq // tt
    assert ms % tt == 0 and ms // tt <= ntt
    seq_rows = lambda b, t: (jnp.minimum(b * ntt + t, mp // tt - 1), 0)
    all_rows = lambda b, t: (jnp.minimum(b * ntt + t, m // tt - 1), 0)
    per_seq = lambda b, t: (jnp.minimum(b, n_prompt - 1), 0, 0)
    oa, hl_p, cn_p = pl.pallas_call(
        functools.partial(_lru_prompt_kernel, n_seq=n_prompt),
        grid=(n_prompt + 1, ntt),
        in_specs=[pl.BlockSpec((tt, d_rnn), seq_rows), pl.BlockSpec((tt, d_rnn), seq_rows)] + lru_w_specs,
        out_specs=[pl.BlockSpec((tt, d_rnn), all_rows), pl.BlockSpec((1, 1, d_rnn), per_seq),
                   pl.BlockSpec((1, CONV_W - 1, d_rnn), per_seq)],
        out_shape=[jax.ShapeDtypeStruct((m, d_rnn), bf16), jax.ShapeDtypeStruct((n_prompt, 1, d_rnn), f32),
                   jax.ShapeDtypeStruct((n_prompt, CONV_W - 1, d_rnn), f32)],
        scratch_shapes=[pltpu.VMEM((8 * (CONV_W - 1), d_rnn), f32), pltpu.VMEM((tt, d_rnn), f32),
                        pltpu.VMEM((tt, d_rnn), f32), pltpu.VMEM((1, d_rnn), f32)],
        compiler_params=_cparams(("arbitrary", "arbitrary")),
        name="lru_prompt",
    )(xa, gg, *lru_w)

    bt = 32
    rows_s = bt * dec_seq
    s_off = mp // rows_s
    scp = jnp.pad(conv_s, ((0, 0), (dec_seq - (CONV_W - 1), 0), (0, 0)))
    samp_rows = lambda i: (s_off + i, 0)
    samp3 = lambda i: (i, 0, 0)
    oa, hl_s, cn_s = pl.pallas_call(
        _lru_sample_kernel,
        grid=(n_sample // bt,),
        in_specs=[pl.BlockSpec((rows_s, d_rnn), samp_rows), pl.BlockSpec((rows_s, d_rnn), samp_rows),
                  pl.BlockSpec((bt, dec_seq, d_rnn), samp3), pl.BlockSpec((bt, 1, d_rnn), samp3)] + lru_w_specs + [_ANY],
        out_specs=[pl.BlockSpec((rows_s, d_rnn), samp_rows), pl.BlockSpec((bt, 1, d_rnn), samp3),
                   pl.BlockSpec((bt, CONV_W - 1, d_rnn), samp3)],
        out_shape=[jax.ShapeDtypeStruct((m, d_rnn), bf16), jax.ShapeDtypeStruct((n_sample, 1, d_rnn), f32),
                   jax.ShapeDtypeStruct((n_sample, CONV_W - 1, d_rnn), f32)],
        input_output_aliases={11: 0},
        compiler_params=_cparams(("parallel",)),
        name="lru_sample",
    )(xa, gg, scp, h0_s.reshape(n_sample, 1, d_rnn), *lru_w, oa)

    rn = lp['ret_norm'].reshape(hh, 1, dv)
    ct = RET_CHUNK
    nct = seq // ct
    tabs = _decay_tables(ct, 1)
    full3 = lambda shape: pl.BlockSpec(shape, lambda *_: (0, 0, 0))
    tab_specs = lambda r: [full3((hh, 1, dv)), full3((hh, r, r)), full3((hh, r, 1)), full3((hh, r, 1)), full3((hh, 1, 1))]
    assert ms % ct == 0 and ms // ct <= nct
    chunk_row = lambda b, c: jnp.minimum(b * nct + c, mp // ct - 1)
    ob, so_p = pl.pallas_call(
        functools.partial(_ret_prompt_kernel, n_seq=n_prompt),
        grid=(n_prompt + 1, nct),
        in_specs=[pl.BlockSpec((ct, qk_dim), lambda b, c: (chunk_row(b, c), 0)),
                  pl.BlockSpec((ct, qk_dim), lambda b, c: (chunk_row(b, c), 1)),
                  pl.BlockSpec((ct, v_dim), lambda b, c: (chunk_row(b, c), 0)),
                  pl.BlockSpec((ct, v_dim), lambda b, c: (chunk_row(b, c), 0))] + tab_specs(ct),
        out_specs=[pl.BlockSpec((ct, v_dim), lambda b, c: (jnp.minimum(b * nct + c, m // ct - 1), 0)),
                   pl.BlockSpec((1, hh, dk, dv), lambda b, c: (jnp.minimum(b, n_prompt - 1), 0, 0, 0))],
        out_shape=[jax.ShapeDtypeStruct((m, v_dim), bf16), jax.ShapeDtypeStruct((n_prompt, hh, dk, dv), f32)],
        compiler_params=_cparams(("arbitrary", "arbitrary")),
        name="ret_prompt",
    )(qk, qk, v, sg, rn, *tabs)

    pr = SAMPLE_PAIR
    rows_r = pr * dec_seq
    r_off = mp // rows_r
    tabs = _decay_tables(dec_seq, pr)
    n_parts = 2
    hp = hh // n_parts
    n_units = (n_sample // pr) * n_parts
    tng = GATE_COL_TILE
    njg = 2 * d // tng
    nig = n_units // njg
    tmg = m // nig
    assert nig * njg == n_units >= S_RING and nig * tmg == m and tmg % 16 == 0 and njg % n_parts == 0
    unit = lambda i, j: i * njg + j
    seqs = lambda i, j: unit(i, j) // n_parts
    part = lambda i, j: unit(i, j) % n_parts
    qk_parts = qk_dim // (hp * dk)
    gates, ob, so_s = pl.pallas_call(
        functools.partial(_gates_ret_kernel, t_len=dec_seq, n_units=n_units),
        grid=(nig, njg),
        in_specs=[pl.BlockSpec((tmg, d), lambda i, j: (i, 0)),
                  pl.BlockSpec((d, tng), lambda i, j: (0, jnp.where(i == 0, j, njg - 1) + c_gate // tng)),
                  pl.BlockSpec((1, tng), lambda i, j: (0, j + c_gate // tng)),
                  pl.BlockSpec((rows_r, hp * dk), lambda i, j: (r_off + seqs(i, j), part(i, j))),
                  pl.BlockSpec((rows_r, hp * dk), lambda i, j: (r_off + seqs(i, j), qk_parts + part(i, j))),
                  pl.BlockSpec((rows_r, hp * dv), lambda i, j: (r_off + seqs(i, j), part(i, j))),
                  pl.BlockSpec((rows_r, hp * dv), lambda i, j: (r_off + seqs(i, j), part(i, j))),
                  _ANY] + tab_specs(rows_r) + [_ANY],
        out_specs=[pl.BlockSpec((tmg, tng), lambda i, j: (i, j)),
                   pl.BlockSpec((rows_r, hp * dv), lambda i, j: (r_off + seqs(i, j), part(i, j))),
                   pl.BlockSpec((pr, hp, dk, dv), lambda i, j: (seqs(i, j), part(i, j), 0, 0))],
        out_shape=[jax.ShapeDtypeStruct((m, 2 * d), bf16), jax.ShapeDtypeStruct((m, v_dim), bf16),
                   jax.ShapeDtypeStruct((n_sample, hh, dk, dv), f32)],
        scratch_shapes=[pltpu.VMEM((njg, d, tng), bf16), pltpu.VMEM((S_RING, pr, hp, dk, dv), f32),
                        pltpu.SemaphoreType.DMA((S_RING,))],
        input_output_aliases={13: 1},
        compiler_params=_cparams(("arbitrary", "arbitrary")),
        name="gates_ret_sample",
    )(u, w_in, b_in, qk, qk, v, sg, ret_s, rn, *tabs, ob)

    tile = lambda j, i: (i, j)
    pa = _mm(oa, lp['proj_a'], col0=0, n_cols=d, out_dtype=f32, epilogue=lambda acc, rows, col_tile: acc,
             name="proj_a")
    tnb = 512
    merged = _mm(ob, lp['proj_b'], col0=0, n_cols=d, out_dtype=bf16, tn=tnb, resident=False,
                 epilogue=lambda acc, pa_ref, ga_ref, gb_ref, rows, col_tile: (
                     ga_ref[rows, :].astype(f32) * pa_ref[rows, :] + gb_ref[rows, :].astype(f32) * acc),
                 extras=[(pa, (tm, tnb), tile), (gates, (tm, tnb), tile),
                         (gates, (tm, tnb), lambda j, i: (i, j + d // tnb))], name="proj_b_merge")
    x2 = _mm(merged, lp['w_out'], col0=0, n_cols=d, out_dtype=f32,
             epilogue=lambda acc, x_ref, rows, col_tile: x_ref[rows, :] + acc,
             extras=[(x1, (tm, tn), tile)], name="w_out")

    tm2, tf2 = _ffn_tiles(m, d, ff, with_norm=False)
    x3 = _ffn([(x2, 0)], m, lp['ffn2_norm'], lp['ffn2_wg'], lp['ffn2_wu'], lp['ffn2_wd'], tm=tm2, tf=tf2)

    tp = 512
    ple_dim = pe_p.shape[1]
    fin = final_norm is not None
    fn = (final_norm if fin else lp['ple_norm']).reshape(1, d)
    npt = mp // tp
    c2 = lambda i: (0, 0)
    prow = lambda i: (jnp.minimum(i, npt - 1), 0)
    srow = lambda i: (jnp.maximum(i - npt, 0), 0)
    y_p, y_s = pl.pallas_call(
        functools.partial(_ple_kernel, n_prompt_tiles=npt, final=fin),
        grid=(m // tp,),
        in_specs=[pl.BlockSpec((tp, d), lambda i: (i, 0)), pl.BlockSpec((tp, ple_dim), prow),
                  pl.BlockSpec((tp, ple_dim), srow), pl.BlockSpec((1, d), c2), pl.BlockSpec((d, d), c2),
                  pl.BlockSpec((1, d), c2), pl.BlockSpec((ple_dim, d), c2), pl.BlockSpec((1, d), c2)],
        out_specs=[pl.BlockSpec((tp, d), prow), pl.BlockSpec((tp, d), srow)],
        out_shape=[jax.ShapeDtypeStruct((mp, d), f32), jax.ShapeDtypeStruct((ms, d), f32)],
        compiler_params=_cparams(("arbitrary",)),
        name="ple",
    )(x3, pe_p, pe_s, lp['ple_norm'].reshape(1, d), _to_bf16(lp['ple_wg']), lp['ple_bg'].reshape(1, d),
      lp['ple_proj'], fn)

    states = (hl_p.reshape(n_prompt, d_rnn), cn_p, so_p, hl_s.reshape(n_sample, d_rnn), cn_s, so_s)
    return y_p, y_s, states


def kernel(x_prompt, x_sample, p_prompt, p_sample, state_lru, state_conv, state_ret, ffn1_norm, ffn1_wg, ffn1_wu, ffn1_wd, mix_norm, w_in, b_in, conv_w, conv_b, lru_wa, lru_ba, lru_wx, lru_bx, lru_lambda, ret_norm, proj_a, proj_b, w_out, ffn2_norm, ffn2_wg, ffn2_wu, ffn2_wd, ple_norm, ple_wg, ple_bg, ple_proj, final_norm):
    params = dict(ffn1_norm=ffn1_norm, ffn1_wg=ffn1_wg, ffn1_wu=ffn1_wu, ffn1_wd=ffn1_wd, mix_norm=mix_norm,
                  w_in=w_in, b_in=b_in, conv_w=conv_w, conv_b=conv_b, lru_wa=lru_wa, lru_ba=lru_ba, lru_wx=lru_wx,
                  lru_bx=lru_bx, lru_lambda=lru_lambda, ret_norm=ret_norm, proj_a=proj_a, proj_b=proj_b,
                  w_out=w_out, ffn2_norm=ffn2_norm, ffn2_wg=ffn2_wg, ffn2_wu=ffn2_wu, ffn2_wd=ffn2_wd,
                  ple_norm=ple_norm, ple_wg=ple_wg, ple_bg=ple_bg, ple_proj=ple_proj)
    depth = w_in.shape[0]
    n_prompt, seq, d = x_prompt.shape
    n_sample, dec_seq, _ = x_sample.shape
    mp, ms = n_prompt * seq, n_sample * dec_seq
    dk = d // N_RET_HEADS

    y_p = x_prompt.astype(f32).reshape(mp, d)
    y_s = x_sample.astype(f32).reshape(ms, d)
    cos_t, sin_t = _rope_table(seq, dec_seq, ms, dk // 2)
    cos_t, sin_t = (jnp.concatenate([jnp.tile(t[:seq], (n_prompt, 1)), t[seq:]], axis=0) for t in (cos_t, sin_t))
    outs = [[] for _ in range(6)]
    for i in range(depth):
        lp = {k: v[i].astype(f32) for k, v in params.items()}
        y_p, y_s, st = _layer([(y_p, 0), (y_s, mp)], p_prompt[i].astype(f32).reshape(mp, -1),
                              p_sample[i].astype(f32).reshape(ms, -1), n_prompt, seq, n_sample, dec_seq,
                              state_lru[i].astype(f32), state_conv[i].astype(f32), state_ret[i].astype(f32),
                              cos_t, sin_t, lp, final_norm.astype(f32) if i == depth - 1 else None)
        for o, s in zip(outs, st):
            o.append(s)
    y_prompt = y_p.reshape(n_prompt, seq, d).astype(x_prompt.dtype)
    y_sample = y_s.reshape(n_sample, dec_seq, d).astype(x_sample.dtype)
    lru_p, conv_p, ret_p, lru_s, conv_s, ret_s = (jnp.stack(o) for o in outs)
    return (y_prompt, y_sample, lru_p.astype(state_lru.dtype), conv_p.astype(state_conv.dtype),
            ret_p.astype(state_ret.dtype), lru_s.astype(state_lru.dtype), conv_s.astype(state_conv.dtype),
            ret_s.astype(state_ret.dtype))
```

```python
import functools

import jax
import jax.numpy as jnp
from jax import lax
from jax.experimental import pallas as pl
from jax.experimental.pallas import tpu as pltpu

f32 = jnp.float32
bf16 = jnp.bfloat16

N_LRU_BLOCKS = 8
CONV_W = 4
LRU_C = 8.0
N_RET_HEADS = 8
ROPE_BASE = 10000.0
EPS = 1e-6
PAST_LEN = 16384

RET_CHUNK = 256
SAMPLE_PAIR = 2
ROW_TILE = 1024
IN_ROW_TILE = 1536
GATE_COL_TILE = 256
S_RING = 3
VMEM_LIMIT = 58 * 1024 * 1024


def _cparams(sem):
    return pltpu.CompilerParams(dimension_semantics=sem, vmem_limit_bytes=VMEM_LIMIT)


LOG2E = 1.4426950408889634


def _exp_neg(x):
    return jnp.exp2(x * -LOG2E)


def _sigmoid(x):
    return 1.0 / (1.0 + _exp_neg(x))


def _rms(x, g):
    return x * lax.rsqrt(jnp.mean(x * x, axis=-1, keepdims=True) + EPS) * g


def _dot(a, b):
    return jnp.dot(a, b, preferred_element_type=f32)


def _dot_nt(a, b):
    return lax.dot_general(a, b, (((1,), (1,)), ((), ())), preferred_element_type=f32)


def _dot_tn(a, b):
    return lax.dot_general(a, b, (((0,), (0,)), ((), ())), preferred_element_type=f32)


_ANY = pl.BlockSpec(memory_space=pl.ANY)


def _cast_kernel(x_ref, o_ref):
    o_ref[...] = x_ref[...].astype(o_ref.dtype)


def _to_bf16(w):
    w2 = w.reshape(-1, w.shape[-1])
    return pl.pallas_call(_cast_kernel, out_shape=jax.ShapeDtypeStruct(w2.shape, bf16), name="to_bf16")(w2).reshape(w.shape)


def _rope_table_kernel(inv_ref, cos_ref, sin_ref, *, seq, dec_seq):
    rows = cos_ref.shape[0]
    r = lax.broadcasted_iota(jnp.int32, (rows, inv_ref.shape[1]), 0)
    pos = jnp.where(r < seq, r, PAST_LEN + lax.rem(r - seq, dec_seq))
    ang = pos.astype(f32) * inv_ref[...]
    cos_ref[...] = jnp.cos(ang)
    sin_ref[...] = jnp.sin(ang)


def _rope_table(seq, dec_seq, n_sample_rows, half):
    inv = (ROPE_BASE ** (-jnp.arange(half, dtype=f32) / half)).reshape(1, half)
    rows = seq + n_sample_rows
    return pl.pallas_call(
        functools.partial(_rope_table_kernel, seq=seq, dec_seq=dec_seq),
        out_shape=(jax.ShapeDtypeStruct((rows, half), f32), jax.ShapeDtypeStruct((rows, half), f32)),
        name="rope_table",
    )(inv)


def _ffn_kernel(*refs, group_rows, n_tiles, emit_norm):
    n_groups = len(group_rows)
    x_hbm = refs[:n_groups]
    g_ref, wg_ref, wu_ref, wd_ref = refs[n_groups:n_groups + 4]
    rest = list(refs[n_groups + 4:])
    g2_ref = rest.pop(0) if emit_norm else None
    xo_ref = rest.pop(0)
    u2_ref = rest.pop(0) if emit_norm else None
    (u_scr,) = rest
    i = pl.program_id(0)
    f = pl.program_id(1)
    tm = xo_ref.shape[0]

    @pl.when(f == 0)
    def _():
        for t in range(n_tiles):
            @pl.when(i == t)
            def _(t=t):
                for gi, (r0, nr) in enumerate(group_rows):
                    lo, hi = max(t * tm, r0), min((t + 1) * tm, r0 + nr)
                    if lo < hi:
                        pltpu.sync_copy(x_hbm[gi].at[lo - r0:hi - r0, :], xo_ref.at[lo - t * tm:hi - t * tm, :])

        u_scr[...] = _rms(xo_ref[...], g_ref[...]).astype(bf16)

    u = u_scr[...]
    hs = []
    for c0 in range(0, wg_ref.shape[1], FFN_SUB_TILE):
        cols = slice(c0, c0 + FFN_SUB_TILE)
        hg = _dot(u, wg_ref[:, cols].astype(bf16))
        hu = _dot(u, wu_ref[:, cols].astype(bf16))
        hs.append((0.5 * hg * _sigmoid(hg) * hu).astype(bf16))
    h = hs[0] if len(hs) == 1 else jnp.concatenate(hs, axis=1)
    xo_ref[...] += _dot(h, wd_ref[...].astype(bf16))

    if emit_norm:
        @pl.when(f == pl.num_programs(1) - 1)
        def _():
            u2_ref[...] = _rms(xo_ref[...], g2_ref[...]).astype(bf16)


FFN_SUB_TILE = 256
FFN_VMEM_BUDGET = 60 * 1024 * 1024


def _ffn_tiles(m, d, ff, with_norm):
    sub = FFN_SUB_TILE
    for tm in range(m, 1023, -256):
        if m % tm:
            continue
        for tf in (4 * sub, 2 * sub, sub):
            if ff % tf == 0:
                rows = tm * d * (4 + 2 + (2 if with_norm else 0))
                weights = 3 * d * tf * 4 * 2
                temps = 2 * tm * sub * 4 + tm * tf * 2 + (2 * d * sub + tf * d) * 2
                if rows + weights + temps <= FFN_VMEM_BUDGET:
                    return tm, tf
    raise ValueError("no FFN tiling fits VMEM")


def _ffn(parts, m_total, g, wg, wu, wd, g2=None, *, tm, tf):
    d = wg.shape[0]
    ff = wg.shape[1]
    emit_norm = g2 is not None
    row = lambda i, f: (i, 0)
    out_mode = dict(pipeline_mode=pl.Buffered(1))
    in_specs = [_ANY] * len(parts) + [
        pl.BlockSpec((1, d), lambda i, f: (0, 0)),
        pl.BlockSpec((d, tf), lambda i, f: (0, f)),
        pl.BlockSpec((d, tf), lambda i, f: (0, f)),
        pl.BlockSpec((tf, d), lambda i, f: (f, 0)),
    ]
    args = [xp for xp, _ in parts] + [g.reshape(1, d), wg, wu, wd]
    out_shape = [jax.ShapeDtypeStruct((m_total, d), f32)]
    out_specs = [pl.BlockSpec((tm, d), row, **out_mode)]
    if emit_norm:
        in_specs.append(pl.BlockSpec((1, d), lambda i, f: (0, 0)))
        args.append(g2.reshape(1, d))
        out_shape.append(jax.ShapeDtypeStruct((m_total, d), bf16))
        out_specs.append(pl.BlockSpec((tm, d), row, **out_mode))
    outs = pl.pallas_call(
        functools.partial(_ffn_kernel, group_rows=tuple((r0, xp.shape[0]) for xp, r0 in parts),
                          n_tiles=m_total // tm, emit_norm=emit_norm),
        grid=(m_total // tm, ff // tf),
        in_specs=in_specs,
        out_specs=out_specs,
        out_shape=out_shape,
        scratch_shapes=[pltpu.VMEM((tm, d), bf16)],
        compiler_params=_cparams(("arbitrary", "arbitrary")),
        name="ffn",
    )(*args)
    return outs if emit_norm else outs[0]


MM_ROW_SPLIT = 2


A_RING = 3


def _mm_kernel(a_ref, b_ref, *rest, epilogue, col_axis, a_ring):
    if a_ring:
        *extra, o_ref, bw_scr, a_buf, a_sem = rest
    else:
        *extra, o_ref, bw_scr = rest
    j = pl.program_id(col_axis)
    slot = j if col_axis == 1 else 0

    @pl.when(pl.program_id(1 - col_axis) == 0)
    def _():
        bw_scr[slot] = b_ref[...].astype(bf16)

    if a_ring:
        n_rows, tm = pl.num_programs(1), a_buf.shape[1]
        step = pl.program_id(0) * n_rows + pl.program_id(1)
        n_steps = pl.num_programs(0) * n_rows

        def a_copy(s):
            src = a_ref.at[pl.ds(pl.multiple_of(lax.rem(s, n_rows) * tm, tm), tm), :]
            return pltpu.make_async_copy(src, a_buf.at[lax.rem(s, A_RING)], a_sem.at[lax.rem(s, A_RING)])

        @pl.when(step == 0)
        def _():
            for s0 in range(A_RING - 1):
                a_copy(jnp.int32(s0)).start()

        @pl.when(step + (A_RING - 1) < n_steps)
        def _():
            a_copy(step + (A_RING - 1)).start()

        a_copy(step).wait()
        a_tile = a_buf.at[lax.rem(step, A_RING)]
    else:
        a_tile = a_ref

    rows = a_tile.shape[0] // MM_ROW_SPLIT
    for r in range(MM_ROW_SPLIT):
        rs = slice(r * rows, (r + 1) * rows)
        o_ref[rs, :] = epilogue(_dot(a_tile[rs, :], bw_scr[slot]), *extra, rows=rs, col_tile=j).astype(o_ref.dtype)


def _mm(a, b, *, col0, n_cols, out_dtype, epilogue, extras=(), tm=ROW_TILE, tn=1024, resident=True, name="mm"):
    m, k = a.shape
    off = col0 // tn
    nj = n_cols // tn
    scratch = [pltpu.VMEM((nj if resident else 1, k, tn), bf16)]
    if resident:
        grid, order = (m // tm, nj), (lambda f: (lambda i, j: f(j, i)))
        b_map = lambda j, i: (0, jnp.where(i == 0, j, nj - 1) + off)
        a_spec = pl.BlockSpec((tm, k), order(lambda j, i: (i, 0)))
    else:
        grid, order = (nj, m // tm), (lambda f: f)
        b_map = lambda j, i: (0, j + off)
        a_spec = _ANY
        scratch += [pltpu.VMEM((A_RING, tm, k), a.dtype), pltpu.SemaphoreType.DMA((A_RING,))]
        assert nj * (m // tm) >= A_RING
    in_specs = [a_spec, pl.BlockSpec((k, tn), order(b_map))]
    args = [a, b]
    for arr, blk, imap in extras:
        in_specs.append(pl.BlockSpec(blk, order(imap)))
        args.append(arr)
    return pl.pallas_call(
        functools.partial(_mm_kernel, epilogue=epilogue, col_axis=1 if resident else 0, a_ring=not resident),
        grid=grid,
        in_specs=in_specs,
        out_specs=pl.BlockSpec((tm, tn), order(lambda j, i: (i, j))),
        out_shape=jax.ShapeDtypeStruct((m, n_cols), out_dtype),
        scratch_shapes=scratch,
        compiler_params=_cparams(("arbitrary", "arbitrary")),
        name=name,
    )(*args)


def _gelu_tanh(x):
    half_x = 0.5 * x
    return half_x + half_x * jnp.tanh(x * (0.7978845608028654 + 0.7978845608028654 * 0.044715 * (x * x)))


def _rope_epilogue(acc, bias_ref, cos_ref, sin_ref, *, rows, col_tile, k_tile0, k_scale, head_dim):
    z = acc + bias_ref[...]
    cos = cos_ref[rows, :]
    sin = sin_ref[rows, :]
    half = head_dim // 2
    parts = []
    for h0 in range(0, z.shape[1], head_dim):
        x1 = z[:, h0:h0 + half]
        x2 = z[:, h0 + half:h0 + head_dim]
        parts += [x1 * cos - x2 * sin, x2 * cos + x1 * sin]
    scale = jnp.where(col_tile >= k_tile0, k_scale, 1.0).astype(f32)
    return jnp.concatenate(parts, axis=1) * scale


def _lru_gates(xc, n, cs, wa_ref, wx_ref, ba_ref, bx_ref, lam_ref):
    xcb = xc.astype(bf16)
    r = _sigmoid(_dot(xcb, wa_ref[n].astype(bf16)) + ba_ref[:, cs])
    gi = _sigmoid(_dot(xcb, wx_ref[n].astype(bf16)) + bx_ref[:, cs])
    q = r * (LRU_C * jax.nn.softplus(-lam_ref[:, cs]))
    a = _exp_neg(q)
    one_minus = jnp.tanh(q) * (a * a + 1.0)
    return a, one_minus * lax.rsqrt(jnp.maximum(one_minus, 1e-36)), gi


def _lru_prompt_kernel(*refs, n_seq):
    oa_ref = refs[9]

    @pl.when(pl.program_id(0) >= n_seq)
    def _():
        oa_ref[...] = jnp.zeros_like(oa_ref)

    @pl.when(pl.program_id(0) < n_seq)
    def _():
        _lru_prompt_body(*refs)


def _lru_prompt_body(xa_ref, gg_ref, cw_ref, cb_ref, wa_ref, wx_ref, ba_ref, bx_ref, lam_ref,
                     oa_ref, hl_ref, cn_ref, tail_scr, a_scr, b_scr, hc_scr):
    tt, c = xa_ref.shape
    blk = c // N_LRU_BLOCKS
    n_steps = tt // 8
    nw = CONV_W - 1
    t = pl.program_id(1)

    def interleave(x):
        return jnp.swapaxes(x.reshape(8, n_steps, blk), 0, 1).reshape(tt, blk)

    @pl.when(t == 0)
    def _():
        tail_scr[...] = jnp.zeros_like(tail_scr)
        hc_scr[...] = jnp.zeros_like(hc_scr)

    sub = lax.broadcasted_iota(jnp.int32, (8, blk), 0)
    seq_start = lax.broadcasted_iota(jnp.int32, (tt, 1), 0) + t * tt == 0
    for n in range(N_LRU_BLOCKS):
        cs = slice(n * blk, (n + 1) * blk)
        xp = interleave(xa_ref[:, cs])
        lead = []
        for jv in range(nw):
            own = xp[(n_steps - nw + jv) * 8:(n_steps - nw + jv + 1) * 8, :]
            prev = tail_scr[jv * 8:(jv + 1) * 8, cs]
            lead.append(pltpu.roll(jnp.where(sub == 7, prev, own), 1, axis=0))
        xc = cb_ref[:, cs] + cw_ref[nw:nw + 1, cs] * xp
        for s in range(1, CONV_W):
            shifted = jnp.concatenate(lead[nw - s:] + [xp[:tt - 8 * s, :]], axis=0)
            xc = xc + cw_ref[nw - s:nw - s + 1, cs] * shifted
        tail_scr[:, cs] = xp[tt - 8 * nw:, :]
        a, mult, gi = _lru_gates(xc, n, cs, wa_ref, wx_ref, ba_ref, bx_ref, lam_ref)
        mult = jnp.where(seq_start, 1.0, mult)
        a_scr[:, cs] = a
        b_scr[:, cs] = mult * (gi * xc)

    def step(k, carry):
        h, p = carry
        r = pl.ds(pl.multiple_of(k * 8, 8), 8)
        a = a_scr[r, :]
        h = a * h + b_scr[r, :]
        p = a * p
        b_scr[r, :] = h
        a_scr[r, :] = p
        return h, p

    h_end, p_end = lax.fori_loop(0, n_steps, step, (jnp.zeros((8, c), f32), jnp.ones((8, c), f32)), unroll=4)

    sub_c = lax.broadcasted_iota(jnp.int32, (8, c), 0)
    d = 1
    while d < 8:
        keep = sub_c >= d
        h_end = jnp.where(keep, p_end * pltpu.roll(h_end, d, axis=0) + h_end, h_end)
        p_end = jnp.where(keep, p_end * pltpu.roll(p_end, d, axis=0), p_end)
        d *= 2
    h_in = hc_scr[...]
    ends = h_end + p_end * h_in
    starts = jnp.where(sub_c == 0, h_in, pltpu.roll(ends, 1, axis=0))
    hc_scr[...] = ends[7:8, :]
    hl_ref[0] = ends[7:8, :]

    for n in range(N_LRU_BLOCKS):
        cs = slice(n * blk, (n + 1) * blk)
        h3 = b_scr[:, cs].reshape(n_steps, 8, blk) + a_scr[:, cs].reshape(n_steps, 8, blk) * starts[None, :, cs]
        h = jnp.swapaxes(h3, 0, 1).reshape(tt, blk)
        oa_ref[:, cs] = h.astype(bf16) * gg_ref[:, cs]
    cn_ref[0] = xa_ref[tt - nw:tt, :]


def _lru_sample_kernel(xa_ref, gg_ref, scp_ref, h0_ref, cw_ref, cb_ref, wa_ref, wx_ref, ba_ref, bx_ref, lam_ref,
                       oa_prev_ref, oa_ref, hl_ref, cn_ref):
    del oa_prev_ref
    rows, c = xa_ref.shape
    bt, t_len = scp_ref.shape[0], scp_ref.shape[1]
    blk = c // N_LRU_BLOCKS
    tpos = lax.broadcasted_iota(jnp.int32, (bt, t_len, blk), 1)
    for n in range(N_LRU_BLOCKS):
        cs = slice(n * blk, (n + 1) * blk)
        x3 = xa_ref[:, cs].reshape(bt, t_len, blk)
        ext = jnp.concatenate([scp_ref[:, :, cs], x3], axis=1)
        xc3 = cb_ref[:, cs] + cw_ref[CONV_W - 1:CONV_W, cs] * x3
        for s in range(1, CONV_W):
            xc3 = xc3 + cw_ref[CONV_W - 1 - s:CONV_W - s, cs] * pltpu.roll(ext, s, axis=1)[:, t_len:, :]
        xc = xc3.reshape(rows, blk)
        a, mult, gi = _lru_gates(xc, n, cs, wa_ref, wx_ref, ba_ref, bx_ref, lam_ref)
        a3 = a.reshape(bt, t_len, blk)
        b3 = (mult * (gi * xc)).reshape(bt, t_len, blk)
        d = 1
        while d < t_len:
            keep = tpos >= d
            b3 = jnp.where(keep, a3 * pltpu.roll(b3, d, axis=1) + b3, b3)
            a3 = jnp.where(keep, a3 * pltpu.roll(a3, d, axis=1), a3)
            d *= 2
        h3 = b3 + a3 * h0_ref[:, :, cs]
        oa_ref[:, cs] = h3.reshape(rows, blk).astype(bf16) * gg_ref[:, cs]
        hl_ref[:, :, cs] = h3[:, t_len - 1:t_len, :]
        cn_ref[:, :, cs] = pltpu.roll(x3, CONV_W - 1, axis=1)[:, 0:CONV_W - 1, :]


def _ret_norm_gate(o, rn, sg):
    return (o * lax.rsqrt(jnp.mean(o * o, axis=-1, keepdims=True) + EPS) * rn).astype(bf16) * sg


def _ret_prompt_kernel(*refs, n_seq):
    ob_ref = refs[9]

    @pl.when(pl.program_id(0) >= n_seq)
    def _():
        ob_ref[...] = jnp.zeros_like(ob_ref)

    @pl.when(pl.program_id(0) < n_seq)
    def _():
        _ret_prompt_body(*refs)


def _ret_prompt_body(q_ref, k_ref, v_ref, sg_ref, rn_ref, dm_ref, cd_ref, sd_ref, chd_ref, ob_ref, so_ref):
    c = pl.program_id(1)
    dk, dv = so_ref.shape[2], so_ref.shape[3]

    @pl.when(c == 0)
    def _():
        so_ref[...] = jnp.zeros_like(so_ref)

    for h in range(N_RET_HEADS):
        ks = slice(h * dk, (h + 1) * dk)
        vs = slice(h * dv, (h + 1) * dv)
        q = q_ref[:, ks]
        k = k_ref[:, ks]
        v = v_ref[:, vs]
        s_old = so_ref[0, h]
        scores = _dot_nt(q, k) * dm_ref[h]
        inner = _dot(scores.astype(bf16), v)
        cross = _dot(q, s_old.astype(bf16)) * cd_ref[h]
        kd = (k.astype(f32) * sd_ref[h]).astype(bf16)
        so_ref[0, h] = s_old * chd_ref[h] + _dot_tn(kd, v)
        ob_ref[:, vs] = _ret_norm_gate(inner + cross, rn_ref[h], sg_ref[:, vs])


def _ret_sample_body(q_ref, k_ref, v_ref, sg_ref, s_ref, rn_ref, dm_ref, cd_ref, sd_ref, chd_ref, ob_ref, so_ref,
                     *, t_len, head0):
    rows = q_ref.shape[0]
    n_heads, dk, dv = so_ref.shape[1], so_ref.shape[2], so_ref.shape[3]
    row = lax.broadcasted_iota(jnp.int32, (rows, 1), 0)
    for hl in range(n_heads):
        h = head0 + hl
        ks = slice(hl * dk, (hl + 1) * dk)
        vs = slice(hl * dv, (hl + 1) * dv)
        q = q_ref[:, ks]
        k = k_ref[:, ks]
        v = v_ref[:, vs]
        scores = _dot_nt(q, k) * dm_ref[h]
        o = _dot(scores.astype(bf16), v)
        kd = k.astype(f32) * sd_ref[h]
        for bb in range(rows // t_len):
            mine = (row >= bb * t_len) & (row < (bb + 1) * t_len)
            s_old = s_ref[bb, hl]
            cross = _dot(q, s_old.astype(bf16)) * cd_ref[h]
            o = o + jnp.where(mine, cross, 0.0)
            so_ref[bb, hl] = s_old * chd_ref[h] + _dot_tn(jnp.where(mine, kd, 0.0).astype(bf16), v)
        ob_ref[:, vs] = _ret_norm_gate(o, rn_ref[h], sg_ref[:, vs])


def _gates_ret_kernel(a_ref, b_ref, bias_ref, q_ref, k_ref, v_ref, sg_ref, s_hbm, rn_ref, dm_ref, cd_ref, sd_ref,
                      chd_ref, ob_prev_ref, g_ref, ob_ref, so_ref, bw_scr, s_buf, s_sem, *, t_len, n_units):
    del ob_prev_ref
    i, j = pl.program_id(0), pl.program_id(1)
    n_seq, n_heads = s_buf.shape[1], s_buf.shape[2]
    n_parts = N_RET_HEADS // n_heads
    unit = i * pl.num_programs(1) + j

    def state_copy(u):
        slot = lax.rem(u, S_RING)
        src = s_hbm.at[pl.ds((u // n_parts) * n_seq, n_seq), pl.ds(lax.rem(u, n_parts) * n_heads, n_heads)]
        return pltpu.make_async_copy(src, s_buf.at[slot], s_sem.at[slot])

    @pl.when(unit == 0)
    def _():
        for u0 in range(S_RING - 1):
            state_copy(jnp.int32(u0)).start()

    @pl.when(unit + (S_RING - 1) < n_units)
    def _():
        state_copy(unit + (S_RING - 1)).start()

    @pl.when(i == 0)
    def _():
        bw_scr[j] = b_ref[...].astype(bf16)

    rows = a_ref.shape[0] // MM_ROW_SPLIT
    for r in range(MM_ROW_SPLIT):
        rs = slice(r * rows, (r + 1) * rows)
        g_ref[rs, :] = _sigmoid(_dot(a_ref[rs, :], bw_scr[j]) + bias_ref[...]).astype(g_ref.dtype)

    state_copy(unit).wait()
    _ret_sample_body(q_ref, k_ref, v_ref, sg_ref, s_buf.at[lax.rem(unit, S_RING)], rn_ref, dm_ref, cd_ref, sd_ref,
                     chd_ref, ob_ref, so_ref, t_len=t_len, head0=lax.rem(unit, n_parts) * n_heads)


def _decay_tables(chunk, n_seq):
    log_g = jnp.log1p(-jnp.exp2(-5.0 - jnp.arange(N_RET_HEADS, dtype=f32)))
    r = jnp.arange(chunk * n_seq)
    idx = (r % chunk).astype(f32)
    same = (r[:, None] // chunk) == (r[None, :] // chunk)
    diff = idx[:, None] - idx[None, :]
    dmask = jnp.where(same & (diff >= 0), jnp.exp(jnp.maximum(diff, 0.0)[None] * log_g[:, None, None]), 0.0)
    cross_decay = jnp.exp((idx[None] + 1.0) * log_g[:, None])[..., None]
    state_decay = jnp.exp((chunk - 1.0 - idx[None]) * log_g[:, None])[..., None]
    chunk_decay = jnp.exp(chunk * log_g)[:, None, None]
    return dmask, cross_decay, state_decay, chunk_decay


def _ple_kernel(x_ref, pep_ref, pes_ref, gn_ref, wg_ref, bg_ref, pp_ref, fn_ref, yp_ref, ys_ref, *, n_prompt_tiles, final):
    i = pl.program_id(0)
    x = x_ref[...]
    u = _rms(x, gn_ref[...]).astype(bf16)
    gate = _sigmoid(_dot(u, wg_ref[...]) + bg_ref[...])
    pe = jnp.where(i < n_prompt_tiles, pep_ref[...], pes_ref[...]).astype(bf16)
    x = x + gate * _dot(pe, pp_ref[...].astype(bf16))
    y = _rms(x, fn_ref[...]) if final else x

    @pl.when(i < n_prompt_tiles)
    def _():
        yp_ref[...] = y

    @pl.when(i >= n_prompt_tiles)
    def _():
        ys_ref[...] = y


def _layer(x_parts, pe_p, pe_s, n_prompt, seq, n_sample, dec_seq, h0_s, conv_s, ret_s, cos_t, sin_t, lp, final_norm):
    d = lp['mix_norm'].shape[0]
    mp, ms = n_prompt * seq, n_sample * dec_seq
    m = mp + ms
    d_rnn = lp['conv_w'].shape[1]
    hh = N_RET_HEADS
    dk = d // hh
    dv = lp['ret_norm'].shape[1]
    qk_dim, v_dim = hh * dk, hh * dv
    tm, tn = ROW_TILE, 1024

    ff = lp['ffn1_wg'].shape[1]
    tm1, tf1 = _ffn_tiles(m, d, ff, with_norm=True)
    x1, u = _ffn(x_parts, m, lp['ffn1_norm'], lp['ffn1_wg'], lp['ffn1_wu'], lp['ffn1_wd'], lp['mix_norm'],
                 tm=tm1, tf=tf1)

    w_in = lp['w_in']
    b_in = lp['b_in'].reshape(1, -1)

    def bias_extra(col0):
        return (b_in, (1, tn), lambda j, i, o=col0 // tn: (0, j + o))

    def with_bias(fn):
        return lambda acc, b_ref, rows, col_tile: fn(acc + b_ref[...])

    c_xa, c_ga, c_q, c_v = 0, d_rnn, 2 * d_rnn, 2 * d_rnn + 2 * qk_dim
    c_gr, c_gate = c_v + v_dim, c_v + 2 * v_dim
    tmi = IN_ROW_TILE if m % IN_ROW_TILE == 0 else tm
    proj = functools.partial(_mm, u, w_in, tm=tmi, tn=tn)
    xa = proj(col0=c_xa, n_cols=d_rnn, out_dtype=f32, epilogue=with_bias(lambda z: z),
              extras=[bias_extra(c_xa)], name="inproj_xa")
    gg = proj(col0=c_ga, n_cols=d_rnn, out_dtype=bf16, epilogue=with_bias(_gelu_tanh),
              extras=[bias_extra(c_ga)], name="inproj_ga")
    half = dk // 2
    rope_map = lambda j, i: (i, 0)
    qk = proj(col0=c_q, n_cols=2 * qk_dim, out_dtype=bf16,
              epilogue=functools.partial(_rope_epilogue, k_tile0=qk_dim // tn, k_scale=dk ** -0.5, head_dim=dk),
              extras=[bias_extra(c_q), (cos_t, (tmi, half), rope_map), (sin_t, (tmi, half), rope_map)],
              name="inproj_qk")
    v = proj(col0=c_v, n_cols=v_dim, out_dtype=bf16, epilogue=with_bias(lambda z: z),
             extras=[bias_extra(c_v)], name="inproj_v")
    sg = proj(col0=c_gr, n_cols=v_dim, out_dtype=bf16, epilogue=with_bias(lambda z: z * _sigmoid(z)),
              extras=[bias_extra(c_gr)], name="inproj_gr")

    row2 = lambda a: a.reshape(1, -1)
    lru_w = [lp['conv_w'], row2(lp['conv_b']), lp['lru_wa'], lp['lru_wx'], row2(lp['lru_ba']), row2(lp['lru_bx']),
             row2(lp['lru_lambda'])]
    nb = N_LRU_BLOCKS
    blk = d_rnn // nb
    z2 = lambda *_: (0, 0)
    z3 = lambda *_: (0, 0, 0)
    lru_w_specs = [pl.BlockSpec((CONV_W, d_rnn), z2), pl.BlockSpec((1, d_rnn), z2),
                   pl.BlockSpec((nb, blk, blk), z3), pl.BlockSpec((nb, blk, blk), z3),
                   pl.BlockSpec((1, d_rnn), z2), pl.BlockSpec((1, d_rnn), z2), pl.BlockSpec((1, d_rnn), z2)]

    tt = 512
    ntt = seq // tt
    assert ms % tt == 0 and ms // tt <= ntt
    seq_rows = lambda b, t: (jnp.minimum(b * ntt + t, mp // tt - 1), 0)
    all_rows = lambda b, t: (jnp.minimum(b * ntt + t, m // tt - 1), 0)
    per_seq = lambda b, t: (jnp.minimum(b, n_prompt - 1), 0, 0)
    oa, hl_p, cn_p = pl.pallas_call(
        functools.partial(_lru_prompt_kernel, n_seq=n_prompt),
        grid=(n_prompt + 1, ntt),
        in_specs=[pl.BlockSpec((tt, d_rnn), seq_rows), pl.BlockSpec((tt, d_rnn), seq_rows)] + lru_w_specs,
        out_specs=[pl.BlockSpec((tt, d_rnn), all_rows), pl.BlockSpec((1, 1, d_rnn), per_seq),
                   pl.BlockSpec((1, CONV_W - 1, d_rnn), per_seq)],
        out_shape=[jax.ShapeDtypeStruct((m, d_rnn), bf16), jax.ShapeDtypeStruct((n_prompt, 1, d_rnn), f32),
                   jax.ShapeDtypeStruct((n_prompt, CONV_W - 1, d_rnn), f32)],
        scratch_shapes=[pltpu.VMEM((8 * (CONV_W - 1), d_rnn), f32), pltpu.VMEM((tt, d_rnn), f32),
                        pltpu.VMEM((tt, d_rnn), f32), pltpu.VMEM((1, d_rnn), f32)],
        compiler_params=_cparams(("arbitrary", "arbitrary")),
        name="lru_prompt",
    )(xa, gg, *lru_w)

    bt = 32
    rows_s = bt * dec_seq
    s_off = mp // rows_s
    scp = jnp.pad(conv_s, ((0, 0), (dec_seq - (CONV_W - 1), 0), (0, 0)))
    samp_rows = lambda i: (s_off + i, 0)
    samp3 = lambda i: (i, 0, 0)
    oa, hl_s, cn_s = pl.pallas_call(
        _lru_sample_kernel,
        grid=(n_sample // bt,),
        in_specs=[pl.BlockSpec((rows_s, d_rnn), samp_rows), pl.BlockSpec((rows_s, d_rnn), samp_rows),
                  pl.BlockSpec((bt, dec_seq, d_rnn), samp3), pl.BlockSpec((bt, 1, d_rnn), samp3)] + lru_w_specs + [_ANY],
        out_specs=[pl.BlockSpec((rows_s, d_rnn), samp_rows), pl.BlockSpec((bt, 1, d_rnn), samp3),
                   pl.BlockSpec((bt, CONV_W - 1, d_rnn), samp3)],
        out_shape=[jax.ShapeDtypeStruct((m, d_rnn), bf16), jax.ShapeDtypeStruct((n_sample, 1, d_rnn), f32),
                   jax.ShapeDtypeStruct((n_sample, CONV_W - 1, d_rnn), f32)],
        input_output_aliases={11: 0},
        compiler_params=_cparams(("parallel",)),
        name="lru_sample",
    )(xa, gg, scp, h0_s.reshape(n_sample, 1, d_rnn), *lru_w, oa)

    rn = lp['ret_norm'].reshape(hh, 1, dv)
    ct = RET_CHUNK
    nct = seq // ct
    tabs = _decay_tables(ct, 1)
    full3 = lambda shape: pl.BlockSpec(shape, lambda *_: (0, 0, 0))
    tab_specs = lambda r: [full3((hh, 1, dv)), full3((hh, r, r)), full3((hh, r, 1)), full3((hh, r, 1)), full3((hh, 1, 1))]
    assert ms % ct == 0 and ms // ct <= nct
    chunk_row = lambda b, c: jnp.minimum(b * nct + c, mp // ct - 1)
    ob, so_p = pl.pallas_call(
        functools.partial(_ret_prompt_kernel, n_seq=n_prompt),
        grid=(n_prompt + 1, nct),
        in_specs=[pl.BlockSpec((ct, qk_dim), lambda b, c: (chunk_row(b, c), 0)),
                  pl.BlockSpec((ct, qk_dim), lambda b, c: (chunk_row(b, c), 1)),
                  pl.BlockSpec((ct, v_dim), lambda b, c: (chunk_row(b, c), 0)),
                  pl.BlockSpec((ct, v_dim), lambda b, c: (chunk_row(b, c), 0))] + tab_specs(ct),
        out_specs=[pl.BlockSpec((ct, v_dim), lambda b, c: (jnp.minimum(b * nct + c, m // ct - 1), 0)),
                   pl.BlockSpec((1, hh, dk, dv), lambda b, c: (jnp.minimum(b, n_prompt - 1), 0, 0, 0))],
        out_shape=[jax.ShapeDtypeStruct((m, v_dim), bf16), jax.ShapeDtypeStruct((n_prompt, hh, dk, dv), f32)],
        compiler_params=_cparams(("arbitrary", "arbitrary")),
        name="ret_prompt",
    )(qk, qk, v, sg, rn, *tabs)

    pr = SAMPLE_PAIR
    rows_r = pr * dec_seq
    r_off = mp // rows_r
    tabs = _decay_tables(dec_seq, pr)
    n_parts = 2
    hp = hh // n_parts
    n_units = (n_sample // pr) * n_parts
    tng = GATE_COL_TILE
    njg = 2 * d // tng
    nig = n_units // njg
    tmg = m // nig
    assert nig * njg == n_units >= S_RING and nig * tmg == m and tmg % 16 == 0 and njg % n_parts == 0
    unit = lambda i, j: i * njg + j
    seqs = lambda i, j: unit(i, j) // n_parts
    part = lambda i, j: unit(i, j) % n_parts
    qk_parts = qk_dim // (hp * dk)
    gates, ob, so_s = pl.pallas_call(
        functools.partial(_gates_ret_kernel, t_len=dec_seq, n_units=n_units),
        grid=(nig, njg),
        in_specs=[pl.BlockSpec((tmg, d), lambda i, j: (i, 0)),
                  pl.BlockSpec((d, tng), lambda i, j: (0, jnp.where(i == 0, j, njg - 1) + c_gate // tng)),
                  pl.BlockSpec((1, tng), lambda i, j: (0, j + c_gate // tng)),
                  pl.BlockSpec((rows_r, hp * dk), lambda i, j: (r_off + seqs(i, j), part(i, j))),
                  pl.BlockSpec((rows_r, hp * dk), lambda i, j: (r_off + seqs(i, j), qk_parts + part(i, j))),
                  pl.BlockSpec((rows_r, hp * dv), lambda i, j: (r_off + seqs(i, j), part(i, j))),
                  pl.BlockSpec((rows_r, hp * dv), lambda i, j: (r_off + seqs(i, j), part(i, j))),
                  _ANY] + tab_specs(rows_r) + [_ANY],
        out_specs=[pl.BlockSpec((tmg, tng), lambda i, j: (i, j)),
                   pl.BlockSpec((rows_r, hp * dv), lambda i, j: (r_off + seqs(i, j), part(i, j))),
                   pl.BlockSpec((pr, hp, dk, dv), lambda i, j: (seqs(i, j), part(i, j), 0, 0))],
        out_shape=[jax.ShapeDtypeStruct((m, 2 * d), bf16), jax.ShapeDtypeStruct((m, v_dim), bf16),
                   jax.ShapeDtypeStruct((n_sample, hh, dk, dv), f32)],
        scratch_shapes=[pltpu.VMEM((njg, d, tng), bf16), pltpu.VMEM((S_RING, pr, hp, dk, dv), f32),
                        pltpu.SemaphoreType.DMA((S_RING,))],
        input_output_aliases={13: 1},
        compiler_params=_cparams(("arbitrary", "arbitrary")),
        name="gates_ret_sample",
    )(u, w_in, b_in, qk, qk, v, sg, ret_s, rn, *tabs, ob)

    tile = lambda j, i: (i, j)
    pa = _mm(oa, lp['proj_a'], col0=0, n_cols=d, out_dtype=f32, epilogue=lambda acc, rows, col_tile: acc,
             name="proj_a")
    tnb = 512
    merged = _mm(ob, lp['proj_b'], col0=0, n_cols=d, out_dtype=bf16, tn=tnb, resident=False,
                 epilogue=lambda acc, pa_ref, ga_ref, gb_ref, rows, col_tile: (
                     ga_ref[rows, :].astype(f32) * pa_ref[rows, :] + gb_ref[rows, :].astype(f32) * acc),
                 extras=[(pa, (tm, tnb), tile), (gates, (tm, tnb), tile),
                         (gates, (tm, tnb), lambda j, i: (i, j + d // tnb))], name="proj_b_merge")
    x2 = _mm(merged, lp['w_out'], col0=0, n_cols=d, out_dtype=f32,
             epilogue=lambda acc, x_ref, rows, col_tile: x_ref[rows, :] + acc,
             extras=[(x1, (tm, tn), tile)], name="w_out")

    tm2, tf2 = _ffn_tiles(m, d, ff, with_norm=False)
    x3 = _ffn([(x2, 0)], m, lp['ffn2_norm'], lp['ffn2_wg'], lp['ffn2_wu'], lp['ffn2_wd'], tm=tm2, tf=tf2)

    tp = 512
    ple_dim = pe_p.shape[1]
    fin = final_norm is not None
    fn = (final_norm if fin else lp['ple_norm']).reshape(1, d)
    npt = mp // tp
    c2 = lambda i: (0, 0)
    prow = lambda i: (jnp.minimum(i, npt - 1), 0)
    srow = lambda i: (jnp.maximum(i - npt, 0), 0)
    y_p, y_s = pl.pallas_call(
        functools.partial(_ple_kernel, n_prompt_tiles=npt, final=fin),
        grid=(m // tp,),
        in_specs=[pl.BlockSpec((tp, d), lambda i: (i, 0)), pl.BlockSpec((tp, ple_dim), prow),
                  pl.BlockSpec((tp, ple_dim), srow), pl.BlockSpec((1, d), c2), pl.BlockSpec((d, d), c2),
                  pl.BlockSpec((1, d), c2), pl.BlockSpec((ple_dim, d), c2), pl.BlockSpec((1, d), c2)],
        out_specs=[pl.BlockSpec((tp, d), prow), pl.BlockSpec((tp, d), srow)],
        out_shape=[jax.ShapeDtypeStruct((mp, d), f32), jax.ShapeDtypeStruct((ms, d), f32)],
        compiler_params=_cparams(("arbitrary",)),
        name="ple",
    )(x3, pe_p, pe_s, lp['ple_norm'].reshape(1, d), _to_bf16(lp['ple_wg']), lp['ple_bg'].reshape(1, d),
      lp['ple_proj'], fn)

    states = (hl_p.reshape(n_prompt, d_rnn), cn_p, so_p, hl_s.reshape(n_sample, d_rnn), cn_s, so_s)
    return y_p, y_s, states


def kernel(x_prompt, x_sample, p_prompt, p_sample, state_lru, state_conv, state_ret, ffn1_norm, ffn1_wg, ffn1_wu, ffn1_wd, mix_norm, w_in, b_in, conv_w, conv_b, lru_wa, lru_ba, lru_wx, lru_bx, lru_lambda, ret_norm, proj_a, proj_b, w_out, ffn2_norm, ffn2_wg, ffn2_wu, ffn2_wd, ple_norm, ple_wg, ple_bg, ple_proj, final_norm):
    params = dict(ffn1_norm=ffn1_norm, ffn1_wg=ffn1_wg, ffn1_wu=ffn1_wu, ffn1_wd=ffn1_wd, mix_norm=mix_norm,
                  w_in=w_in, b_in=b_in, conv_w=conv_w, conv_b=conv_b, lru_wa=lru_wa, lru_ba=lru_ba, lru_wx=lru_wx,
                  lru_bx=lru_bx, lru_lambda=lru_lambda, ret_norm=ret_norm, proj_a=proj_a, proj_b=proj_b,
                  w_out=w_out, ffn2_norm=ffn2_norm, ffn2_wg=ffn2_wg, ffn2_wu=ffn2_wu, ffn2_wd=ffn2_wd,
                  ple_norm=ple_norm, ple_wg=ple_wg, ple_bg=ple_bg, ple_proj=ple_proj)
    depth = w_in.shape[0]
    n_prompt, seq, d = x_prompt.shape
    n_sample, dec_seq, _ = x_sample.shape
    mp, ms = n_prompt * seq, n_sample * dec_seq
    dk = d // N_RET_HEADS

    y_p = x_prompt.astype(f32).reshape(mp, d)
    y_s = x_sample.astype(f32).reshape(ms, d)
    cos_t, sin_t = _rope_table(seq, dec_seq, ms, dk // 2)
    cos_t, sin_t = (jnp.concatenate([jnp.tile(t[:seq], (n_prompt, 1)), t[seq:]], axis=0) for t in (cos_t, sin_t))
    outs = [[] for _ in range(6)]
    for i in range(depth):
        lp = {k: v[i].astype(f32) for k, v in params.items()}
        y_p, y_s, st = _layer([(y_p, 0), (y_s, mp)], p_prompt[i].astype(f32).reshape(mp, -1),
                              p_sample[i].astype(f32).reshape(ms, -1), n_prompt, seq, n_sample, dec_seq,
                              state_lru[i].astype(f32), state_conv[i].astype(f32), state_ret[i].astype(f32),
                              cos_t, sin_t, lp, final_norm.astype(f32) if i == depth - 1 else None)
        for o, s in zip(outs, st):
            o.append(s)
    y_prompt = y_p.reshape(n_prompt, seq, d).astype(x_prompt.dtype)
    y_sample = y_s.reshape(n_sample, dec_seq, d).astype(x_sample.dtype)
    lru_p, conv_p, ret_p, lru_s, conv_s, ret_s = (jnp.stack(o) for o in outs)
    return (y_prompt, y_sample, lru_p.astype(state_lru.dtype), conv_p.astype(state_conv.dtype),
            ret_p.astype(state_ret.dtype), lru_s.astype(state_lru.dtype), conv_s.astype(state_conv.dtype),
            ret_s.astype(state_ret.dtype))
```

```python
import functools

import jax
import jax.numpy as jnp
from jax import lax
from jax.experimental import pallas as pl
from jax.experimental.pallas import tpu as pltpu

f32 = jnp.float32
bf16 = jnp.bfloat16

N_LRU_BLOCKS = 8
CONV_W = 4
LRU_C = 8.0
N_RET_HEADS = 8
ROPE_BASE = 10000.0
EPS = 1e-6
PAST_LEN = 16384

RET_CHUNK = 256
RET_STEP_CHUNKS = 2
SAMPLE_PAIR = 2
ROW_TILE = 1024
IN_ROW_TILE = 1536
GATE_COL_TILE = 256
S_RING = 3
VMEM_LIMIT = 58 * 1024 * 1024


def _cparams(sem):
    return pltpu.CompilerParams(dimension_semantics=sem, vmem_limit_bytes=VMEM_LIMIT)


LOG2E = 1.4426950408889634


def _exp_neg(x):
    return jnp.exp2(x * -LOG2E)


def _sigmoid(x):
    return 1.0 / (1.0 + _exp_neg(x))


def _rms(x, g):
    return x * lax.rsqrt(jnp.mean(x * x, axis=-1, keepdims=True) + EPS) * g


def _dot(a, b):
    return jnp.dot(a, b, preferred_element_type=f32)


def _dot_nt(a, b):
    return lax.dot_general(a, b, (((1,), (1,)), ((), ())), preferred_element_type=f32)


def _dot_tn(a, b):
    return lax.dot_general(a, b, (((0,), (0,)), ((), ())), preferred_element_type=f32)


_ANY = pl.BlockSpec(memory_space=pl.ANY)


def _cast_kernel(x_ref, o_ref):
    o_ref[...] = x_ref[...].astype(o_ref.dtype)


def _to_bf16(w):
    w2 = w.reshape(-1, w.shape[-1])
    return pl.pallas_call(_cast_kernel, out_shape=jax.ShapeDtypeStruct(w2.shape, bf16), name="to_bf16")(w2).reshape(w.shape)


def _rope_table_kernel(inv_ref, cos_ref, sin_ref, *, seq, dec_seq):
    rows = cos_ref.shape[0]
    r = lax.broadcasted_iota(jnp.int32, (rows, inv_ref.shape[1]), 0)
    pos = jnp.where(r < seq, r, PAST_LEN + lax.rem(r - seq, dec_seq))
    ang = pos.astype(f32) * inv_ref[...]
    cos_ref[...] = jnp.cos(ang)
    sin_ref[...] = jnp.sin(ang)


def _rope_table(seq, dec_seq, n_sample_rows, half):
    inv = (ROPE_BASE ** (-jnp.arange(half, dtype=f32) / half)).reshape(1, half)
    rows = seq + n_sample_rows
    return pl.pallas_call(
        functools.partial(_rope_table_kernel, seq=seq, dec_seq=dec_seq),
        out_shape=(jax.ShapeDtypeStruct((rows, half), f32), jax.ShapeDtypeStruct((rows, half), f32)),
        name="rope_table",
    )(inv)


def _ffn_kernel(*refs, group_rows, n_tiles, emit_norm):
    n_groups = len(group_rows)
    x_hbm = refs[:n_groups]
    g_ref, wg_ref, wu_ref, wd_ref = refs[n_groups:n_groups + 4]
    rest = list(refs[n_groups + 4:])
    g2_ref = rest.pop(0) if emit_norm else None
    xo_ref = rest.pop(0)
    u2_ref = rest.pop(0) if emit_norm else None
    (u_scr,) = rest
    i = pl.program_id(0)
    f = pl.program_id(1)
    tm = xo_ref.shape[0]

    @pl.when(f == 0)
    def _():
        for t in range(n_tiles):
            @pl.when(i == t)
            def _(t=t):
                for gi, (r0, nr) in enumerate(group_rows):
                    lo, hi = max(t * tm, r0), min((t + 1) * tm, r0 + nr)
                    if lo < hi:
                        pltpu.sync_copy(x_hbm[gi].at[lo - r0:hi - r0, :], xo_ref.at[lo - t * tm:hi - t * tm, :])

        u_scr[...] = _rms(xo_ref[...], g_ref[...]).astype(bf16)

    u = u_scr[...]
    hs = []
    for c0 in range(0, wg_ref.shape[1], FFN_SUB_TILE):
        cols = slice(c0, c0 + FFN_SUB_TILE)
        hg = _dot(u, wg_ref[:, cols].astype(bf16))
        hu = _dot(u, wu_ref[:, cols].astype(bf16))
        hs.append((0.5 * hg * _sigmoid(hg) * hu).astype(bf16))
    h = hs[0] if len(hs) == 1 else jnp.concatenate(hs, axis=1)
    xo_ref[...] += _dot(h, wd_ref[...].astype(bf16))

    if emit_norm:
        @pl.when(f == pl.num_programs(1) - 1)
        def _():
            u2_ref[...] = _rms(xo_ref[...], g2_ref[...]).astype(bf16)


FFN_SUB_TILE = 256
FFN_VMEM_BUDGET = 60 * 1024 * 1024


def _ffn_tiles(m, d, ff, with_norm):
    sub = FFN_SUB_TILE
    for tm in range(m, 1023, -256):
        if m % tm:
            continue
        for tf in (4 * sub, 2 * sub, sub):
            if ff % tf == 0:
                rows = tm * d * (4 + 2 + (2 if with_norm else 0))
                weights = 3 * d * tf * 4 * 2
                temps = 2 * tm * sub * 4 + tm * tf * 2 + (2 * d * sub + tf * d) * 2
                if rows + weights + temps <= FFN_VMEM_BUDGET:
                    return tm, tf
    raise ValueError("no FFN tiling fits VMEM")


def _ffn(parts, m_total, g, wg, wu, wd, g2=None, *, tm, tf):
    d = wg.shape[0]
    ff = wg.shape[1]
    emit_norm = g2 is not None
    row = lambda i, f: (i, 0)
    out_mode = dict(pipeline_mode=pl.Buffered(1))
    in_specs = [_ANY] * len(parts) + [
        pl.BlockSpec((1, d), lambda i, f: (0, 0)),
        pl.BlockSpec((d, tf), lambda i, f: (0, f)),
        pl.BlockSpec((d, tf), lambda i, f: (0, f)),
        pl.BlockSpec((tf, d), lambda i, f: (f, 0)),
    ]
    args = [xp for xp, _ in parts] + [g.reshape(1, d), wg, wu, wd]
    out_shape = [jax.ShapeDtypeStruct((m_total, d), f32)]
    out_specs = [pl.BlockSpec((tm, d), row, **out_mode)]
    if emit_norm:
        in_specs.append(pl.BlockSpec((1, d), lambda i, f: (0, 0)))
        args.append(g2.reshape(1, d))
        out_shape.append(jax.ShapeDtypeStruct((m_total, d), bf16))
        out_specs.append(pl.BlockSpec((tm, d), row, **out_mode))
    outs = pl.pallas_call(
        functools.partial(_ffn_kernel, group_rows=tuple((r0, xp.shape[0]) for xp, r0 in parts),
                          n_tiles=m_total // tm, emit_norm=emit_norm),
        grid=(m_total // tm, ff // tf),
        in_specs=in_specs,
        out_specs=out_specs,
        out_shape=out_shape,
        scratch_shapes=[pltpu.VMEM((tm, d), bf16)],
        compiler_params=_cparams(("arbitrary", "arbitrary")),
        name="ffn",
    )(*args)
    return outs if emit_norm else outs[0]


MM_ROW_SPLIT = 2


A_RING = 3


def _mm_kernel(a_ref, b_ref, *rest, epilogue, col_axis, a_ring):
    if a_ring:
        *extra, o_ref, bw_scr, a_buf, a_sem = rest
    else:
        *extra, o_ref, bw_scr = rest
    j = pl.program_id(col_axis)
    slot = j if col_axis == 1 else 0

    @pl.when(pl.program_id(1 - col_axis) == 0)
    def _():
        bw_scr[slot] = b_ref[...].astype(bf16)

    if a_ring:
        n_rows, tm = pl.num_programs(1), a_buf.shape[1]
        step = pl.program_id(0) * n_rows + pl.program_id(1)
        n_steps = pl.num_programs(0) * n_rows

        def a_copy(s):
            src = a_ref.at[pl.ds(pl.multiple_of(lax.rem(s, n_rows) * tm, tm), tm), :]
            return pltpu.make_async_copy(src, a_buf.at[lax.rem(s, A_RING)], a_sem.at[lax.rem(s, A_RING)])

        @pl.when(step == 0)
        def _():
            for s0 in range(A_RING - 1):
                a_copy(jnp.int32(s0)).start()

        @pl.when(step + (A_RING - 1) < n_steps)
        def _():
            a_copy(step + (A_RING - 1)).start()

        a_copy(step).wait()
        a_tile = a_buf.at[lax.rem(step, A_RING)]
    else:
        a_tile = a_ref

    rows = a_tile.shape[0] // MM_ROW_SPLIT
    for r in range(MM_ROW_SPLIT):
        rs = slice(r * rows, (r + 1) * rows)
        o_ref[rs, :] = epilogue(_dot(a_tile[rs, :], bw_scr[slot]), *extra, rows=rs, col_tile=j).astype(o_ref.dtype)


def _mm(a, b, *, col0, n_cols, out_dtype, epilogue, extras=(), tm=ROW_TILE, tn=1024, resident=True, name="mm"):
    m, k = a.shape
    off = col0 // tn
    nj = n_cols // tn
    scratch = [pltpu.VMEM((nj if resident else 1, k, tn), bf16)]
    if resident:
        grid, order = (m // tm, nj), (lambda f: (lambda i, j: f(j, i)))
        b_map = lambda j, i: (0, jnp.where(i == 0, j, nj - 1) + off)
        a_spec = pl.BlockSpec((tm, k), order(lambda j, i: (i, 0)))
    else:
        grid, order = (nj, m // tm), (lambda f: f)
        b_map = lambda j, i: (0, j + off)
        a_spec = _ANY
        scratch += [pltpu.VMEM((A_RING, tm, k), a.dtype), pltpu.SemaphoreType.DMA((A_RING,))]
        assert nj * (m // tm) >= A_RING
    in_specs = [a_spec, pl.BlockSpec((k, tn), order(b_map))]
    args = [a, b]
    for arr, blk, imap in extras:
        in_specs.append(pl.BlockSpec(blk, order(imap)))
        args.append(arr)
    return pl.pallas_call(
        functools.partial(_mm_kernel, epilogue=epilogue, col_axis=1 if resident else 0, a_ring=not resident),
        grid=grid,
        in_specs=in_specs,
        out_specs=pl.BlockSpec((tm, tn), order(lambda j, i: (i, j))),
        out_shape=jax.ShapeDtypeStruct((m, n_cols), out_dtype),
        scratch_shapes=scratch,
        compiler_params=_cparams(("arbitrary", "arbitrary")),
        name=name,
    )(*args)


def _gelu_tanh(x):
    half_x = 0.5 * x
    return half_x + half_x * jnp.tanh(x * (0.7978845608028654 + 0.7978845608028654 * 0.044715 * (x * x)))


def _rope_epilogue(acc, bias_ref, cos_ref, sin_ref, *, rows, col_tile, k_tile0, k_scale, head_dim):
    z = acc + bias_ref[...]
    cos = cos_ref[rows, :]
    sin = sin_ref[rows, :]
    half = head_dim // 2
    parts = []
    for h0 in range(0, z.shape[1], head_dim):
        x1 = z[:, h0:h0 + half]
        x2 = z[:, h0 + half:h0 + head_dim]
        parts += [x1 * cos - x2 * sin, x2 * cos + x1 * sin]
    scale = jnp.where(col_tile >= k_tile0, k_scale, 1.0).astype(f32)
    return jnp.concatenate(parts, axis=1) * scale


def _lru_gates(xc, n, cs, wa_ref, wx_ref, ba_ref, bx_ref, lam_ref):
    xcb = xc.astype(bf16)
    r = _sigmoid(_dot(xcb, wa_ref[n].astype(bf16)) + ba_ref[:, cs])
    gi = _sigmoid(_dot(xcb, wx_ref[n].astype(bf16)) + bx_ref[:, cs])
    q = r * (LRU_C * jax.nn.softplus(-lam_ref[:, cs]))
    a = _exp_neg(q)
    one_minus = jnp.tanh(q) * (a * a + 1.0)
    return a, one_minus * lax.rsqrt(jnp.maximum(one_minus, 1e-36)), gi


def _lru_prompt_kernel(*refs, n_seq):
    oa_ref = refs[9]

    @pl.when(pl.program_id(0) >= n_seq)
    def _():
        oa_ref[...] = jnp.zeros_like(oa_ref)

    @pl.when(pl.program_id(0) < n_seq)
    def _():
        _lru_prompt_body(*refs)


def _lru_prompt_body(xa_ref, gg_ref, cw_ref, cb_ref, wa_ref, wx_ref, ba_ref, bx_ref, lam_ref,
                     oa_ref, hl_ref, cn_ref, tail_scr, a_scr, b_scr, hc_scr):
    tt, c = xa_ref.shape
    blk = c // N_LRU_BLOCKS
    n_steps = tt // 8
    nw = CONV_W - 1
    t = pl.program_id(1)

    def interleave(x):
        return jnp.swapaxes(x.reshape(8, n_steps, blk), 0, 1).reshape(tt, blk)

    @pl.when(t == 0)
    def _():
        tail_scr[...] = jnp.zeros_like(tail_scr)
        hc_scr[...] = jnp.zeros_like(hc_scr)

    sub = lax.broadcasted_iota(jnp.int32, (8, blk), 0)
    seq_start = lax.broadcasted_iota(jnp.int32, (tt, 1), 0) + t * tt == 0
    for n in range(N_LRU_BLOCKS):
        cs = slice(n * blk, (n + 1) * blk)
        xp = interleave(xa_ref[:, cs])
        lead = []
        for jv in range(nw):
            own = xp[(n_steps - nw + jv) * 8:(n_steps - nw + jv + 1) * 8, :]
            prev = tail_scr[jv * 8:(jv + 1) * 8, cs]
            lead.append(pltpu.roll(jnp.where(sub == 7, prev, own), 1, axis=0))
        xc = cb_ref[:, cs] + cw_ref[nw:nw + 1, cs] * xp
        for s in range(1, CONV_W):
            shifted = jnp.concatenate(lead[nw - s:] + [xp[:tt - 8 * s, :]], axis=0)
            xc = xc + cw_ref[nw - s:nw - s + 1, cs] * shifted
        tail_scr[:, cs] = xp[tt - 8 * nw:, :]
        a, mult, gi = _lru_gates(xc, n, cs, wa_ref, wx_ref, ba_ref, bx_ref, lam_ref)
        mult = jnp.where(seq_start, 1.0, mult)
        a_scr[:, cs] = a
        b_scr[:, cs] = mult * (gi * xc)

    def step(k, carry):
        h, p = carry
        r = pl.ds(pl.multiple_of(k * 8, 8), 8)
        a = a_scr[r, :]
        h = a * h + b_scr[r, :]
        p = a * p
        b_scr[r, :] = h
        a_scr[r, :] = p
        return h, p

    h_end, p_end = lax.fori_loop(0, n_steps, step, (jnp.zeros((8, c), f32), jnp.ones((8, c), f32)), unroll=4)

    sub_c = lax.broadcasted_iota(jnp.int32, (8, c), 0)
    d = 1
    while d < 8:
        keep = sub_c >= d
        h_end = jnp.where(keep, p_end * pltpu.roll(h_end, d, axis=0) + h_end, h_end)
        p_end = jnp.where(keep, p_end * pltpu.roll(p_end, d, axis=0), p_end)
        d *= 2
    h_in = hc_scr[...]
    ends = h_end + p_end * h_in
    starts = jnp.where(sub_c == 0, h_in, pltpu.roll(ends, 1, axis=0))
    hc_scr[...] = ends[7:8, :]
    hl_ref[0] = ends[7:8, :]

    for n in range(N_LRU_BLOCKS):
        cs = slice(n * blk, (n + 1) * blk)
        h3 = b_scr[:, cs].reshape(n_steps, 8, blk) + a_scr[:, cs].reshape(n_steps, 8, blk) * starts[None, :, cs]
        h = jnp.swapaxes(h3, 0, 1).reshape(tt, blk)
        oa_ref[:, cs] = h.astype(bf16) * gg_ref[:, cs]
    cn_ref[0] = xa_ref[tt - nw:tt, :]


def _lru_sample_kernel(xa_ref, gg_ref, scp_ref, h0_ref, cw_ref, cb_ref, wa_ref, wx_ref, ba_ref, bx_ref, lam_ref,
                       oa_prev_ref, oa_ref, hl_ref, cn_ref):
    del oa_prev_ref
    rows, c = xa_ref.shape
    bt, t_len = scp_ref.shape[0], scp_ref.shape[1]
    blk = c // N_LRU_BLOCKS
    tpos = lax.broadcasted_iota(jnp.int32, (bt, t_len, blk), 1)
    for n in range(N_LRU_BLOCKS):
        cs = slice(n * blk, (n + 1) * blk)
        x3 = xa_ref[:, cs].reshape(bt, t_len, blk)
        ext = jnp.concatenate([scp_ref[:, :, cs], x3], axis=1)
        xc3 = cb_ref[:, cs] + cw_ref[CONV_W - 1:CONV_W, cs] * x3
        for s in range(1, CONV_W):
            xc3 = xc3 + cw_ref[CONV_W - 1 - s:CONV_W - s, cs] * pltpu.roll(ext, s, axis=1)[:, t_len:, :]
        xc = xc3.reshape(rows, blk)
        a, mult, gi = _lru_gates(xc, n, cs, wa_ref, wx_ref, ba_ref, bx_ref, lam_ref)
        a3 = a.reshape(bt, t_len, blk)
        b3 = (mult * (gi * xc)).reshape(bt, t_len, blk)
        d = 1
        while d < t_len:
            keep = tpos >= d
            b3 = jnp.where(keep, a3 * pltpu.roll(b3, d, axis=1) + b3, b3)
            a3 = jnp.where(keep, a3 * pltpu.roll(a3, d, axis=1), a3)
            d *= 2
        h3 = b3 + a3 * h0_ref[:, :, cs]
        oa_ref[:, cs] = h3.reshape(rows, blk).astype(bf16) * gg_ref[:, cs]
        hl_ref[:, :, cs] = h3[:, t_len - 1:t_len, :]
        cn_ref[:, :, cs] = pltpu.roll(x3, CONV_W - 1, axis=1)[:, 0:CONV_W - 1, :]


def _ret_norm_gate(o, rn, sg):
    return (o * lax.rsqrt(jnp.mean(o * o, axis=-1, keepdims=True) + EPS) * rn).astype(bf16) * sg


def _ret_prompt_kernel(*refs, n_seq):
    ob_ref = refs[9]

    @pl.when(pl.program_id(0) >= n_seq)
    def _():
        ob_ref[...] = jnp.zeros_like(ob_ref)

    @pl.when(pl.program_id(0) < n_seq)
    def _():
        _ret_prompt_body(*refs)


def _ret_prompt_body(q_ref, k_ref, v_ref, sg_ref, rn_ref, dm_ref, cd_ref, sd_ref, chd_ref, ob_ref, so_ref):
    c = pl.program_id(1)
    dk, dv = so_ref.shape[2], so_ref.shape[3]
    ct = dm_ref.shape[1]

    @pl.when(c == 0)
    def _():
        so_ref[...] = jnp.zeros_like(so_ref)

    for r0 in range(0, q_ref.shape[0], ct):
        rs = slice(r0, r0 + ct)
        for h in range(N_RET_HEADS):
            ks = slice(h * dk, (h + 1) * dk)
            vs = slice(h * dv, (h + 1) * dv)
            q = q_ref[rs, ks]
            k = k_ref[rs, ks]
            v = v_ref[rs, vs]
            s_old = so_ref[0, h]
            scores = _dot_nt(q, k) * dm_ref[h]
            inner = _dot(scores.astype(bf16), v)
            cross = _dot(q, s_old.astype(bf16)) * cd_ref[h]
            kd = (k.astype(f32) * sd_ref[h]).astype(bf16)
            so_ref[0, h] = s_old * chd_ref[h] + _dot_tn(kd, v)
            ob_ref[rs, vs] = _ret_norm_gate(inner + cross, rn_ref[h], sg_ref[rs, vs])


def _ret_sample_body(q_ref, k_ref, v_ref, sg_ref, s_ref, rn_ref, dm_ref, cd_ref, sd_ref, chd_ref, ob_ref, so_ref,
                     *, t_len, head0):
    rows = q_ref.shape[0]
    n_heads, dk, dv = so_ref.shape[1], so_ref.shape[2], so_ref.shape[3]
    row = lax.broadcasted_iota(jnp.int32, (rows, 1), 0)
    for hl in range(n_heads):
        h = head0 + hl
        ks = slice(hl * dk, (hl + 1) * dk)
        vs = slice(hl * dv, (hl + 1) * dv)
        q = q_ref[:, ks]
        k = k_ref[:, ks]
        v = v_ref[:, vs]
        scores = _dot_nt(q, k) * dm_ref[h]
        o = _dot(scores.astype(bf16), v)
        kd = k.astype(f32) * sd_ref[h]
        for bb in range(rows // t_len):
            mine = (row >= bb * t_len) & (row < (bb + 1) * t_len)
            s_old = s_ref[bb, hl]
            cross = _dot(q, s_old.astype(bf16)) * cd_ref[h]
            o = o + jnp.where(mine, cross, 0.0)
            so_ref[bb, hl] = s_old * chd_ref[h] + _dot_tn(jnp.where(mine, kd, 0.0).astype(bf16), v)
        ob_ref[:, vs] = _ret_norm_gate(o, rn_ref[h], sg_ref[:, vs])


def _gates_ret_kernel(a_ref, b_ref, bias_ref, q_ref, k_ref, v_ref, sg_ref, s_hbm, rn_ref, dm_ref, cd_ref, sd_ref,
                      chd_ref, ob_prev_ref, g_ref, ob_ref, so_ref, bw_scr, s_buf, s_sem, *, t_len, n_units):
    del ob_prev_ref
    i, j = pl.program_id(0), pl.program_id(1)
    n_seq, n_heads = s_buf.shape[1], s_buf.shape[2]
    n_parts = N_RET_HEADS // n_heads
    unit = i * pl.num_programs(1) + j

    def state_copy(u):
        slot = lax.rem(u, S_RING)
        src = s_hbm.at[pl.ds((u // n_parts) * n_seq, n_seq), pl.ds(lax.rem(u, n_parts) * n_heads, n_heads)]
        return pltpu.make_async_copy(src, s_buf.at[slot], s_sem.at[slot])

    @pl.when(unit == 0)
    def _():
        for u0 in range(S_RING - 1):
            state_copy(jnp.int32(u0)).start()

    @pl.when(unit + (S_RING - 1) < n_units)
    def _():
        state_copy(unit + (S_RING - 1)).start()

    @pl.when(i == 0)
    def _():
        bw_scr[j] = b_ref[...].astype(bf16)

    rows = a_ref.shape[0] // MM_ROW_SPLIT
    for r in range(MM_ROW_SPLIT):
        rs = slice(r * rows, (r + 1) * rows)
        g_ref[rs, :] = _sigmoid(_dot(a_ref[rs, :], bw_scr[j]) + bias_ref[...]).astype(g_ref.dtype)

    state_copy(unit).wait()
    _ret_sample_body(q_ref, k_ref, v_ref, sg_ref, s_buf.at[lax.rem(unit, S_RING)], rn_ref, dm_ref, cd_ref, sd_ref,
                     chd_ref, ob_ref, so_ref, t_len=t_len, head0=lax.rem(unit, n_parts) * n_heads)


def _decay_tables(chunk, n_seq):
    log_g = jnp.log1p(-jnp.exp2(-5.0 - jnp.arange(N_RET_HEADS, dtype=f32)))
    r = jnp.arange(chunk * n_seq)
    idx = (r % chunk).astype(f32)
    same = (r[:, None] // chunk) == (r[None, :] // chunk)
    diff = idx[:, None] - idx[None, :]
    dmask = jnp.where(same & (diff >= 0), jnp.exp(jnp.maximum(diff, 0.0)[None] * log_g[:, None, None]), 0.0)
    cross_decay = jnp.exp((idx[None] + 1.0) * log_g[:, None])[..., None]
    state_decay = jnp.exp((chunk - 1.0 - idx[None]) * log_g[:, None])[..., None]
    chunk_decay = jnp.exp(chunk * log_g)[:, None, None]
    return dmask, cross_decay, state_decay, chunk_decay


def _ple_kernel(x_ref, pep_ref, pes_ref, gn_ref, wg_ref, bg_ref, pp_ref, fn_ref, yp_ref, ys_ref, *, n_prompt_tiles, final):
    i = pl.program_id(0)
    x = x_ref[...]
    u = _rms(x, gn_ref[...]).astype(bf16)
    gate = _sigmoid(_dot(u, wg_ref[...]) + bg_ref[...])
    pe = jnp.where(i < n_prompt_tiles, pep_ref[...], pes_ref[...]).astype(bf16)
    x = x + gate * _dot(pe, pp_ref[...].astype(bf16))
    y = _rms(x, fn_ref[...]) if final else x

    @pl.when(i < n_prompt_tiles)
    def _():
        yp_ref[...] = y

    @pl.when(i >= n_prompt_tiles)
    def _():
        ys_ref[...] = y


def _layer(x_parts, pe_p, pe_s, n_prompt, seq, n_sample, dec_seq, h0_s, conv_s, ret_s, cos_t, sin_t, lp, final_norm):
    d = lp['mix_norm'].shape[0]
    mp, ms = n_prompt * seq, n_sample * dec_seq
    m = mp + ms
    d_rnn = lp['conv_w'].shape[1]
    hh = N_RET_HEADS
    dk = d // hh
    dv = lp['ret_norm'].shape[1]
    qk_dim, v_dim = hh * dk, hh * dv
    tm, tn = ROW_TILE, 1024

    ff = lp['ffn1_wg'].shape[1]
    tm1, tf1 = _ffn_tiles(m, d, ff, with_norm=True)
    x1, u = _ffn(x_parts, m, lp['ffn1_norm'], lp['ffn1_wg'], lp['ffn1_wu'], lp['ffn1_wd'], lp['mix_norm'],
                 tm=tm1, tf=tf1)

    w_in = lp['w_in']
    b_in = lp['b_in'].reshape(1, -1)

    def bias_extra(col0):
        return (b_in, (1, tn), lambda j, i, o=col0 // tn: (0, j + o))

    def with_bias(fn):
        return lambda acc, b_ref, rows, col_tile: fn(acc + b_ref[...])

    c_xa, c_ga, c_q, c_v = 0, d_rnn, 2 * d_rnn, 2 * d_rnn + 2 * qk_dim
    c_gr, c_gate = c_v + v_dim, c_v + 2 * v_dim
    tmi = IN_ROW_TILE if m % IN_ROW_TILE == 0 else tm
    proj = functools.partial(_mm, u, w_in, tm=tmi, tn=tn)
    xa = proj(col0=c_xa, n_cols=d_rnn, out_dtype=f32, epilogue=with_bias(lambda z: z),
              extras=[bias_extra(c_xa)], name="inproj_xa")
    gg = proj(col0=c_ga, n_cols=d_rnn, out_dtype=bf16, epilogue=with_bias(_gelu_tanh),
              extras=[bias_extra(c_ga)], name="inproj_ga")
    half = dk // 2
    rope_map = lambda j, i: (i, 0)
    qk = proj(col0=c_q, n_cols=2 * qk_dim, out_dtype=bf16,
              epilogue=functools.partial(_rope_epilogue, k_tile0=qk_dim // tn, k_scale=dk ** -0.5, head_dim=dk),
              extras=[bias_extra(c_q), (cos_t, (tmi, half), rope_map), (sin_t, (tmi, half), rope_map)],
              name="inproj_qk")
    v = proj(col0=c_v, n_cols=v_dim, out_dtype=bf16, epilogue=with_bias(lambda z: z),
             extras=[bias_extra(c_v)], name="inproj_v")
    sg = proj(col0=c_gr, n_cols=v_dim, out_dtype=bf16, epilogue=with_bias(lambda z: z * _sigmoid(z)),
              extras=[bias_extra(c_gr)], name="inproj_gr")

    row2 = lambda a: a.reshape(1, -1)
    lru_w = [lp['conv_w'], row2(lp['conv_b']), lp['lru_wa'], lp['lru_wx'], row2(lp['lru_ba']), row2(lp['lru_bx']),
             row2(lp['lru_lambda'])]
    nb = N_LRU_BLOCKS
    blk = d_rnn // nb
    z2 = lambda *_: (0, 0)
    z3 = lambda *_: (0, 0, 0)
    lru_w_specs = [pl.BlockSpec((CONV_W, d_rnn), z2), pl.BlockSpec((1, d_rnn), z2),
                   pl.BlockSpec((nb, blk, blk), z3), pl.BlockSpec((nb, blk, blk), z3),
                   pl.BlockSpec((1, d_rnn), z2), pl.BlockSpec((1, d_rnn), z2), pl.BlockSpec((1, d_rnn), z2)]

    tt = 512
    ntt = seq // tt
    assert ms % tt == 0 and ms // tt <= ntt
    seq_rows = lambda b, t: (jnp.minimum(b * ntt + t, mp // tt - 1), 0)
    all_rows = lambda b, t: (jnp.minimum(b * ntt + t, m // tt - 1), 0)
    per_seq = lambda b, t: (jnp.minimum(b, n_prompt - 1), 0, 0)
    oa, hl_p, cn_p = pl.pallas_call(
        functools.partial(_lru_prompt_kernel, n_seq=n_prompt),
        grid=(n_prompt + 1, ntt),
        in_specs=[pl.BlockSpec((tt, d_rnn), seq_rows), pl.BlockSpec((tt, d_rnn), seq_rows)] + lru_w_specs,
        out_specs=[pl.BlockSpec((tt, d_rnn), all_rows), pl.BlockSpec((1, 1, d_rnn), per_seq),
                   pl.BlockSpec((1, CONV_W - 1, d_rnn), per_seq)],
        out_shape=[jax.ShapeDtypeStruct((m, d_rnn), bf16), jax.ShapeDtypeStruct((n_prompt, 1, d_rnn), f32),
                   jax.ShapeDtypeStruct((n_prompt, CONV_W - 1, d_rnn), f32)],
        scratch_shapes=[pltpu.VMEM((8 * (CONV_W - 1), d_rnn), f32), pltpu.VMEM((tt, d_rnn), f32),
                        pltpu.VMEM((tt, d_rnn), f32), pltpu.VMEM((1, d_rnn), f32)],
        compiler_params=_cparams(("arbitrary", "arbitrary")),
        name="lru_prompt",
    )(xa, gg, *lru_w)

    bt = 32
    rows_s = bt * dec_seq
    s_off = mp // rows_s
    scp = jnp.pad(conv_s, ((0, 0), (dec_seq - (CONV_W - 1), 0), (0, 0)))
    samp_rows = lambda i: (s_off + i, 0)
    samp3 = lambda i: (i, 0, 0)
    oa, hl_s, cn_s = pl.pallas_call(
        _lru_sample_kernel,
        grid=(n_sample // bt,),
        in_specs=[pl.BlockSpec((rows_s, d_rnn), samp_rows), pl.BlockSpec((rows_s, d_rnn), samp_rows),
                  pl.BlockSpec((bt, dec_seq, d_rnn), samp3), pl.BlockSpec((bt, 1, d_rnn), samp3)] + lru_w_specs + [_ANY],
        out_specs=[pl.BlockSpec((rows_s, d_rnn), samp_rows), pl.BlockSpec((bt, 1, d_rnn), samp3),
                   pl.BlockSpec((bt, CONV_W - 1, d_rnn), samp3)],
        out_shape=[jax.ShapeDtypeStruct((m, d_rnn), bf16), jax.ShapeDtypeStruct((n_sample, 1, d_rnn), f32),
                   jax.ShapeDtypeStruct((n_sample, CONV_W - 1, d_rnn), f32)],
        input_output_aliases={11: 0},
        compiler_params=_cparams(("parallel",)),
        name="lru_sample",
    )(xa, gg, scp, h0_s.reshape(n_sample, 1, d_rnn), *lru_w, oa)

    rn = lp['ret_norm'].reshape(hh, 1, dv)
    tabs = _decay_tables(RET_CHUNK, 1)
    ct = RET_STEP_CHUNKS * RET_CHUNK
    nct = seq // ct
    full3 = lambda shape: pl.BlockSpec(shape, lambda *_: (0, 0, 0))
    tab_specs = lambda r: [full3((hh, 1, dv)), full3((hh, r, r)), full3((hh, r, 1)), full3((hh, r, 1)), full3((hh, 1, 1))]
    assert ms % ct == 0 and ms // ct <= nct
    chunk_row = lambda b, c: jnp.minimum(b * nct + c, mp // ct - 1)
    ob, so_p = pl.pallas_call(
        functools.partial(_ret_prompt_kernel, n_seq=n_prompt),
        grid=(n_prompt + 1, nct),
        in_specs=[pl.BlockSpec((ct, qk_dim), lambda b, c: (chunk_row(b, c), 0)),
                  pl.BlockSpec((ct, qk_dim), lambda b, c: (chunk_row(b, c), 1)),
                  pl.BlockSpec((ct, v_dim), lambda b, c: (chunk_row(b, c), 0)),
                  pl.BlockSpec((ct, v_dim), lambda b, c: (chunk_row(b, c), 0))] + tab_specs(RET_CHUNK),
        out_specs=[pl.BlockSpec((ct, v_dim), lambda b, c: (jnp.minimum(b * nct + c, m // ct - 1), 0)),
                   pl.BlockSpec((1, hh, dk, dv), lambda b, c: (jnp.minimum(b, n_prompt - 1), 0, 0, 0))],
        out_shape=[jax.ShapeDtypeStruct((m, v_dim), bf16), jax.ShapeDtypeStruct((n_prompt, hh, dk, dv), f32)],
        compiler_params=_cparams(("arbitrary", "arbitrary")),
        name="ret_prompt",
    )(qk, qk, v, sg, rn, *tabs)

    pr = SAMPLE_PAIR
    rows_r = pr * dec_seq
    r_off = mp // rows_r
    tabs = _decay_tables(dec_seq, pr)
    n_parts = 2
    hp = hh // n_parts
    n_units = (n_sample // pr) * n_parts
    tng = GATE_COL_TILE
    njg = 2 * d // tng
    nig = n_units // njg
    tmg = m // nig
    assert nig * njg == n_units >= S_RING and nig * tmg == m and tmg % 16 == 0 and njg % n_parts == 0
    unit = lambda i, j: i * njg + j
    seqs = lambda i, j: unit(i, j) // n_parts
    part = lambda i, j: unit(i, j) % n_parts
    qk_parts = qk_dim // (hp * dk)
    gates, ob, so_s = pl.pallas_call(
        functools.partial(_gates_ret_kernel, t_len=dec_seq, n_units=n_units),
        grid=(nig, njg),
        in_specs=[pl.BlockSpec((tmg, d), lambda i, j: (i, 0)),
                  pl.BlockSpec((d, tng), lambda i, j: (0, jnp.where(i == 0, j, njg - 1) + c_gate // tng)),
                  pl.BlockSpec((1, tng), lambda i, j: (0, j + c_gate // tng)),
                  pl.BlockSpec((rows_r, hp * dk), lambda i, j: (r_off + seqs(i, j), part(i, j))),
                  pl.BlockSpec((rows_r, hp * dk), lambda i, j: (r_off + seqs(i, j), qk_parts + part(i, j))),
                  pl.BlockSpec((rows_r, hp * dv), lambda i, j: (r_off + seqs(i, j), part(i, j))),
                  pl.BlockSpec((rows_r, hp * dv), lambda i, j: (r_off + seqs(i, j), part(i, j))),
                  _ANY] + tab_specs(rows_r) + [_ANY],
        out_specs=[pl.BlockSpec((tmg, tng), lambda i, j: (i, j)),
                   pl.BlockSpec((rows_r, hp * dv), lambda i, j: (r_off + seqs(i, j), part(i, j))),
                   pl.BlockSpec((pr, hp, dk, dv), lambda i, j: (seqs(i, j), part(i, j), 0, 0))],
        out_shape=[jax.ShapeDtypeStruct((m, 2 * d), bf16), jax.ShapeDtypeStruct((m, v_dim), bf16),
                   jax.ShapeDtypeStruct((n_sample, hh, dk, dv), f32)],
        scratch_shapes=[pltpu.VMEM((njg, d, tng), bf16), pltpu.VMEM((S_RING, pr, hp, dk, dv), f32),
                        pltpu.SemaphoreType.DMA((S_RING,))],
        input_output_aliases={13: 1},
        compiler_params=_cparams(("arbitrary", "arbitrary")),
        name="gates_ret_sample",
    )(u, w_in, b_in, qk, qk, v, sg, ret_s, rn, *tabs, ob)

    tile = lambda j, i: (i, j)
    pa = _mm(oa, lp['proj_a'], col0=0, n_cols=d, out_dtype=f32, epilogue=lambda acc, rows, col_tile: acc,
             name="proj_a")
    tnb = 512
    merged = _mm(ob, lp['proj_b'], col0=0, n_cols=d, out_dtype=bf16, tn=tnb, resident=False,
                 epilogue=lambda acc, pa_ref, ga_ref, gb_ref, rows, col_tile: (
                     ga_ref[rows, :].astype(f32) * pa_ref[rows, :] + gb_ref[rows, :].astype(f32) * acc),
                 extras=[(pa, (tm, tnb), tile), (gates, (tm, tnb), tile),
                         (gates, (tm, tnb), lambda j, i: (i, j + d // tnb))], name="proj_b_merge")
    x2 = _mm(merged, lp['w_out'], col0=0, n_cols=d, out_dtype=f32,
             epilogue=lambda acc, x_ref, rows, col_tile: x_ref[rows, :] + acc,
             extras=[(x1, (tm, tn), tile)], name="w_out")

    tm2, tf2 = _ffn_tiles(m, d, ff, with_norm=False)
    x3 = _ffn([(x2, 0)], m, lp['ffn2_norm'], lp['ffn2_wg'], lp['ffn2_wu'], lp['ffn2_wd'], tm=tm2, tf=tf2)

    tp = 512
    ple_dim = pe_p.shape[1]
    fin = final_norm is not None
    fn = (final_norm if fin else lp['ple_norm']).reshape(1, d)
    npt = mp // tp
    c2 = lambda i: (0, 0)
    prow = lambda i: (jnp.minimum(i, npt - 1), 0)
    srow = lambda i: (jnp.maximum(i - npt, 0), 0)
    y_p, y_s = pl.pallas_call(
        functools.partial(_ple_kernel, n_prompt_tiles=npt, final=fin),
        grid=(m // tp,),
        in_specs=[pl.BlockSpec((tp, d), lambda i: (i, 0)), pl.BlockSpec((tp, ple_dim), prow),
                  pl.BlockSpec((tp, ple_dim), srow), pl.BlockSpec((1, d), c2), pl.BlockSpec((d, d), c2),
                  pl.BlockSpec((1, d), c2), pl.BlockSpec((ple_dim, d), c2), pl.BlockSpec((1, d), c2)],
        out_specs=[pl.BlockSpec((tp, d), prow), pl.BlockSpec((tp, d), srow)],
        out_shape=[jax.ShapeDtypeStruct((mp, d), f32), jax.ShapeDtypeStruct((ms, d), f32)],
        compiler_params=_cparams(("arbitrary",)),
        name="ple",
    )(x3, pe_p, pe_s, lp['ple_norm'].reshape(1, d), _to_bf16(lp['ple_wg']), lp['ple_bg'].reshape(1, d),
      lp['ple_proj'], fn)

    states = (hl_p.reshape(n_prompt, d_rnn), cn_p, so_p, hl_s.reshape(n_sample, d_rnn), cn_s, so_s)
    return y_p, y_s, states


def kernel(x_prompt, x_sample, p_prompt, p_sample, state_lru, state_conv, state_ret, ffn1_norm, ffn1_wg, ffn1_wu, ffn1_wd, mix_norm, w_in, b_in, conv_w, conv_b, lru_wa, lru_ba, lru_wx, lru_bx, lru_lambda, ret_norm, proj_a, proj_b, w_out, ffn2_norm, ffn2_wg, ffn2_wu, ffn2_wd, ple_norm, ple_wg, ple_bg, ple_proj, final_norm):
    params = dict(ffn1_norm=ffn1_norm, ffn1_wg=ffn1_wg, ffn1_wu=ffn1_wu, ffn1_wd=ffn1_wd, mix_norm=mix_norm,
                  w_in=w_in, b_in=b_in, conv_w=conv_w, conv_b=conv_b, lru_wa=lru_wa, lru_ba=lru_ba, lru_wx=lru_wx,
                  lru_bx=lru_bx, lru_lambda=lru_lambda, ret_norm=ret_norm, proj_a=proj_a, proj_b=proj_b,
                  w_out=w_out, ffn2_norm=ffn2_norm, ffn2_wg=ffn2_wg, ffn2_wu=ffn2_wu, ffn2_wd=ffn2_wd,
                  ple_norm=ple_norm, ple_wg=ple_wg, ple_bg=ple_bg, ple_proj=ple_proj)
    depth = w_in.shape[0]
    n_prompt, seq, d = x_prompt.shape
    n_sample, dec_seq, _ = x_sample.shape
    mp, ms = n_prompt * seq, n_sample * dec_seq
    dk = d // N_RET_HEADS

    y_p = x_prompt.astype(f32).reshape(mp, d)
    y_s = x_sample.astype(f32).reshape(ms, d)
    cos_t, sin_t = _rope_table(seq, dec_seq, ms, dk // 2)
    cos_t, sin_t = (jnp.concatenate([jnp.tile(t[:seq], (n_prompt, 1)), t[seq:]], axis=0) for t in (cos_t, sin_t))
    outs = [[] for _ in range(6)]
    for i in range(depth):
        lp = {k: v[i].astype(f32) for k, v in params.items()}
        y_p, y_s, st = _layer([(y_p, 0), (y_s, mp)], p_prompt[i].astype(f32).reshape(mp, -1),
                              p_sample[i].astype(f32).reshape(ms, -1), n_prompt, seq, n_sample, dec_seq,
                              state_lru[i].astype(f32), state_conv[i].astype(f32), state_ret[i].astype(f32),
                              cos_t, sin_t, lp, final_norm.astype(f32) if i == depth - 1 else None)
        for o, s in zip(outs, st):
            o.append(s)
    y_prompt = y_p.reshape(n_prompt, seq, d).astype(x_prompt.dtype)
    y_sample = y_s.reshape(n_sample, dec_seq, d).astype(x_sample.dtype)
    lru_p, conv_p, ret_p, lru_s, conv_s, ret_s = (jnp.stack(o) for o in outs)
    return (y_prompt, y_sample, lru_p.astype(state_lru.dtype), conv_p.astype(state_conv.dtype),
            ret_p.astype(state_ret.dtype), lru_s.astype(state_lru.dtype), conv_s.astype(state_conv.dtype),
            ret_s.astype(state_ret.dtype))
```
